```python
import math
import jax, jax.numpy as jnp
from jax import lax
import numpy as np

D_MODEL = 1024
BATCH = 16
SEQ = 4096
DEPTH = 4

GRID_W = 64
CTX_LEN = 256
N_MIXERS = 3
HEAD_DIM = 64
BLOCK = 128
A_Q_HEADS = D_MODEL // HEAD_DIM
A_KV_HEADS = A_Q_HEADS // 4
A_WINDOW = 128
B_HEADS = D_MODEL // (2 * HEAD_DIM)
C_HEADS = D_MODEL // HEAD_DIM
NA_ROWS = 8
NA_COLS = 16
ROPE_BASE = 10000.0
ROPE_AXIS_DIM = HEAD_DIM // 2
N_EXPERTS = 16
EXPERT_FF = 2 * D_MODEL
CAPACITY_FACTOR = 2
RMS_EPS = 1e-6
NEG_INF = -1e30

kernel_name = "hybrid_interleaved_diffusion_block"


def rmsnorm(x, g):
    xf = x.astype(jnp.float32)
    y = xf * lax.rsqrt(jnp.mean(xf * xf, axis=-1, keepdims=True) + RMS_EPS)
    return (y * g.astype(jnp.float32)).astype(x.dtype)


def modulate(h, shift, scale):
    return h * (1 + scale) + shift


def rope_tables(n_tokens, dtype):
    t = jnp.arange(n_tokens)
    row = (t // GRID_W).astype(jnp.float32)
    col = (t % GRID_W).astype(jnp.float32)
    inv = 1.0 / (ROPE_BASE ** (jnp.arange(0, ROPE_AXIS_DIM, 2, dtype=jnp.float32) / ROPE_AXIS_DIM))
    ang = jnp.stack([row[:, None] * inv, col[:, None] * inv], axis=1)
    return jnp.cos(ang).astype(dtype), jnp.sin(ang).astype(dtype)


def apply_rope(x, cos, sin):
    shp = x.shape
    xr = x.reshape(shp[:-1] + (2, 2, ROPE_AXIS_DIM // 2))
    x1 = xr[..., 0, :]
    x2 = xr[..., 1, :]
    out = jnp.stack([x1 * cos - x2 * sin, x2 * cos + x1 * sin], axis=-2)
    return out.reshape(shp)


def sink_softmax(s, sink_l):
    m = jnp.maximum(jnp.max(s, axis=-1, keepdims=True), sink_l)
    e = jnp.exp(s - m)
    return e / (jnp.sum(e, axis=-1, keepdims=True) + jnp.exp(sink_l - m))


def window_gqa_mixer(hx, hc, w_in, w_out, sink, with_ctx):
    B, T, _ = hx.shape
    L = hc.shape[1]
    G = A_Q_HEADS // A_KV_HEADS
    nq = A_Q_HEADS * HEAD_DIM
    nk = A_KV_HEADS * HEAD_DIM

    def proj(h):
        n = h.shape[1]
        p = h @ w_in
        q = p[..., :nq].reshape(B, n, A_KV_HEADS, G, HEAD_DIM).transpose(0, 2, 3, 1, 4) * HEAD_DIM ** -0.5
        k = p[..., nq:nq + nk].reshape(B, n, A_KV_HEADS, HEAD_DIM).transpose(0, 2, 1, 3)
        v = p[..., nq + nk:].reshape(B, n, A_KV_HEADS, HEAD_DIM).transpose(0, 2, 1, 3)
        return q, k, v

    q, k, v = proj(hx)
    qc, kc, vc = proj(hc)
    cos, sin = rope_tables(T, hx.dtype)
    q = apply_rope(q, cos, sin)
    k = apply_rope(k, cos, sin)
    sink_l = sink.astype(jnp.float32).reshape(1, A_KV_HEADS, G, 1, 1)

    nb = T // BLOCK
    pad = ((0, 0), (0, 0), (BLOCK, BLOCK), (0, 0))
    k_pad = jnp.pad(k, pad)
    v_pad = jnp.pad(v, pad)
    q_blk = q.reshape(B, A_KV_HEADS, G, nb, BLOCK, HEAD_DIM).transpose(3, 0, 1, 2, 4, 5)
    ctx_valid = jnp.ones((BLOCK, L), dtype=bool)

    def block(args):
        j, qb = args
        start = j * BLOCK
        kb = jnp.concatenate([kc, lax.dynamic_slice_in_dim(k_pad, start, 3 * BLOCK, axis=2)], axis=2)
        vb = jnp.concatenate([vc, lax.dynamic_slice_in_dim(v_pad, start, 3 * BLOCK, axis=2)], axis=2)
        qpos = start + jnp.arange(BLOCK)
        kpos = start - BLOCK + jnp.arange(3 * BLOCK)
        band = (jnp.abs(qpos[:, None] - kpos[None, :]) <= A_WINDOW) & (kpos >= 0) & (kpos < T)
        valid = jnp.concatenate([ctx_valid, band], axis=1)
        s = jnp.einsum('bhgqd,bhkd->bhgqk', qb, kb).astype(jnp.float32)
        p = sink_softmax(jnp.where(valid, s, NEG_INF), sink_l)
        return jnp.einsum('bhgqk,bhkd->bhgqd', p.astype(vb.dtype), vb)

    y = lax.map(block, (jnp.arange(nb), q_blk))
    y = y.transpose(1, 0, 4, 2, 3, 5).reshape(B, T, nq)
    out_x = y @ w_out
    out_c = None
    if with_ctx:
        s = jnp.einsum('bhgqd,bhkd->bhgqk', qc, kc).astype(jnp.float32)
        p = sink_softmax(s, sink_l)
        yc = jnp.einsum('bhgqk,bhkd->bhgqd', p.astype(vc.dtype), vc)
        out_c = yc.transpose(0, 3, 1, 2, 4).reshape(B, L, nq) @ w_out
    return out_x, out_c


def diff_attn_mixer(hx, hc, w_in, w_out, lam_params, subln_g, lambda_init, with_ctx):
    B, T, _ = hx.shape
    L = hc.shape[1]
    nqk = 2 * B_HEADS * HEAD_DIM
    dv = 2 * HEAD_DIM

    def proj(h):
        n = h.shape[1]
        p = h @ w_in
        q = p[..., :nqk].reshape(B, n, B_HEADS, 2, HEAD_DIM).transpose(0, 2, 3, 1, 4) * HEAD_DIM ** -0.5
        k = p[..., nqk:2 * nqk].reshape(B, n, B_HEADS, 2, HEAD_DIM).transpose(0, 2, 3, 1, 4)
        v = p[..., 2 * nqk:].reshape(B, n, B_HEADS, dv).transpose(0, 2, 1, 3)
        return q, k, v

    q, k, v = proj(hx)
    qc, kc, vc = proj(hc)
    cos, sin = rope_tables(T, hx.dtype)
    q = apply_rope(q, cos, sin)
    k = apply_rope(k, cos, sin)
    lp = lam_params.astype(jnp.float32)
    lam = jnp.exp(jnp.sum(lp[0] * lp[1])) - jnp.exp(jnp.sum(lp[2] * lp[3])) + lambda_init

    def diff_av(qb, kk, vv):
        p = jax.nn.softmax(jnp.einsum('bhiqd,bhikd->bhiqk', qb, kk).astype(jnp.float32), axis=-1)
        a = p[:, :, 0] - lam * p[:, :, 1]
        return jnp.einsum('bhqk,bhkd->bhqd', a.astype(vv.dtype), vv)

    def finish(y):
        y = rmsnorm(y, subln_g) * (1.0 - lambda_init)
        return y.reshape(B, y.shape[1], B_HEADS * dv) @ w_out

    k_all = jnp.concatenate([kc, k], axis=3)
    v_all = jnp.concatenate([vc, v], axis=2)
    nb = T // BLOCK
    q_blk = q.reshape(B, B_HEADS, 2, nb, BLOCK, HEAD_DIM).transpose(3, 0, 1, 2, 4, 5)
    y = lax.map(lambda qb: diff_av(qb, k_all, v_all), q_blk)
    y = y.transpose(1, 0, 3, 2, 4).reshape(B, T, B_HEADS, dv)
    out_x = finish(y)
    out_c = None
    if with_ctx:
        out_c = finish(diff_av(qc, kc, vc).transpose(0, 2, 1, 3))
    return out_x, out_c


def neighbourhood_mixer(hx, hc, w_in, w_out, rpb, with_ctx):
    B, T, _ = hx.shape
    L = hc.shape[1]
    nh = C_HEADS * HEAD_DIM

    def proj(h):
        n = h.shape[1]
        p = h @ w_in
        q = p[..., :nh].reshape(B, n, C_HEADS, HEAD_DIM).transpose(0, 2, 1, 3) * HEAD_DIM ** -0.5
        k = p[..., nh:2 * nh].reshape(B, n, C_HEADS, HEAD_DIM).transpose(0, 2, 1, 3)
        v = p[..., 2 * nh:].reshape(B, n, C_HEADS, HEAD_DIM).transpose(0, 2, 1, 3)
        return q, k, v

    q, k, v = proj(hx)
    qc, kc, vc = proj(hc)
    rows = T // GRID_W
    wr = min(NA_ROWS, rows)
    wc = min(NA_COLS, GRID_W)
    qg = q.reshape(B, C_HEADS, rows, GRID_W, HEAD_DIM)
    kg = k.reshape(B, C_HEADS, rows, GRID_W, HEAD_DIM)
    vg = v.reshape(B, C_HEADS, rows, GRID_W, HEAD_DIM)
    cols = jnp.arange(GRID_W)
    col_idx = jnp.clip(cols - wc // 2, 0, GRID_W - wc)[:, None] + jnp.arange(wc)[None, :]
    col_bias_idx = col_idx - cols[:, None] + NA_COLS - 1
    rpb_c = rpb[:, :, col_bias_idx]

    def row_block(args):
        r, qr = args
        rs = jnp.clip(r - wr // 2, 0, rows - wr)
        kr = lax.dynamic_slice_in_dim(kg, rs, wr, axis=2)
        vr = lax.dynamic_slice_in_dim(vg, rs, wr, axis=2)
        k_nb = kr[:, :, :, col_idx]
        v_nb = vr[:, :, :, col_idx]
        row_bias_idx = rs + jnp.arange(wr) - r + NA_ROWS - 1
        bias = rpb_c[:, row_bias_idx].transpose(0, 2, 1, 3)
        s_nb = jnp.einsum('bhqd,bhrqcd->bhqrc', qr, k_nb).astype(jnp.float32) + bias.astype(jnp.float32)
        s_ctx = jnp.einsum('bhqd,bhkd->bhqk', qr, kc).astype(jnp.float32)
        logits = jnp.concatenate([s_ctx, s_nb.reshape(B, C_HEADS, GRID_W, wr * wc)], axis=-1)
        p = jax.nn.softmax(logits, axis=-1).astype(vr.dtype)
        p_nb = p[..., L:].reshape(B, C_HEADS, GRID_W, wr, wc)
        return (jnp.einsum('bhqk,bhkd->bhqd', p[..., :L], vc)
                + jnp.einsum('bhqrc,bhrqcd->bhqd', p_nb, v_nb))

    y = lax.map(row_block, (jnp.arange(rows), qg.transpose(2, 0, 1, 3, 4)))
    y = y.transpose(1, 0, 3, 2, 4).reshape(B, T, nh)
    out_x = y @ w_out
    out_c = None
    if with_ctx:
        p = jax.nn.softmax(jnp.einsum('bhqd,bhkd->bhqk', qc, kc).astype(jnp.float32), axis=-1)
        yc = jnp.einsum('bhqk,bhkd->bhqd', p.astype(vc.dtype), vc)
        out_c = yc.transpose(0, 2, 1, 3).reshape(B, L, nh) @ w_out
    return out_x, out_c


def ec_moe(h, router_w, w_gate, w_up, w_down):
    B, N, _ = h.shape
    cap = CAPACITY_FACTOR * N // N_EXPERTS
    aff = jax.nn.softmax(jnp.einsum('bnd,de->bne', h, router_w).astype(jnp.float32), axis=-1)
    gate, idx = lax.top_k(jnp.swapaxes(aff, 1, 2), cap)
    bidx = jnp.arange(B)[:, None, None]
    xin = h[bidx, idx]
    hid = jax.nn.silu(jnp.einsum('becd,edf->becf', xin, w_gate)) * jnp.einsum('becd,edf->becf', xin, w_up)
    y = jnp.einsum('becf,efd->becd', hid, w_down) * gate[..., None].astype(h.dtype)
    return jnp.zeros_like(h).at[bidx, idx].add(y)


def setup_inputs(seed: int = 0) -> dict:
    key = jax.random.key(seed)
    ks = jax.random.split(key, 24)
    D = D_MODEL
    n_win = len(range(0, DEPTH, N_MIXERS))
    n_diff = len(range(1, DEPTH, N_MIXERS))
    n_na = len(range(2, DEPTH, N_MIXERS))
    win_cols = (A_Q_HEADS + 2 * A_KV_HEADS) * HEAD_DIM
    diff_cols = 3 * 2 * B_HEADS * HEAD_DIM
    na_cols = 3 * C_HEADS * HEAD_DIM

    def nrm(k, shape, s):
        return jax.random.normal(k, shape, jnp.float32) * s

    return {
        "x": nrm(ks[0], (BATCH, SEQ, D), 1.0),
        "c": nrm(ks[1], (BATCH, D), 1.0),
        "ctx": nrm(ks[2], (BATCH, CTX_LEN, D), 1.0),
        "c_ctx": nrm(ks[3], (D,), 1.0),
        "ada_w": nrm(ks[4], (DEPTH, D, 6 * D), 0.5 * D ** -0.5),
        "ada_b": nrm(ks[5], (DEPTH, 6 * D), 0.02),
        "norm1_g": 1.0 + nrm(ks[6], (DEPTH, D), 0.02),
        "norm2_g": 1.0 + nrm(ks[7], (DEPTH, D), 0.02),
        "final_g": 1.0 + nrm(ks[8], (D,), 0.02),
        "win_w_in": nrm(ks[9], (n_win, D, win_cols), D ** -0.5),
        "win_w_out": nrm(ks[10], (n_win, A_Q_HEADS * HEAD_DIM, D), (A_Q_HEADS * HEAD_DIM) ** -0.5),
        "win_sink": nrm(ks[11], (n_win, A_Q_HEADS), 0.5),
        "diff_w_in": nrm(ks[12], (n_diff, D, diff_cols), D ** -0.5),
        "diff_w_out": nrm(ks[13], (n_diff, 2 * B_HEADS * HEAD_DIM, D), (2 * B_HEADS * HEAD_DIM) ** -0.5),
        "diff_lambda": nrm(ks[14], (n_diff, 4, HEAD_DIM), 0.1),
        "diff_subln_g": 1.0 + nrm(ks[15], (n_diff, 2 * HEAD_DIM), 0.02),
        "na_w_in": nrm(ks[16], (n_na, D, na_cols), D ** -0.5),
        "na_w_out": nrm(ks[17], (n_na, C_HEADS * HEAD_DIM, D), (C_HEADS * HEAD_DIM) ** -0.5),
        "na_rpb": nrm(ks[18], (n_na, C_HEADS, 2 * NA_ROWS - 1, 2 * NA_COLS - 1), 0.1),
        "router_w": nrm(ks[19], (DEPTH, D, N_EXPERTS), D ** -0.5),
        "w_gate": nrm(ks[20], (DEPTH, N_EXPERTS, D, EXPERT_FF), D ** -0.5),
        "w_up": nrm(ks[21], (DEPTH, N_EXPERTS, D, EXPERT_FF), D ** -0.5),
        "w_down": nrm(ks[22], (DEPTH, N_EXPERTS, EXPERT_FF, D), EXPERT_FF ** -0.5),
    }


def reference(x, c, ctx, c_ctx, ada_w, ada_b, norm1_g, norm2_g, final_g,
              win_w_in, win_w_out, win_sink,
              diff_w_in, diff_w_out, diff_lambda, diff_subln_g,
              na_w_in, na_w_out, na_rpb,
              router_w, w_gate, w_up, w_down):
    for i in range(DEPTH):
        kind = i % N_MIXERS
        slot = i // N_MIXERS
        with_ctx = i < DEPTH - 1
        mod_x = (jax.nn.silu(c) @ ada_w[i] + ada_b[i])[:, None, :]
        mod_c = jax.nn.silu(c_ctx) @ ada_w[i] + ada_b[i]
        sx1, cx1, gx1, sx2, cx2, gx2 = jnp.split(mod_x, 6, axis=-1)
        sc1, cc1, gc1, sc2, cc2, gc2 = jnp.split(mod_c, 6, axis=-1)
        hx = modulate(rmsnorm(x, norm1_g[i]), sx1, cx1)
        hc = modulate(rmsnorm(ctx, norm1_g[i]), sc1, cc1)
        if kind == 0:
            yx, yc = window_gqa_mixer(hx, hc, win_w_in[slot], win_w_out[slot], win_sink[slot], with_ctx)
        elif kind == 1:
            lambda_init = 0.8 - 0.6 * math.exp(-0.3 * i)
            yx, yc = diff_attn_mixer(hx, hc, diff_w_in[slot], diff_w_out[slot], diff_lambda[slot],
                                     diff_subln_g[slot], lambda_init, with_ctx)
        else:
            yx, yc = neighbourhood_mixer(hx, hc, na_w_in[slot], na_w_out[slot], na_rpb[slot], with_ctx)
        x = x + gx1 * yx
        x = x + gx2 * ec_moe(modulate(rmsnorm(x, norm2_g[i]), sx2, cx2),
                             router_w[i], w_gate[i], w_up[i], w_down[i])
        if with_ctx:
            ctx = ctx + gc1 * yc
            ctx = ctx + gc2 * ec_moe(modulate(rmsnorm(ctx, norm2_g[i]), sc2, cc2),
                                     router_w[i], w_gate[i], w_up[i], w_down[i])
    return rmsnorm(x, final_g)
```

```python
import functools
import math

import numpy as np
import jax
import jax.numpy as jnp
from jax import lax
from jax.experimental import pallas as pl
from jax.experimental.pallas import tpu as pltpu

HEAD_DIM = 64
LANES = 128
GRID_W = 64
NA_ROWS = 8
NA_COLS = 16
A_WINDOW = 128
A_KV_HEADS = 4
N_MIXERS = 3
ROPE_BASE = 10000.0
ROPE_AXIS_DIM = HEAD_DIM // 2
N_EXPERTS = 16
CAPACITY_FACTOR = 2
RMS_EPS = 1e-6
NEG_INF = -1e30
TILE = 256
NA_TILE_ROWS = TILE // GRID_W
NA_WIN_ROWS = NA_TILE_ROWS + NA_ROWS
VMEM_LIMIT = 56 * 1024 * 1024

BF16 = jnp.bfloat16
F32 = jnp.float32


def _params(n_grid):
    return pltpu.CompilerParams(
        dimension_semantics=("arbitrary",) * n_grid, vmem_limit_bytes=VMEM_LIMIT)


def _split_bf16(a):
    hi = a.astype(BF16)
    lo = (a - hi.astype(F32)).astype(BF16)
    return hi, lo


def _dot(a, b):
    return jnp.dot(a, b, preferred_element_type=F32)


def _dot_nt(a, b):
    return lax.dot_general(a, b, (((1,), (1,)), ((), ())), preferred_element_type=F32)


def _ada_kernel(c_ref, w_ref, b_ref, o_ref):
    c = c_ref[...]
    a = c * jax.nn.sigmoid(c)
    a_hi, a_lo = _split_bf16(a)
    w_hi, w_lo = _split_bf16(w_ref[...])
    o_ref[...] = _dot(a_hi, w_hi) + _dot(a_lo, w_hi) + _dot(a_hi, w_lo) + b_ref[...]


def _ada(cs, w, b):
    R, D = cs.shape
    N = w.shape[1]
    tn = 1024
    return pl.pallas_call(
        _ada_kernel,
        out_shape=jax.ShapeDtypeStruct((R, N), F32),
        grid=(N // tn,),
        in_specs=[pl.BlockSpec((R, D), lambda j: (0, 0)),
                  pl.BlockSpec((D, tn), lambda j: (0, j)),
                  pl.BlockSpec((1, tn), lambda j: (0, j))],
        out_specs=pl.BlockSpec((R, tn), lambda j: (0, j)),
        compiler_params=_params(1),
        name="ada",
    )(cs, w, b)


def _rms_mod(x, g, shift, scale):
    ms = jnp.mean(x * x, axis=-1, keepdims=True)
    y = x * lax.rsqrt(ms + RMS_EPS) * g
    return y * (1.0 + scale) + shift


def _norm_proj_kernel(x_ref, g_ref, sh_ref, sc_ref, w_ref, cos_ref, sa_ref, sb_ref, o_ref,
                      *, n_rope):
    h = _rms_mod(x_ref[0], g_ref[...], sh_ref[0, 0], sc_ref[0, 0]).astype(BF16)
    n_cols = w_ref.shape[1]
    chunk = 512
    for j in range(n_cols // chunk):
        acc = _dot(h, w_ref[:, j * chunk:(j + 1) * chunk])
        for t in range(chunk // LANES):
            col = j * chunk + t * LANES
            a = acc[:, t * LANES:(t + 1) * LANES]
            if col < n_rope:
                a = (a * cos_ref[...]
                     + pltpu.roll(a, LANES - 16, 1) * sa_ref[...]
                     + pltpu.roll(a, 16, 1) * sb_ref[...])
            o_ref[0, :, col:col + LANES] = a.astype(BF16)


def _norm_proj(xa, g, shift, scale, w, rope, n_rope):
    B, S, D = xa.shape
    N = w.shape[1]
    nt = S // TILE
    mod_spec = pl.BlockSpec((1, 1, 1, D), lambda b, t: (b, t // (nt - 1), 0, 0))
    rope_spec = pl.BlockSpec((TILE, LANES), lambda b, t: (t, 0))
    return pl.pallas_call(
        functools.partial(_norm_proj_kernel, n_rope=n_rope),
        out_shape=jax.ShapeDtypeStruct((B, S, N), BF16),
        grid=(B, nt),
        in_specs=[pl.BlockSpec((1, TILE, D), lambda b, t: (b, t, 0)),
                  pl.BlockSpec((1, D), lambda b, t: (0, 0)),
                  mod_spec, mod_spec,
                  pl.BlockSpec((D, N), lambda b, t: (0, 0)),
                  rope_spec, rope_spec, rope_spec],
        out_specs=pl.BlockSpec((1, TILE, N), lambda b, t: (b, t, 0)),
        compiler_params=_params(2),
        name="norm_proj",
    )(xa, g, shift, scale, w, *rope)


def _rope_tables(T, L):
    t = np.arange(T)
    pos = np.stack([t // GRID_W, t % GRID_W], axis=0).astype(np.float32)
    inv = (1.0 / (ROPE_BASE ** (np.arange(0, ROPE_AXIS_DIM, 2, dtype=np.float32)
                                / ROPE_AXIS_DIM))).astype(np.float32)
    d = np.arange(LANES) % HEAD_DIM
    axis = d // ROPE_AXIS_DIM
    half = (d % ROPE_AXIS_DIM) // (ROPE_AXIS_DIM // 2)
    freq = d % (ROPE_AXIS_DIM // 2)
    ang = jnp.asarray(pos[axis].T) * jnp.asarray(inv[freq])[None, :]
    cos = jnp.cos(ang)
    sin = jnp.sin(ang)
    first = jnp.asarray(half == 0)[None, :]
    sa = jnp.where(first, -sin, 0.0)
    sb = jnp.where(first, 0.0, sin)
    pad = lambda a, v: jnp.concatenate([a, jnp.full((L, LANES), v, F32)], axis=0)
    return pad(cos, 1.0), pad(sa, 0.0), pad(sb, 0.0)


def _half_masks(q):
    lane = lax.broadcasted_iota(jnp.int32, q.shape, 1)
    zero = jnp.zeros_like(q)
    return jnp.where(lane < HEAD_DIM, q, zero), jnp.where(lane >= HEAD_DIM, q, zero)


def _merge_halves(o_first, o_second):
    lane = lax.broadcasted_iota(jnp.int32, o_first.shape, 1)
    return jnp.where(lane < HEAD_DIM, o_first, o_second)


def _win_kernel(sink_ref, q_ref, k_ref, v_ref, o_ref, *, T, win):
    h = pl.program_id(1)
    t = pl.program_id(2)
    n_x = T // TILE
    is_x = t < n_x
    q = q_ref[0]
    qa, qb = _half_masks(q[:, :LANES])
    qc, qd = _half_masks(q[:, LANES:])
    qs = jnp.concatenate([qa, qb, qc, qd], axis=0)
    start = pl.multiple_of(jnp.clip(t * TILE - A_WINDOW, 0, T - win), LANES)
    kc = k_ref[0, T:, :]
    vc = v_ref[0, T:, :]
    kw = k_ref[0, pl.ds(start, win), :]
    vw = v_ref[0, pl.ds(start, win), :]
    s_c = _dot_nt(qs, kc)
    s_w = _dot_nt(qs, kw)
    qpos = t * TILE + lax.broadcasted_iota(jnp.int32, (TILE, win), 0)
    kpos = start + lax.broadcasted_iota(jnp.int32, (TILE, win), 1)
    band = (jnp.abs(qpos - kpos) <= A_WINDOW) & is_x
    p_c, p_w = [], []
    for g in range(4):
        sink = sink_ref[h * 4 + g]
        sc = s_c[g * TILE:(g + 1) * TILE]
        sw = jnp.where(band, s_w[g * TILE:(g + 1) * TILE], NEG_INF)
        m = jnp.maximum(jnp.maximum(jnp.max(sc, axis=-1, keepdims=True),
                                    jnp.max(sw, axis=-1, keepdims=True)), sink)
        ec = jnp.exp(sc - m)
        ew = jnp.exp(sw - m)
        den = (jnp.sum(ec, axis=-1, keepdims=True) + jnp.sum(ew, axis=-1, keepdims=True)
               + jnp.exp(sink - m))
        r = 1.0 / den
        p_c.append((ec * r).astype(BF16))
        p_w.append((ew * r).astype(BF16))
    o = _dot(jnp.concatenate(p_c, axis=0), vc) + _dot(jnp.concatenate(p_w, axis=0), vw)
    o_ref[0, :, :LANES] = _merge_halves(o[:TILE], o[TILE:2 * TILE]).astype(BF16)
    o_ref[0, :, LANES:] = _merge_halves(o[2 * TILE:3 * TILE], o[3 * TILE:]).astype(BF16)


def _attn_win(qkv, sink, T):
    B, S, _ = qkv.shape
    nt = S // TILE
    nq = 16 * HEAD_DIM
    kb = nq // LANES
    win = TILE + 2 * A_WINDOW
    return pl.pallas_call(
        functools.partial(_win_kernel, T=T, win=win),
        out_shape=jax.ShapeDtypeStruct((B, S, nq), BF16),
        grid=(B, A_KV_HEADS, nt),
        in_specs=[pl.BlockSpec(memory_space=pltpu.SMEM),
                  pl.BlockSpec((1, TILE, 2 * LANES), lambda b, h, t: (b, t, h)),
                  pl.BlockSpec((1, S, LANES), lambda b, h, t: (b, 0, kb + h)),
                  pl.BlockSpec((1, S, LANES), lambda b, h, t: (b, 0, kb + A_KV_HEADS + h))],
        out_specs=pl.BlockSpec((1, TILE, 2 * LANES), lambda b, h, t: (b, t, h)),
        compiler_params=_params(3),
        name="attn_win",
    )(sink, qkv, qkv, qkv)


def _diff_kernel(lam_ref, q_ref, k_ref, v_ref, g_ref, o_ref, *, T, out_scale):
    t = pl.program_id(2)
    n_x = T // TILE
    lam = lam_ref[0]
    qa, qb = _half_masks(q_ref[0])
    qs = jnp.concatenate([qa, qb], axis=0)

    def attend(kmat, vmat):
        s = _dot_nt(qs, kmat)
        m = jnp.max(s, axis=-1, keepdims=True)
        e = jnp.exp(s - m)
        r = 1.0 / jnp.sum(e, axis=-1, keepdims=True)
        a = e[:TILE] * r[:TILE] - e[TILE:] * (lam * r[TILE:])
        o = _dot(a.astype(BF16), vmat)
        ms = jnp.mean(o * o, axis=-1, keepdims=True)
        y = o * lax.rsqrt(ms + RMS_EPS) * g_ref[...] * out_scale
        o_ref[0] = y.astype(BF16)

    @pl.when(t < n_x)
    def _():
        attend(k_ref[0], v_ref[0])

    @pl.when(t >= n_x)
    def _():
        attend(k_ref[0, T:, :], v_ref[0, T:, :])


def _attn_diff(qkv, lam, subln_g, T, out_scale):
    B, S, _ = qkv.shape
    nt = S // TILE
    H = 8
    return pl.pallas_call(
        functools.partial(_diff_kernel, T=T, out_scale=out_scale),
        out_shape=jax.ShapeDtypeStruct((B, S, H * LANES), BF16),
        grid=(B, H, nt),
        in_specs=[pl.BlockSpec(memory_space=pltpu.SMEM),
                  pl.BlockSpec((1, TILE, LANES), lambda b, h, t: (b, t, h)),
                  pl.BlockSpec((1, S, LANES), lambda b, h, t: (b, 0, H + h)),
                  pl.BlockSpec((1, S, LANES), lambda b, h, t: (b, 0, 2 * H + h)),
                  pl.BlockSpec((1, LANES), lambda b, h, t: (0, 0))],
        out_specs=pl.BlockSpec((1, TILE, LANES), lambda b, h, t: (b, t, h)),
        compiler_params=_params(3),
        name="attn_diff",
    )(lam, qkv, qkv, qkv, subln_g)


def _na_kernel(q_ref, k_ref, v_ref, bias_ref, o_ref, *, T):
    t = pl.program_id(2)
    n_x = T // TILE
    rows = T // GRID_W
    win = NA_WIN_ROWS * GRID_W
    cls = jnp.where(t >= n_x, 3, jnp.where(t == 0, 0, jnp.where(t == n_x - 1, 2, 1)))
    row0 = jnp.clip(t * NA_TILE_ROWS - NA_ROWS // 2, 0, rows - NA_WIN_ROWS)
    start = pl.multiple_of(row0 * GRID_W, GRID_W)
    qa, qb = _half_masks(q_ref[0])
    qs = jnp.concatenate([qa, qb], axis=0)
    kc = k_ref[0, T:, :]
    vc = v_ref[0, T:, :]
    kw = k_ref[0, pl.ds(start, win), :]
    vw = v_ref[0, pl.ds(start, win), :]
    bias = bias_ref[cls]
    s_c = _dot_nt(qs, kc)
    s_w = _dot_nt(qs, kw) + bias.reshape(2 * TILE, win)
    m = jnp.maximum(jnp.max(s_c, axis=-1, keepdims=True), jnp.max(s_w, axis=-1, keepdims=True))
    ec = jnp.exp(s_c - m)
    ew = jnp.exp(s_w - m)
    r = 1.0 / (jnp.sum(ec, axis=-1, keepdims=True) + jnp.sum(ew, axis=-1, keepdims=True))
    o = _dot((ec * r).astype(BF16), vc) + _dot((ew * r).astype(BF16), vw)
    o_ref[0] = _merge_halves(o[:TILE], o[TILE:]).astype(BF16)


def _attn_na(qkv, bias, T):
    B, S, _ = qkv.shape
    nt = S // TILE
    HP = 8
    win = NA_WIN_ROWS * GRID_W
    return pl.pallas_call(
        functools.partial(_na_kernel, T=T),
        out_shape=jax.ShapeDtypeStruct((B, S, HP * LANES), BF16),
        grid=(B, HP, nt),
        in_specs=[pl.BlockSpec((1, TILE, LANES), lambda b, h, t: (b, t, h)),
                  pl.BlockSpec((1, S, LANES), lambda b, h, t: (b, 0, HP + h)),
                  pl.BlockSpec((1, S, LANES), lambda b, h, t: (b, 0, 2 * HP + h)),
                  pl.BlockSpec((4, 2, TILE, win), lambda b, h, t: (0, h, 0, 0))],
        out_specs=pl.BlockSpec((1, TILE, LANES), lambda b, h, t: (b, t, h)),
        compiler_params=_params(3),
        name="attn_na",
    )(qkv, qkv, qkv, bias)


def _na_bias_tables(rpb, T):
    rows = T // GRID_W
    i = np.arange(TILE) // GRID_W
    c = np.arange(TILE) % GRID_W
    j = np.arange(NA_WIN_ROWS * GRID_W) // GRID_W
    kc = np.arange(NA_WIN_ROWS * GRID_W) % GRID_W
    cs = np.clip(c - NA_COLS // 2, 0, GRID_W - NA_COLS)
    valid_c = (kc[None, :] >= cs[:, None]) & (kc[None, :] < cs[:, None] + NA_COLS)
    bidx_c = np.clip(kc[None, :] - c[:, None] + NA_COLS - 1, 0, 2 * NA_COLS - 2)
    tables = []
    for r0 in (0, NA_TILE_ROWS, rows - NA_TILE_ROWS):
        s = int(np.clip(r0 - NA_ROWS // 2, 0, rows - NA_WIN_ROWS))
        r = r0 + i
        rs = np.clip(r - NA_ROWS // 2, 0, rows - NA_ROWS)
        kr = s + j
        valid_r = (kr[None, :] >= rs[:, None]) & (kr[None, :] < rs[:, None] + NA_ROWS)
        bidx_r = np.clip(kr[None, :] - r[:, None] + NA_ROWS - 1, 0, 2 * NA_ROWS - 2)
        vals = rpb[:, jnp.asarray(bidx_r), jnp.asarray(bidx_c)]
        tables.append(jnp.where(jnp.asarray(valid_r & valid_c)[None], vals, NEG_INF))
    tables.append(jnp.full_like(tables[0], NEG_INF))
    return jnp.stack(tables, axis=0).astype(F32)


def _out_router_kernel(y_ref, w_ref, x_ref, g1_ref, n2_ref, sh_ref, sc_ref, rw_ref,
                       xo_ref, h_ref, aff_ref):
    x = x_ref[0] + g1_ref[0, 0] * _dot(y_ref[0], w_ref[...])
    xo_ref[0] = x
    h = _rms_mod(x, n2_ref[...], sh_ref[0, 0], sc_ref[0, 0])
    h_hi, h_lo = _split_bf16(h)
    h_ref[0] = h_hi
    r_hi, r_lo = _split_bf16(rw_ref[...])
    logits = _dot_nt(r_hi, h_hi) + _dot_nt(r_lo, h_hi) + _dot_nt(r_hi, h_lo)
    m = jnp.max(logits, axis=0, keepdims=True)
    e = jnp.exp(logits - m)
    aff_ref[0] = e / jnp.sum(e, axis=0, keepdims=True)


def _out_router(y, w_out, xa, gate1, n2g, shift2, scale2, rw_t):
    B, S, D = xa.shape
    nt = S // TILE
    E = rw_t.shape[0]
    mod_spec = pl.BlockSpec((1, 1, 1, D), lambda b, t: (b, t // (nt - 1), 0, 0))
    tile_spec = pl.BlockSpec((1, TILE, D), lambda b, t: (b, t, 0))
    return pl.pallas_call(
        _out_router_kernel,
        out_shape=(jax.ShapeDtypeStruct((B, S, D), F32),
                   jax.ShapeDtypeStruct((B, S, D), BF16),
                   jax.ShapeDtypeStruct((B, E, S), F32)),
        grid=(B, nt),
        in_specs=[tile_spec,
                  pl.BlockSpec((D, D), lambda b, t: (0, 0)),
                  tile_spec, mod_spec,
                  pl.BlockSpec((1, D), lambda b, t: (0, 0)),
                  mod_spec, mod_spec,
                  pl.BlockSpec((E, D), lambda b, t: (0, 0))],
        out_specs=(tile_spec, tile_spec, pl.BlockSpec((1, E, TILE), lambda b, t: (b, 0, t))),
        compiler_params=_params(2),
        name="out_router",
    )(y, w_out, xa, gate1, n2g, shift2, scale2, rw_t)


def _ffn_kernel(x_ref, gate_ref, wg_ref, wu_ref, wd_ref, o_ref):
    x = x_ref[0, 0]
    ff = wg_ref.shape[2]
    chunk = 512
    acc = jnp.zeros(o_ref.shape[2:], F32)
    for j in range(ff // chunk):
        cols = slice(j * chunk, (j + 1) * chunk)
        g = _dot(x, wg_ref[0, :, cols])
        u = _dot(x, wu_ref[0, :, cols])
        hid = (g * jax.nn.sigmoid(g) * u).astype(BF16)
        acc = acc + _dot(hid, wd_ref[0, cols, :])
    o_ref[0, 0] = acc * gate_ref[0, 0]


def _ffn(xin, gate, wg, wu, wd):
    B, E, S, D = xin.shape
    FF = wg.shape[2]
    return pl.pallas_call(
        _ffn_kernel,
        out_shape=jax.ShapeDtypeStruct((B, E, S, D), F32),
        grid=(E, B),
        in_specs=[pl.BlockSpec((1, 1, S, D), lambda e, b: (b, e, 0, 0)),
                  pl.BlockSpec((1, 1, S, 1), lambda e, b: (b, e, 0, 0)),
                  pl.BlockSpec((1, D, FF), lambda e, b: (e, 0, 0)),
                  pl.BlockSpec((1, D, FF), lambda e, b: (e, 0, 0)),
                  pl.BlockSpec((1, FF, D), lambda e, b: (e, 0, 0))],
        out_specs=pl.BlockSpec((1, 1, S, D), lambda e, b: (b, e, 0, 0)),
        compiler_params=_params(2),
        name="expert_ffn",
    )(xin, gate, wg, wu, wd)


def _final_norm_kernel(x_ref, g_ref, o_ref):
    x = x_ref[0]
    ms = jnp.mean(x * x, axis=-1, keepdims=True)
    o_ref[0] = x * lax.rsqrt(ms + RMS_EPS) * g_ref[...]


def _final_norm(xa, g, T):
    B, S, D = xa.shape
    return pl.pallas_call(
        _final_norm_kernel,
        out_shape=jax.ShapeDtypeStruct((B, T, D), F32),
        grid=(B, T // TILE),
        in_specs=[pl.BlockSpec((1, TILE, D), lambda b, t: (b, t, 0)),
                  pl.BlockSpec((1, D), lambda b, t: (0, 0))],
        out_specs=pl.BlockSpec((1, TILE, D), lambda b, t: (b, t, 0)),
        compiler_params=_params(2),
        name="final_norm",
    )(xa, g)


def _scale_q(w, nq):
    return jnp.concatenate([w[:, :nq] * HEAD_DIM ** -0.5, w[:, nq:]], axis=1)


def _win_weights(w_in):
    D = w_in.shape[0]
    nq = 16 * HEAD_DIM
    nk = A_KV_HEADS * HEAD_DIM
    w = _scale_q(w_in, nq)
    dup = lambda m: jnp.concatenate([m.reshape(D, A_KV_HEADS, 1, HEAD_DIM)] * 2,
                                    axis=2).reshape(D, 2 * nk)
    return jnp.concatenate([w[:, :nq], dup(w[:, nq:nq + nk]), dup(w[:, nq + nk:])],
                           axis=1).astype(BF16)


def _route(aff, h2, T):
    B, E, S = aff.shape
    L = S - T
    gate_x, idx_x = lax.top_k(aff[:, :, :T], CAPACITY_FACTOR * T // N_EXPERTS)
    gate_c, idx_c = lax.top_k(aff[:, :, T:], CAPACITY_FACTOR * L // N_EXPERTS)
    idx = jnp.concatenate([idx_x, idx_c + T], axis=2)
    gate = jnp.concatenate([gate_x, gate_c], axis=2)
    bidx = jnp.arange(B)[:, None, None]
    return idx, gate, h2[bidx, idx]


def _combine(y, idx, S):
    B, E, _, D = y.shape
    bidx = jnp.arange(B)[:, None, None]
    return jnp.zeros((B, S, D), F32).at[bidx, idx].add(y)


def kernel(x, c, ctx, c_ctx, ada_w, ada_b, norm1_g, norm2_g, final_g, win_w_in, win_w_out,
           win_sink, diff_w_in, diff_w_out, diff_lambda, diff_subln_g, na_w_in, na_w_out,
           na_rpb, router_w, w_gate, w_up, w_down):
    B, T, D = x.shape
    L = ctx.shape[1]
    S = T + L
    depth = ada_w.shape[0]
    assert L == TILE and T % TILE == 0 and (T // GRID_W) >= NA_WIN_ROWS
    xa = jnp.concatenate([x, ctx], axis=1)
    cs = jnp.concatenate([c, c_ctx[None, :]], axis=0)
    rope = _rope_tables(T, L)
    for i in range(depth):
        kind = i % N_MIXERS
        slot = i // N_MIXERS
        mod = _ada(cs, ada_w[i], ada_b[i][None, :])
        mod = jnp.stack([mod[:B], jnp.broadcast_to(mod[B:], (B, 6 * D))], axis=1)
        mod = mod.reshape(B, 2, 1, 6, D)
        sh1, sc1, g1, sh2, sc2, g2 = [mod[:, :, :, k, :] for k in range(6)]
        if kind == 0:
            w_in = _win_weights(win_w_in[slot])
            qkv = _norm_proj(xa, norm1_g[i][None, :], sh1, sc1, w_in, rope,
                             (16 + 2 * A_KV_HEADS) * HEAD_DIM)
            y = _attn_win(qkv, win_sink[slot], T)
            w_out = win_w_out[slot]
        elif kind == 1:
            lambda_init = 0.8 - 0.6 * math.exp(-0.3 * i)
            w_in = _scale_q(diff_w_in[slot], 16 * HEAD_DIM).astype(BF16)
            qkv = _norm_proj(xa, norm1_g[i][None, :], sh1, sc1, w_in, rope, 32 * HEAD_DIM)
            lp = diff_lambda[slot]
            lam = (jnp.exp(jnp.sum(lp[0] * lp[1])) - jnp.exp(jnp.sum(lp[2] * lp[3]))
                   + lambda_init).reshape(1)
            y = _attn_diff(qkv, lam, diff_subln_g[slot][None, :], T, 1.0 - lambda_init)
            w_out = diff_w_out[slot]
        else:
            w_in = _scale_q(na_w_in[slot], 16 * HEAD_DIM).astype(BF16)
            qkv = _norm_proj(xa, norm1_g[i][None, :], sh1, sc1, w_in, rope, 0)
            y = _attn_na(qkv, _na_bias_tables(na_rpb[slot], T), T)
            w_out = na_w_out[slot]
        xa, h2, aff = _out_router(y, w_out.astype(BF16), xa, g1, norm2_g[i][None, :],
                                  sh2, sc2, router_w[i].T)
        idx, gate, xin = _route(aff, h2, T)
        ye = _ffn(xin, gate[..., None], w_gate[i].astype(BF16), w_up[i].astype(BF16),
                  w_down[i].astype(BF16))
        xa = xa + _gated(_combine(ye, idx, S), g2, T)
    return _final_norm(xa, final_g[None, :], T)


def _gated(moe, g2, T):
    return jnp.concatenate([moe[:, :T] * g2[:, 0], moe[:, T:] * g2[:, 1]], axis=1)
```

```python
import functools
import math

import numpy as np
import jax
import jax.numpy as jnp
from jax import lax
from jax.experimental import pallas as pl
from jax.experimental.pallas import tpu as pltpu

HEAD_DIM = 64
LANES = 128
GRID_W = 64
NA_ROWS = 8
NA_COLS = 16
A_WINDOW = 128
A_KV_HEADS = 4
N_MIXERS = 3
ROPE_BASE = 10000.0
ROPE_AXIS_DIM = HEAD_DIM // 2
N_EXPERTS = 16
CAPACITY_FACTOR = 2
RMS_EPS = 1e-6
NEG_INF = -1e30
TILE = 256
NA_TILE_ROWS = TILE // GRID_W
NA_WIN_ROWS = NA_TILE_ROWS + NA_ROWS
VMEM_LIMIT = 56 * 1024 * 1024
SLOT_WIN = 64
SLOT_ALIGN = 16
GATE_COLS = LANES

BF16 = jnp.bfloat16
F32 = jnp.float32


def _params(n_grid):
    return pltpu.CompilerParams(
        dimension_semantics=("arbitrary",) * n_grid, vmem_limit_bytes=VMEM_LIMIT)


def _split_bf16(a):
    hi = a.astype(BF16)
    lo = (a - hi.astype(F32)).astype(BF16)
    return hi, lo


def _dot(a, b):
    return jnp.dot(a, b, preferred_element_type=F32)


def _dot_nt(a, b):
    return lax.dot_general(a, b, (((1,), (1,)), ((), ())), preferred_element_type=F32)


def _ada_kernel(c_ref, w_ref, b_ref, o_ref):
    c = c_ref[...]
    a = c * jax.nn.sigmoid(c)
    a_hi, a_lo = _split_bf16(a)
    w_hi, w_lo = _split_bf16(w_ref[...])
    o_ref[...] = _dot(a_hi, w_hi) + _dot(a_lo, w_hi) + _dot(a_hi, w_lo) + b_ref[...]


def _ada(cs, w, b):
    R, D = cs.shape
    N = w.shape[1]
    tn = 1024
    return pl.pallas_call(
        _ada_kernel,
        out_shape=jax.ShapeDtypeStruct((R, N), F32),
        grid=(N // tn,),
        in_specs=[pl.BlockSpec((R, D), lambda j: (0, 0)),
                  pl.BlockSpec((D, tn), lambda j: (0, j)),
                  pl.BlockSpec((1, tn), lambda j: (0, j))],
        out_specs=pl.BlockSpec((R, tn), lambda j: (0, j)),
        compiler_params=_params(1),
        name="ada",
    )(cs, w, b)


def _rms_mod(x, g, shift, scale):
    ms = jnp.mean(x * x, axis=-1, keepdims=True)
    y = x * lax.rsqrt(ms + RMS_EPS) * g
    return y * (1.0 + scale) + shift


def _norm_proj_kernel(x_ref, g_ref, sh_ref, sc_ref, w_ref, cos_ref, sa_ref, sb_ref, o_ref,
                      *, n_rope):
    h = _rms_mod(x_ref[0], g_ref[...], sh_ref[0, 0], sc_ref[0, 0]).astype(BF16)
    n_cols = w_ref.shape[1]
    chunk = 512
    for j in range(n_cols // chunk):
        acc = _dot(h, w_ref[:, j * chunk:(j + 1) * chunk])
        for t in range(chunk // LANES):
            col = j * chunk + t * LANES
            a = acc[:, t * LANES:(t + 1) * LANES]
            if col < n_rope:
                a = (a * cos_ref[...]
                     + pltpu.roll(a, LANES - 16, 1) * sa_ref[...]
                     + pltpu.roll(a, 16, 1) * sb_ref[...])
            o_ref[0, :, col:col + LANES] = a.astype(BF16)


def _norm_proj(xa, g, shift, scale, w, rope, n_rope):
    B, S, D = xa.shape
    N = w.shape[1]
    nt = S // TILE
    mod_spec = pl.BlockSpec((1, 1, 1, D), lambda b, t: (b, t // (nt - 1), 0, 0))
    rope_spec = pl.BlockSpec((TILE, LANES), lambda b, t: (t, 0))
    return pl.pallas_call(
        functools.partial(_norm_proj_kernel, n_rope=n_rope),
        out_shape=jax.ShapeDtypeStruct((B, S, N), BF16),
        grid=(B, nt),
        in_specs=[pl.BlockSpec((1, TILE, D), lambda b, t: (b, t, 0)),
                  pl.BlockSpec((1, D), lambda b, t: (0, 0)),
                  mod_spec, mod_spec,
                  pl.BlockSpec((D, N), lambda b, t: (0, 0)),
                  rope_spec, rope_spec, rope_spec],
        out_specs=pl.BlockSpec((1, TILE, N), lambda b, t: (b, t, 0)),
        compiler_params=_params(2),
        name="norm_proj",
    )(xa, g, shift, scale, w, *rope)


def _rope_tables(T, L):
    t = np.arange(T)
    pos = np.stack([t // GRID_W, t % GRID_W], axis=0).astype(np.float32)
    inv = (1.0 / (ROPE_BASE ** (np.arange(0, ROPE_AXIS_DIM, 2, dtype=np.float32)
                                / ROPE_AXIS_DIM))).astype(np.float32)
    d = np.arange(LANES) % HEAD_DIM
    axis = d // ROPE_AXIS_DIM
    half = (d % ROPE_AXIS_DIM) // (ROPE_AXIS_DIM // 2)
    freq = d % (ROPE_AXIS_DIM // 2)
    ang = jnp.asarray(pos[axis].T) * jnp.asarray(inv[freq])[None, :]
    cos = jnp.cos(ang)
    sin = jnp.sin(ang)
    first = jnp.asarray(half == 0)[None, :]
    sa = jnp.where(first, -sin, 0.0)
    sb = jnp.where(first, 0.0, sin)
    pad = lambda a, v: jnp.concatenate([a, jnp.full((L, LANES), v, F32)], axis=0)
    return pad(cos, 1.0), pad(sa, 0.0), pad(sb, 0.0)


def _half_masks(q):
    lane = lax.broadcasted_iota(jnp.int32, q.shape, 1)
    zero = jnp.zeros_like(q)
    return jnp.where(lane < HEAD_DIM, q, zero), jnp.where(lane >= HEAD_DIM, q, zero)


def _merge_halves(o_first, o_second):
    lane = lax.broadcasted_iota(jnp.int32, o_first.shape, 1)
    return jnp.where(lane < HEAD_DIM, o_first, o_second)


def _win_kernel(sink_ref, q_ref, k_ref, v_ref, o_ref, *, T, win):
    h = pl.program_id(1)
    t = pl.program_id(2)
    n_x = T // TILE
    is_x = t < n_x
    q = q_ref[0]
    qa, qb = _half_masks(q[:, :LANES])
    qc, qd = _half_masks(q[:, LANES:])
    qs = jnp.concatenate([qa, qb, qc, qd], axis=0)
    start = pl.multiple_of(jnp.clip(t * TILE - A_WINDOW, 0, T - win), LANES)
    kc = k_ref[0, T:, :]
    vc = v_ref[0, T:, :]
    kw = k_ref[0, pl.ds(start, win), :]
    vw = v_ref[0, pl.ds(start, win), :]
    s_c = _dot_nt(qs, kc)
    s_w = _dot_nt(qs, kw)
    qpos = t * TILE + lax.broadcasted_iota(jnp.int32, (TILE, win), 0)
    kpos = start + lax.broadcasted_iota(jnp.int32, (TILE, win), 1)
    band = (jnp.abs(qpos - kpos) <= A_WINDOW) & is_x
    p_c, p_w = [], []
    for g in range(4):
        sink = sink_ref[h * 4 + g]
        sc = s_c[g * TILE:(g + 1) * TILE]
        sw = jnp.where(band, s_w[g * TILE:(g + 1) * TILE], NEG_INF)
        m = jnp.maximum(jnp.maximum(jnp.max(sc, axis=-1, keepdims=True),
                                    jnp.max(sw, axis=-1, keepdims=True)), sink)
        ec = jnp.exp(sc - m)
        ew = jnp.exp(sw - m)
        den = (jnp.sum(ec, axis=-1, keepdims=True) + jnp.sum(ew, axis=-1, keepdims=True)
               + jnp.exp(sink - m))
        r = 1.0 / den
        p_c.append((ec * r).astype(BF16))
        p_w.append((ew * r).astype(BF16))
    o = _dot(jnp.concatenate(p_c, axis=0), vc) + _dot(jnp.concatenate(p_w, axis=0), vw)
    o_ref[0, :, :LANES] = _merge_halves(o[:TILE], o[TILE:2 * TILE]).astype(BF16)
    o_ref[0, :, LANES:] = _merge_halves(o[2 * TILE:3 * TILE], o[3 * TILE:]).astype(BF16)


def _attn_win(qkv, sink, T):
    B, S, _ = qkv.shape
    nt = S // TILE
    nq = 16 * HEAD_DIM
    kb = nq // LANES
    win = TILE + 2 * A_WINDOW
    return pl.pallas_call(
        functools.partial(_win_kernel, T=T, win=win),
        out_shape=jax.ShapeDtypeStruct((B, S, nq), BF16),
        grid=(B, A_KV_HEADS, nt),
        in_specs=[pl.BlockSpec(memory_space=pltpu.SMEM),
                  pl.BlockSpec((1, TILE, 2 * LANES), lambda b, h, t: (b, t, h)),
                  pl.BlockSpec((1, S, LANES), lambda b, h, t: (b, 0, kb + h)),
                  pl.BlockSpec((1, S, LANES), lambda b, h, t: (b, 0, kb + A_KV_HEADS + h))],
        out_specs=pl.BlockSpec((1, TILE, 2 * LANES), lambda b, h, t: (b, t, h)),
        compiler_params=_params(3),
        name="attn_win",
    )(sink, qkv, qkv, qkv)


def _diff_kernel(lam_ref, q_ref, k_ref, v_ref, g_ref, o_ref, *, T, out_scale):
    t = pl.program_id(2)
    n_x = T // TILE
    lam = lam_ref[0]
    qa, qb = _half_masks(q_ref[0])
    qs = jnp.concatenate([qa, qb], axis=0)

    def attend(kmat, vmat):
        s = _dot_nt(qs, kmat)
        m = jnp.max(s, axis=-1, keepdims=True)
        e = jnp.exp(s - m)
        r = 1.0 / jnp.sum(e, axis=-1, keepdims=True)
        a = e[:TILE] * r[:TILE] - e[TILE:] * (lam * r[TILE:])
        o = _dot(a.astype(BF16), vmat)
        ms = jnp.mean(o * o, axis=-1, keepdims=True)
        y = o * lax.rsqrt(ms + RMS_EPS) * g_ref[...] * out_scale
        o_ref[0] = y.astype(BF16)

    @pl.when(t < n_x)
    def _():
        attend(k_ref[0], v_ref[0])

    @pl.when(t >= n_x)
    def _():
        attend(k_ref[0, T:, :], v_ref[0, T:, :])


def _attn_diff(qkv, lam, subln_g, T, out_scale):
    B, S, _ = qkv.shape
    nt = S // TILE
    H = 8
    return pl.pallas_call(
        functools.partial(_diff_kernel, T=T, out_scale=out_scale),
        out_shape=jax.ShapeDtypeStruct((B, S, H * LANES), BF16),
        grid=(B, H, nt),
        in_specs=[pl.BlockSpec(memory_space=pltpu.SMEM),
                  pl.BlockSpec((1, TILE, LANES), lambda b, h, t: (b, t, h)),
                  pl.BlockSpec((1, S, LANES), lambda b, h, t: (b, 0, H + h)),
                  pl.BlockSpec((1, S, LANES), lambda b, h, t: (b, 0, 2 * H + h)),
                  pl.BlockSpec((1, LANES), lambda b, h, t: (0, 0))],
        out_specs=pl.BlockSpec((1, TILE, LANES), lambda b, h, t: (b, t, h)),
        compiler_params=_params(3),
        name="attn_diff",
    )(lam, qkv, qkv, qkv, subln_g)


def _na_kernel(q_ref, k_ref, v_ref, bias_ref, o_ref, *, T):
    t = pl.program_id(2)
    n_x = T // TILE
    rows = T // GRID_W
    win = NA_WIN_ROWS * GRID_W
    cls = jnp.where(t >= n_x, 3, jnp.where(t == 0, 0, jnp.where(t == n_x - 1, 2, 1)))
    row0 = jnp.clip(t * NA_TILE_ROWS - NA_ROWS // 2, 0, rows - NA_WIN_ROWS)
    start = pl.multiple_of(row0 * GRID_W, GRID_W)
    qa, qb = _half_masks(q_ref[0])
    qs = jnp.concatenate([qa, qb], axis=0)
    kc = k_ref[0, T:, :]
    vc = v_ref[0, T:, :]
    kw = k_ref[0, pl.ds(start, win), :]
    vw = v_ref[0, pl.ds(start, win), :]
    bias = bias_ref[cls]
    s_c = _dot_nt(qs, kc)
    s_w = _dot_nt(qs, kw) + bias.reshape(2 * TILE, win)
    m = jnp.maximum(jnp.max(s_c, axis=-1, keepdims=True), jnp.max(s_w, axis=-1, keepdims=True))
    ec = jnp.exp(s_c - m)
    ew = jnp.exp(s_w - m)
    r = 1.0 / (jnp.sum(ec, axis=-1, keepdims=True) + jnp.sum(ew, axis=-1, keepdims=True))
    o = _dot((ec * r).astype(BF16), vc) + _dot((ew * r).astype(BF16), vw)
    o_ref[0] = _merge_halves(o[:TILE], o[TILE:]).astype(BF16)


def _attn_na(qkv, bias, T):
    B, S, _ = qkv.shape
    nt = S // TILE
    HP = 8
    win = NA_WIN_ROWS * GRID_W
    return pl.pallas_call(
        functools.partial(_na_kernel, T=T),
        out_shape=jax.ShapeDtypeStruct((B, S, HP * LANES), BF16),
        grid=(B, HP, nt),
        in_specs=[pl.BlockSpec((1, TILE, LANES), lambda b, h, t: (b, t, h)),
                  pl.BlockSpec((1, S, LANES), lambda b, h, t: (b, 0, HP + h)),
                  pl.BlockSpec((1, S, LANES), lambda b, h, t: (b, 0, 2 * HP + h)),
                  pl.BlockSpec((4, 2, TILE, win), lambda b, h, t: (0, h, 0, 0))],
        out_specs=pl.BlockSpec((1, TILE, LANES), lambda b, h, t: (b, t, h)),
        compiler_params=_params(3),
        name="attn_na",
    )(qkv, qkv, qkv, bias)


def _na_bias_tables(rpb, T):
    rows = T // GRID_W
    i = np.arange(TILE) // GRID_W
    c = np.arange(TILE) % GRID_W
    j = np.arange(NA_WIN_ROWS * GRID_W) // GRID_W
    kc = np.arange(NA_WIN_ROWS * GRID_W) % GRID_W
    cs = np.clip(c - NA_COLS // 2, 0, GRID_W - NA_COLS)
    valid_c = (kc[None, :] >= cs[:, None]) & (kc[None, :] < cs[:, None] + NA_COLS)
    bidx_c = np.clip(kc[None, :] - c[:, None] + NA_COLS - 1, 0, 2 * NA_COLS - 2)
    tables = []
    for r0 in (0, NA_TILE_ROWS, rows - NA_TILE_ROWS):
        s = int(np.clip(r0 - NA_ROWS // 2, 0, rows - NA_WIN_ROWS))
        r = r0 + i
        rs = np.clip(r - NA_ROWS // 2, 0, rows - NA_ROWS)
        kr = s + j
        valid_r = (kr[None, :] >= rs[:, None]) & (kr[None, :] < rs[:, None] + NA_ROWS)
        bidx_r = np.clip(kr[None, :] - r[:, None] + NA_ROWS - 1, 0, 2 * NA_ROWS - 2)
        vals = rpb[:, jnp.asarray(bidx_r), jnp.asarray(bidx_c)]
        tables.append(jnp.where(jnp.asarray(valid_r & valid_c)[None], vals, NEG_INF))
    tables.append(jnp.full_like(tables[0], NEG_INF))
    return jnp.stack(tables, axis=0).astype(F32)


def _out_router_kernel(y_ref, w_ref, x_ref, g1_ref, n2_ref, sh_ref, sc_ref, rw_ref,
                       xo_ref, h_ref, aff_ref):
    D = x_ref.shape[2]
    x = x_ref[0] + g1_ref[0, 0] * _dot(y_ref[0], w_ref[...])
    xo_ref[0] = x
    h = _rms_mod(x, n2_ref[...], sh_ref[0, 0], sc_ref[0, 0])
    h_hi, h_lo = _split_bf16(h)
    r_hi, r_lo = _split_bf16(rw_ref[...])
    logits = _dot(h_hi, r_hi) + _dot(h_lo, r_hi) + _dot(h_hi, r_lo)
    lane = lax.broadcasted_iota(jnp.int32, logits.shape, 1)
    first = lane < N_EXPERTS
    m = jnp.max(jnp.where(first, logits, -jnp.inf), axis=-1, keepdims=True)
    e = jnp.exp(logits - m)
    aff = e / jnp.sum(jnp.where(first, e, 0.0), axis=-1, keepdims=True)
    aff_ref[0] = aff[:, :N_EXPERTS]
    hi = aff.astype(BF16)
    rem = aff - hi.astype(F32)
    mid = rem.astype(BF16)
    lo = (rem - mid.astype(F32)).astype(BF16)
    zero = jnp.zeros_like(hi)
    pieces = jnp.where(first, hi, jnp.where(lane < 2 * N_EXPERTS, mid,
                                            jnp.where(lane < 3 * N_EXPERTS, lo, zero)))
    h_ref[0, :, :D] = h_hi
    h_ref[0, :, D:] = pieces


def _out_router(y, w_out, xa, gate1, n2g, shift2, scale2, rw3):
    B, S, D = xa.shape
    nt = S // TILE
    mod_spec = pl.BlockSpec((1, 1, 1, D), lambda b, t: (b, t // (nt - 1), 0, 0))
    tile_spec = pl.BlockSpec((1, TILE, D), lambda b, t: (b, t, 0))
    return pl.pallas_call(
        _out_router_kernel,
        out_shape=(jax.ShapeDtypeStruct((B, S, D), F32),
                   jax.ShapeDtypeStruct((B, S, D + GATE_COLS), BF16),
                   jax.ShapeDtypeStruct((B, S, N_EXPERTS), F32)),
        grid=(B, nt),
        in_specs=[tile_spec,
                  pl.BlockSpec((D, D), lambda b, t: (0, 0)),
                  tile_spec, mod_spec,
                  pl.BlockSpec((1, D), lambda b, t: (0, 0)),
                  mod_spec, mod_spec,
                  pl.BlockSpec((D, GATE_COLS), lambda b, t: (0, 0))],
        out_specs=(tile_spec,
                   pl.BlockSpec((1, TILE, D + GATE_COLS), lambda b, t: (b, t, 0)),
                   pl.BlockSpec((1, TILE, N_EXPERTS), lambda b, t: (b, t, 0))),
        compiler_params=_params(2),
        name="out_router",
    )(y, w_out, xa, gate1, n2g, shift2, scale2, rw3)


def _route_kernel(aff_ref, slot_ref, start_ref, cnt_ref, *, T, cap_x, cap_c):
    S = aff_ref.shape[1]
    bits = lax.bitcast_convert_type(aff_ref[0], jnp.int32)
    ri = lax.broadcasted_iota(jnp.int32, (TILE, TILE), 0)
    ci = lax.broadcasted_iota(jnp.int32, (TILE, TILE), 1)
    ltri = (ri > ci).astype(BF16)

    def count(mask):
        return jnp.sum(mask.astype(F32), axis=0, keepdims=True)

    for lo_row, hi_row, cap, base in ((0, T, cap_x, 0), (T, S, cap_c, cap_x)):
        b = bits[lo_row:hi_row]

        def step(i, thr):
            cand = thr | lax.shift_left(jnp.int32(1), 30 - i)
            return jnp.where(count(b >= cand) >= cap, cand, thr)

        thr = lax.fori_loop(0, 31, step, jnp.zeros((1, N_EXPERTS), jnp.int32))
        need = cap - count(b > thr)
        seen_eq = jnp.zeros((1, N_EXPERTS), F32)
        seen = jnp.zeros((1, N_EXPERTS), F32)
        for j in range((hi_row - lo_row) // TILE):
            blk = b[j * TILE:(j + 1) * TILE]
            gt = blk > thr
            eq = blk == thr
            eq_rank = _dot(ltri, eq.astype(BF16)) + seen_eq
            sel = gt | (eq & (eq_rank < need))
            pos = _dot(ltri, sel.astype(BF16)) + seen + base
            t = lo_row // TILE + j
            slot_ref[0, t * TILE:(t + 1) * TILE, :] = jnp.where(sel, pos, -1.0)
            n_sel = count(sel)
            start_ref[0, t:t + 1, :] = seen + base
            cnt_ref[0, t:t + 1, :] = n_sel
            seen_eq = seen_eq + count(eq)
            seen = seen + n_sel


def _route(aff, T):
    B, S, E = aff.shape
    nt = S // TILE
    cap_x = CAPACITY_FACTOR * T // N_EXPERTS
    cap_c = CAPACITY_FACTOR * (S - T) // N_EXPERTS
    plan = jax.ShapeDtypeStruct((B, nt, E), F32)
    return pl.pallas_call(
        functools.partial(_route_kernel, T=T, cap_x=cap_x, cap_c=cap_c),
        out_shape=(jax.ShapeDtypeStruct((B, S, E), F32), plan, plan),
        grid=(B,),
        in_specs=[pl.BlockSpec((1, S, E), lambda b: (b, 0, 0))],
        out_specs=(pl.BlockSpec((1, S, E), lambda b: (b, 0, 0)),
                   pl.BlockSpec((1, nt, E), lambda b: (b, 0, 0)),
                   pl.BlockSpec((1, nt, E), lambda b: (b, 0, 0))),
        compiler_params=_params(1),
        name="route",
    )(aff)


def _slot_plan(start, cnt):
    start = start.astype(jnp.int32)
    cnt = cnt.astype(jnp.int32)
    first = (start // SLOT_ALIGN) * SLOT_ALIGN
    n_pass = jnp.max((start - first + cnt + SLOT_WIN - 1) // SLOT_WIN, axis=-1)
    return first, n_pass.astype(jnp.int32)


def _window_rel(slot, first, k, cap_total):
    nominal = first + k * SLOT_WIN
    begin = jnp.minimum(nominal, float(cap_total - SLOT_WIN))
    rel = slot - nominal
    return jnp.where((rel >= 0) & (rel < SLOT_WIN), rel + (nominal - begin), 255.0)


def _window_begin(first_s, k, cap_total):
    return pl.multiple_of(jnp.minimum(first_s + k * SLOT_WIN, cap_total - SLOT_WIN), SLOT_ALIGN)


def _dispatch_kernel(first_s, npass_s, slot_ref, first_ref, h_ref, o_ref):
    b, half, t = pl.program_id(0), pl.program_id(1), pl.program_id(2)
    nt = pl.num_programs(2)
    n_e = o_ref.shape[1]
    cap_total = o_ref.shape[2]

    @pl.when(t == 0)
    def _():
        o_ref[...] = jnp.zeros_like(o_ref)

    slot = slot_ref[0]
    first = first_ref[0, 0]
    pick_r = lax.broadcasted_iota(jnp.int32, (n_e, N_EXPERTS), 0)
    pick_c = lax.broadcasted_iota(jnp.int32, (n_e, N_EXPERTS), 1)
    pick = (pick_c == pick_r + half * n_e).astype(BF16)
    sub = lax.broadcasted_iota(jnp.int32, (SLOT_WIN, TILE), 0).astype(F32)

    def one_pass(k, carry):
        rel = _window_rel(slot, first, k.astype(F32), cap_total).astype(BF16)
        rel_t = _dot_nt(pick, rel)
        onehot = jnp.concatenate(
            [(jnp.broadcast_to(rel_t[e:e + 1, :], (SLOT_WIN, TILE)) == sub).astype(BF16)
             for e in range(n_e)], axis=0)
        rows = _dot(onehot, h_ref[0])
        for e in range(n_e):
            base = (b * nt + t) * N_EXPERTS + half * n_e + e
            win = pl.ds(_window_begin(first_s[base], k, cap_total), SLOT_WIN)
            cur = o_ref[0, e, win, :].astype(F32)
            o_ref[0, e, win, :] = (cur + rows[e * SLOT_WIN:(e + 1) * SLOT_WIN]).astype(BF16)
        return carry

    lax.fori_loop(0, npass_s[b * nt + t], one_pass, 0)


def _dispatch(hx, slot, first, n_pass, cap_total):
    B, S, W = hx.shape
    nt = S // TILE
    halves = 2
    n_e = N_EXPERTS // halves
    grid_spec = pltpu.PrefetchScalarGridSpec(
        num_scalar_prefetch=2,
        grid=(B, halves, nt),
        in_specs=[pl.BlockSpec((1, TILE, N_EXPERTS), lambda b, h, t, *_: (b, t, 0)),
                  pl.BlockSpec((1, 1, 1, N_EXPERTS), lambda b, h, t, *_: (b, t, 0, 0)),
                  pl.BlockSpec((1, TILE, W), lambda b, h, t, *_: (b, t, 0))],
        out_specs=pl.BlockSpec((1, n_e, cap_total, W), lambda b, h, t, *_: (b, h, 0, 0)),
    )
    return pl.pallas_call(
        _dispatch_kernel,
        out_shape=jax.ShapeDtypeStruct((B, N_EXPERTS, cap_total, W), BF16),
        grid_spec=grid_spec,
        compiler_params=_params(3),
        name="dispatch",
    )(first.reshape(-1), n_pass.reshape(-1), slot,
      first.astype(F32).reshape(B, nt, 1, N_EXPERTS), hx)


def _ffn_kernel(x_ref, wg_ref, wu_ref, wd_ref, o_ref):
    D = wg_ref.shape[1]
    ff = wg_ref.shape[2]
    x = x_ref[0, 0, :, :D]
    pieces = x_ref[0, 0, :, D:].astype(F32)
    lane = lax.broadcasted_iota(jnp.int32, pieces.shape, 1)
    mine = ((lane & (N_EXPERTS - 1)) == pl.program_id(0)) & (lane < 3 * N_EXPERTS)
    gate = jnp.sum(jnp.where(mine, pieces, 0.0), axis=-1, keepdims=True)
    chunk = 512
    acc = jnp.zeros((x.shape[0], D), F32)
    for j in range(ff // chunk):
        cols = slice(j * chunk, (j + 1) * chunk)
        g = _dot(x, wg_ref[0, :, cols])
        u = _dot(x, wu_ref[0, :, cols])
        hid = (g * jax.nn.sigmoid(g) * u).astype(BF16)
        acc = acc + _dot(hid, wd_ref[0, cols, :])
    o_ref[0, 0] = (acc * gate).astype(BF16)


def _ffn(xin, wg, wu, wd):
    B, E, S, W = xin.shape
    D, FF = wg.shape[1], wg.shape[2]
    return pl.pallas_call(
        _ffn_kernel,
        out_shape=jax.ShapeDtypeStruct((B, E, S, D), BF16),
        grid=(E, B),
        in_specs=[pl.BlockSpec((1, 1, S, W), lambda e, b: (b, e, 0, 0)),
                  pl.BlockSpec((1, D, FF), lambda e, b: (e, 0, 0)),
                  pl.BlockSpec((1, D, FF), lambda e, b: (e, 0, 0)),
                  pl.BlockSpec((1, FF, D), lambda e, b: (e, 0, 0))],
        out_specs=pl.BlockSpec((1, 1, S, D), lambda e, b: (b, e, 0, 0)),
        compiler_params=_params(2),
        name="expert_ffn",
    )(xin, wg, wu, wd)


def _combine_kernel(first_s, npass_s, slot_ref, first_ref, expand_ref, y_ref, x_ref, g2_ref,
                    o_ref):
    b, t = pl.program_id(0), pl.program_id(1)
    nt = pl.num_programs(1)
    cap_total = y_ref.shape[2]
    o_ref[0] = x_ref[0]
    slot = slot_ref[0]
    first = first_ref[0, 0]
    lane_row = (lax.broadcasted_iota(jnp.int32, (TILE, N_EXPERTS * SLOT_WIN), 1)
                & (SLOT_WIN - 1)).astype(F32)

    def one_pass(k, carry):
        rel = _window_rel(slot, first, k.astype(F32), cap_total).astype(BF16)
        onehot = (_dot(rel, expand_ref[...]) == lane_row).astype(BF16)
        rows = jnp.concatenate(
            [y_ref[0, e, pl.ds(_window_begin(first_s[(b * nt + t) * N_EXPERTS + e], k,
                                             cap_total), SLOT_WIN), :]
             for e in range(N_EXPERTS)], axis=0)
        o_ref[0] += g2_ref[0, 0] * _dot(onehot, rows)
        return carry

    lax.fori_loop(0, npass_s[b * nt + t], one_pass, 0)


def _combine(y, slot, first, n_pass, xa, gate2):
    B, S, D = xa.shape
    nt = S // TILE
    E, cap_total = y.shape[1], y.shape[2]
    expand = np.repeat(np.eye(E, dtype=np.float32), SLOT_WIN, axis=1)
    grid_spec = pltpu.PrefetchScalarGridSpec(
        num_scalar_prefetch=2,
        grid=(B, nt),
        in_specs=[pl.BlockSpec((1, TILE, E), lambda b, t, *_: (b, t, 0)),
                  pl.BlockSpec((1, 1, 1, E), lambda b, t, *_: (b, t, 0, 0)),
                  pl.BlockSpec((E, E * SLOT_WIN), lambda b, t, *_: (0, 0)),
                  pl.BlockSpec((1, E, cap_total, D), lambda b, t, *_: (b, 0, 0, 0)),
                  pl.BlockSpec((1, TILE, D), lambda b, t, *_: (b, t, 0)),
                  pl.BlockSpec((1, 1, 1, D), lambda b, t, *_: (b, t // (nt - 1), 0, 0))],
        out_specs=pl.BlockSpec((1, TILE, D), lambda b, t, *_: (b, t, 0)),
    )
    return pl.pallas_call(
        _combine_kernel,
        out_shape=jax.ShapeDtypeStruct((B, S, D), F32),
        grid_spec=grid_spec,
        compiler_params=_params(2),
        name="combine",
    )(first.reshape(-1), n_pass.reshape(-1), slot,
      first.astype(F32).reshape(B, nt, 1, E), jnp.asarray(expand, BF16), y, xa, gate2)


def _final_norm_kernel(x_ref, g_ref, o_ref):
    x = x_ref[0]
    ms = jnp.mean(x * x, axis=-1, keepdims=True)
    o_ref[0] = x * lax.rsqrt(ms + RMS_EPS) * g_ref[...]


def _final_norm(xa, g, T):
    B, S, D = xa.shape
    return pl.pallas_call(
        _final_norm_kernel,
        out_shape=jax.ShapeDtypeStruct((B, T, D), F32),
        grid=(B, T // TILE),
        in_specs=[pl.BlockSpec((1, TILE, D), lambda b, t: (b, t, 0)),
                  pl.BlockSpec((1, D), lambda b, t: (0, 0))],
        out_specs=pl.BlockSpec((1, TILE, D), lambda b, t: (b, t, 0)),
        compiler_params=_params(2),
        name="final_norm",
    )(xa, g)


def _scale_q(w, nq):
    return jnp.concatenate([w[:, :nq] * HEAD_DIM ** -0.5, w[:, nq:]], axis=1)


def _win_weights(w_in):
    D = w_in.shape[0]
    nq = 16 * HEAD_DIM
    nk = A_KV_HEADS * HEAD_DIM
    w = _scale_q(w_in, nq)
    dup = lambda m: jnp.concatenate([m.reshape(D, A_KV_HEADS, 1, HEAD_DIM)] * 2,
                                    axis=2).reshape(D, 2 * nk)
    return jnp.concatenate([w[:, :nq], dup(w[:, nq:nq + nk]), dup(w[:, nq + nk:])],
                           axis=1).astype(BF16)


def _router_weights(rw):
    D, E = rw.shape
    return jnp.concatenate([rw, rw, rw, jnp.zeros((D, GATE_COLS - 3 * E), rw.dtype)], axis=1)


def kernel(x, c, ctx, c_ctx, ada_w, ada_b, norm1_g, norm2_g, final_g, win_w_in, win_w_out,
           win_sink, diff_w_in, diff_w_out, diff_lambda, diff_subln_g, na_w_in, na_w_out,
           na_rpb, router_w, w_gate, w_up, w_down):
    B, T, D = x.shape
    L = ctx.shape[1]
    S = T + L
    depth = ada_w.shape[0]
    assert L == TILE and T % TILE == 0 and (T // GRID_W) >= NA_WIN_ROWS
    xa = jnp.concatenate([x, ctx], axis=1)
    cs = jnp.concatenate([c, c_ctx[None, :]], axis=0)
    rope = _rope_tables(T, L)
    for i in range(depth):
        kind = i % N_MIXERS
        slot = i // N_MIXERS
        mod = _ada(cs, ada_w[i], ada_b[i][None, :])
        mod = jnp.stack([mod[:B], jnp.broadcast_to(mod[B:], (B, 6 * D))], axis=1)
        mod = mod.reshape(B, 2, 1, 6, D)
        sh1, sc1, g1, sh2, sc2, g2 = [mod[:, :, :, k, :] for k in range(6)]
        if kind == 0:
            w_in = _win_weights(win_w_in[slot])
            qkv = _norm_proj(xa, norm1_g[i][None, :], sh1, sc1, w_in, rope,
                             (16 + 2 * A_KV_HEADS) * HEAD_DIM)
            y = _attn_win(qkv, win_sink[slot], T)
            w_out = win_w_out[slot]
        elif kind == 1:
            lambda_init = 0.8 - 0.6 * math.exp(-0.3 * i)
            w_in = _scale_q(diff_w_in[slot], 16 * HEAD_DIM).astype(BF16)
            qkv = _norm_proj(xa, norm1_g[i][None, :], sh1, sc1, w_in, rope, 32 * HEAD_DIM)
            lp = diff_lambda[slot]
            lam = (jnp.exp(jnp.sum(lp[0] * lp[1])) - jnp.exp(jnp.sum(lp[2] * lp[3]))
                   + lambda_init).reshape(1)
            y = _attn_diff(qkv, lam, diff_subln_g[slot][None, :], T, 1.0 - lambda_init)
            w_out = diff_w_out[slot]
        else:
            w_in = _scale_q(na_w_in[slot], 16 * HEAD_DIM).astype(BF16)
            qkv = _norm_proj(xa, norm1_g[i][None, :], sh1, sc1, w_in, rope, 0)
            y = _attn_na(qkv, _na_bias_tables(na_rpb[slot], T), T)
            w_out = na_w_out[slot]
        xa, hx, aff = _out_router(y, w_out.astype(BF16), xa, g1, norm2_g[i][None, :],
                                  sh2, sc2, _router_weights(router_w[i]))
        tok_slot, start, cnt = _route(aff, T)
        first, n_pass = _slot_plan(start, cnt)
        cap_total = CAPACITY_FACTOR * S // N_EXPERTS
        xin = _dispatch(hx, tok_slot, first, n_pass, cap_total)
        ye = _ffn(xin, w_gate[i].astype(BF16), w_up[i].astype(BF16), w_down[i].astype(BF16))
        xa = _combine(ye, tok_slot, first, n_pass, xa, g2)
    return _final_norm(xa, final_g[None, :], T)
```

```python
import functools
import math

import numpy as np
import jax
import jax.numpy as jnp
from jax import lax
from jax.experimental import pallas as pl
from jax.experimental.pallas import tpu as pltpu

HEAD_DIM = 64
LANES = 128
GRID_W = 64
NA_ROWS = 8
NA_COLS = 16
A_WINDOW = 128
A_KV_HEADS = 4
N_MIXERS = 3
ROPE_BASE = 10000.0
ROPE_AXIS_DIM = HEAD_DIM // 2
N_EXPERTS = 16
CAPACITY_FACTOR = 2
RMS_EPS = 1e-6
NEG_INF = -1e30
LOG2E = math.log2(math.e)
TILE = 256
NA_TILE_ROWS = TILE // GRID_W
NA_WIN_ROWS = NA_TILE_ROWS + NA_ROWS
VMEM_LIMIT = 56 * 1024 * 1024
SLOT_WIN = 64
SLOT_ALIGN = 16
GATE_COLS = LANES

BF16 = jnp.bfloat16
F32 = jnp.float32


def _params(n_grid):
    return pltpu.CompilerParams(
        dimension_semantics=("arbitrary",) * n_grid, vmem_limit_bytes=VMEM_LIMIT)


def _split_bf16(a):
    hi = a.astype(BF16)
    lo = (a - hi.astype(F32)).astype(BF16)
    return hi, lo


def _dot(a, b):
    return jnp.dot(a, b, preferred_element_type=F32)


def _dot_nt(a, b):
    return lax.dot_general(a, b, (((1,), (1,)), ((), ())), preferred_element_type=F32)


def _dot_tn(a, b):
    return lax.dot_general(a, b, (((0,), (0,)), ((), ())), preferred_element_type=F32)


def _ada_kernel(c_ref, w_ref, b_ref, o_ref):
    c = c_ref[...]
    a = c * jax.nn.sigmoid(c)
    a_hi, a_lo = _split_bf16(a)
    w_hi, w_lo = _split_bf16(w_ref[...])
    o_ref[...] = _dot(a_hi, w_hi) + _dot(a_lo, w_hi) + _dot(a_hi, w_lo) + b_ref[...]


def _ada(cs, w, b):
    R, D = cs.shape
    N = w.shape[1]
    tn = 1024
    return pl.pallas_call(
        _ada_kernel,
        out_shape=jax.ShapeDtypeStruct((R, N), F32),
        grid=(N // tn,),
        in_specs=[pl.BlockSpec((R, D), lambda j: (0, 0)),
                  pl.BlockSpec((D, tn), lambda j: (0, j)),
                  pl.BlockSpec((1, tn), lambda j: (0, j))],
        out_specs=pl.BlockSpec((R, tn), lambda j: (0, j)),
        compiler_params=_params(1),
        name="ada",
    )(cs, w, b)


def _rms_mod(x, g, shift, scale):
    ms = jnp.mean(x * x, axis=-1, keepdims=True)
    y = x * lax.rsqrt(ms + RMS_EPS) * g
    return y * (1.0 + scale) + shift


def _norm_proj_kernel(x_ref, g_ref, sh_ref, sc_ref, w_ref, cos_ref, sa_ref, sb_ref, o_ref,
                      *, n_rope):
    h = _rms_mod(x_ref[0], g_ref[...], sh_ref[0, 0], sc_ref[0, 0]).astype(BF16)
    n_cols = w_ref.shape[1]
    chunk = 512
    for j in range(n_cols // chunk):
        acc = _dot(h, w_ref[:, j * chunk:(j + 1) * chunk])
        for t in range(chunk // LANES):
            col = j * chunk + t * LANES
            a = acc[:, t * LANES:(t + 1) * LANES]
            if col < n_rope:
                a = (a * cos_ref[...]
                     + pltpu.roll(a, LANES - 16, 1) * sa_ref[...]
                     + pltpu.roll(a, 16, 1) * sb_ref[...])
            o_ref[0, :, col:col + LANES] = a.astype(BF16)


def _norm_proj(xa, g, shift, scale, w, rope, n_rope):
    B, S, D = xa.shape
    N = w.shape[1]
    nt = S // TILE
    mod_spec = pl.BlockSpec((1, 1, 1, D), lambda b, t: (b, t // (nt - 1), 0, 0))
    rope_spec = pl.BlockSpec((TILE, LANES), lambda b, t: (t, 0))
    return pl.pallas_call(
        functools.partial(_norm_proj_kernel, n_rope=n_rope),
        out_shape=jax.ShapeDtypeStruct((B, S, N), BF16),
        grid=(B, nt),
        in_specs=[pl.BlockSpec((1, TILE, D), lambda b, t: (b, t, 0)),
                  pl.BlockSpec((1, D), lambda b, t: (0, 0)),
                  mod_spec, mod_spec,
                  pl.BlockSpec((D, N), lambda b, t: (0, 0)),
                  rope_spec, rope_spec, rope_spec],
        out_specs=pl.BlockSpec((1, TILE, N), lambda b, t: (b, t, 0)),
        compiler_params=_params(2),
        name="norm_proj",
    )(xa, g, shift, scale, w, *rope)


def _rope_tables(T, L):
    t = np.arange(T)
    pos = np.stack([t // GRID_W, t % GRID_W], axis=0).astype(np.float32)
    inv = (1.0 / (ROPE_BASE ** (np.arange(0, ROPE_AXIS_DIM, 2, dtype=np.float32)
                                / ROPE_AXIS_DIM))).astype(np.float32)
    d = np.arange(LANES) % HEAD_DIM
    axis = d // ROPE_AXIS_DIM
    half = (d % ROPE_AXIS_DIM) // (ROPE_AXIS_DIM // 2)
    freq = d % (ROPE_AXIS_DIM // 2)
    ang = jnp.asarray(pos[axis].T) * jnp.asarray(inv[freq])[None, :]
    cos = jnp.cos(ang)
    sin = jnp.sin(ang)
    first = jnp.asarray(half == 0)[None, :]
    sa = jnp.where(first, -sin, 0.0)
    sb = jnp.where(first, 0.0, sin)
    pad = lambda a, v: jnp.concatenate([a, jnp.full((L, LANES), v, F32)], axis=0)
    return pad(cos, 1.0), pad(sa, 0.0), pad(sb, 0.0)


def _half_masks(q):
    lane = lax.broadcasted_iota(jnp.int32, q.shape, 1)
    zero = jnp.zeros_like(q)
    return jnp.where(lane < HEAD_DIM, q, zero), jnp.where(lane >= HEAD_DIM, q, zero)


def _merge_halves(o_first, o_second):
    lane = lax.broadcasted_iota(jnp.int32, o_first.shape, 1)
    return jnp.where(lane < HEAD_DIM, o_first, o_second)


def _win_kernel(sink_ref, q_ref, k_ref, v_ref, o_ref, *, T, win):
    h = pl.program_id(1)
    t = pl.program_id(2)
    n_x = T // TILE
    is_x = t < n_x
    q = q_ref[0]
    qa, qb = _half_masks(q[:, :LANES])
    qc, qd = _half_masks(q[:, LANES:])
    qs = jnp.concatenate([qa, qb, qc, qd], axis=0)
    start = pl.multiple_of(jnp.clip(t * TILE - A_WINDOW, 0, T - win), LANES)
    kc = k_ref[0, T:, :]
    vc = v_ref[0, T:, :]
    kw = k_ref[0, pl.ds(start, win), :]
    vw = v_ref[0, pl.ds(start, win), :]
    s_c = _dot_nt(kc, qs)
    s_w = _dot_nt(kw, qs)
    kpos = start + lax.broadcasted_iota(jnp.int32, (win, TILE), 0)
    qpos = t * TILE + lax.broadcasted_iota(jnp.int32, (win, TILE), 1)
    band = (jnp.abs(qpos - kpos) <= A_WINDOW) & is_x
    s_w = jnp.where(jnp.concatenate([band] * 4, axis=1), s_w, NEG_INF)
    sink = jnp.concatenate(
        [jnp.full((1, TILE), sink_ref[h * 4 + g] * LOG2E, F32) for g in range(4)], axis=1)
    m = jnp.maximum(jnp.maximum(jnp.max(s_c, axis=0, keepdims=True),
                                jnp.max(s_w, axis=0, keepdims=True)), sink)
    ec = jnp.exp2(s_c - m)
    ew = jnp.exp2(s_w - m)
    r = 1.0 / (jnp.sum(ec, axis=0, keepdims=True) + jnp.sum(ew, axis=0, keepdims=True)
               + jnp.exp2(sink - m))
    o = ((_dot_tn(vc, ec.astype(BF16)) + _dot_tn(vw, ew.astype(BF16))) * r).T
    o_ref[0, :, :LANES] = _merge_halves(o[:TILE], o[TILE:2 * TILE]).astype(BF16)
    o_ref[0, :, LANES:] = _merge_halves(o[2 * TILE:3 * TILE], o[3 * TILE:]).astype(BF16)


def _attn_win(qkv, sink, T):
    B, S, _ = qkv.shape
    nt = S // TILE
    nq = 16 * HEAD_DIM
    kb = nq // LANES
    win = TILE + 2 * A_WINDOW
    return pl.pallas_call(
        functools.partial(_win_kernel, T=T, win=win),
        out_shape=jax.ShapeDtypeStruct((B, S, nq), BF16),
        grid=(B, A_KV_HEADS, nt),
        in_specs=[pl.BlockSpec(memory_space=pltpu.SMEM),
                  pl.BlockSpec((1, TILE, 2 * LANES), lambda b, h, t: (b, t, h)),
                  pl.BlockSpec((1, S, LANES), lambda b, h, t: (b, 0, kb + h)),
                  pl.BlockSpec((1, S, LANES), lambda b, h, t: (b, 0, kb + A_KV_HEADS + h))],
        out_specs=pl.BlockSpec((1, TILE, 2 * LANES), lambda b, h, t: (b, t, h)),
        compiler_params=_params(3),
        name="attn_win",
    )(sink, qkv, qkv, qkv)


def _diff_kernel(lam_ref, q_ref, k_ref, v_ref, g_ref, o_ref, *, T, out_scale):
    t = pl.program_id(2)
    n_x = T // TILE
    lam = lam_ref[0]
    qa, qb = _half_masks(q_ref[0])
    qs = jnp.concatenate([qa, qb], axis=0)

    def attend(kmat, vmat):
        s = _dot_nt(qs, kmat)
        m = jnp.max(s, axis=-1, keepdims=True)
        e = jnp.exp2(s - m)
        l = jnp.sum(e, axis=-1, keepdims=True)
        a = e[:TILE] - e[TILE:] * (lam * l[:TILE] / l[TILE:])
        o = _dot(a.astype(BF16), vmat) / l[:TILE]
        ms = jnp.mean(o * o, axis=-1, keepdims=True)
        y = o * lax.rsqrt(ms + RMS_EPS) * g_ref[...] * out_scale
        o_ref[0] = y.astype(BF16)

    @pl.when(t < n_x)
    def _():
        attend(k_ref[0], v_ref[0])

    @pl.when(t >= n_x)
    def _():
        attend(k_ref[0, T:, :], v_ref[0, T:, :])


def _attn_diff(qkv, lam, subln_g, T, out_scale):
    B, S, _ = qkv.shape
    nt = S // TILE
    H = 8
    return pl.pallas_call(
        functools.partial(_diff_kernel, T=T, out_scale=out_scale),
        out_shape=jax.ShapeDtypeStruct((B, S, H * LANES), BF16),
        grid=(B, H, nt),
        in_specs=[pl.BlockSpec(memory_space=pltpu.SMEM),
                  pl.BlockSpec((1, TILE, LANES), lambda b, h, t: (b, t, h)),
                  pl.BlockSpec((1, S, LANES), lambda b, h, t: (b, 0, H + h)),
                  pl.BlockSpec((1, S, LANES), lambda b, h, t: (b, 0, 2 * H + h)),
                  pl.BlockSpec((1, LANES), lambda b, h, t: (0, 0))],
        out_specs=pl.BlockSpec((1, TILE, LANES), lambda b, h, t: (b, t, h)),
        compiler_params=_params(3),
        name="attn_diff",
    )(lam, qkv, qkv, qkv, subln_g)


def _na_kernel(q_ref, k_ref, v_ref, bias_ref, o_ref, *, T):
    t = pl.program_id(2)
    n_x = T // TILE
    rows = T // GRID_W
    win = NA_WIN_ROWS * GRID_W
    cls = jnp.where(t >= n_x, 3, jnp.where(t == 0, 0, jnp.where(t == n_x - 1, 2, 1)))
    row0 = jnp.clip(t * NA_TILE_ROWS - NA_ROWS // 2, 0, rows - NA_WIN_ROWS)
    start = pl.multiple_of(row0 * GRID_W, GRID_W)
    qa, qb = _half_masks(q_ref[0])
    qs = jnp.concatenate([qa, qb], axis=0)
    kc = k_ref[0, T:, :]
    vc = v_ref[0, T:, :]
    kw = k_ref[0, pl.ds(start, win), :]
    vw = v_ref[0, pl.ds(start, win), :]
    s_c = _dot_nt(kc, qs)
    s_w = _dot_nt(kw, qs) + bias_ref[cls, 0]
    m = jnp.maximum(jnp.max(s_c, axis=0, keepdims=True), jnp.max(s_w, axis=0, keepdims=True))
    ec = jnp.exp2(s_c - m)
    ew = jnp.exp2(s_w - m)
    r = 1.0 / (jnp.sum(ec, axis=0, keepdims=True) + jnp.sum(ew, axis=0, keepdims=True))
    o = ((_dot_tn(vc, ec.astype(BF16)) + _dot_tn(vw, ew.astype(BF16))) * r).T
    o_ref[0] = _merge_halves(o[:TILE], o[TILE:]).astype(BF16)


def _attn_na(qkv, bias, T):
    B, S, _ = qkv.shape
    nt = S // TILE
    HP = 8
    win = NA_WIN_ROWS * GRID_W
    return pl.pallas_call(
        functools.partial(_na_kernel, T=T),
        out_shape=jax.ShapeDtypeStruct((B, S, HP * LANES), BF16),
        grid=(B, HP, nt),
        in_specs=[pl.BlockSpec((1, TILE, LANES), lambda b, h, t: (b, t, h)),
                  pl.BlockSpec((1, S, LANES), lambda b, h, t: (b, 0, HP + h)),
                  pl.BlockSpec((1, S, LANES), lambda b, h, t: (b, 0, 2 * HP + h)),
                  pl.BlockSpec((4, 1, win, 2 * TILE), lambda b, h, t: (0, h, 0, 0))],
        out_specs=pl.BlockSpec((1, TILE, LANES), lambda b, h, t: (b, t, h)),
        compiler_params=_params(3),
        name="attn_na",
    )(qkv, qkv, qkv, bias)


def _na_bias_tables(rpb, T):
    rows = T // GRID_W
    H = rpb.shape[0]
    win = NA_WIN_ROWS * GRID_W
    hi = lax.Precision.HIGHEST
    pick = lambda idx, n: jnp.asarray(idx[..., None] == np.arange(n), F32)
    c = np.arange(GRID_W)
    cs = np.clip(c - NA_COLS // 2, 0, GRID_W - NA_COLS)
    valid_c = (c[None, :] >= cs[:, None]) & (c[None, :] < cs[:, None] + NA_COLS)
    bidx_c = np.clip(c[None, :] - c[:, None] + NA_COLS - 1, 0, 2 * NA_COLS - 2)
    by_col = jnp.einsum('hrd,ckd->hrck', rpb, pick(bidx_c, 2 * NA_COLS - 1), precision=hi)
    i = np.arange(NA_TILE_ROWS)
    j = np.arange(NA_WIN_ROWS)
    tables = []
    for r0 in (0, NA_TILE_ROWS, rows - NA_TILE_ROWS):
        s = int(np.clip(r0 - NA_ROWS // 2, 0, rows - NA_WIN_ROWS))
        r = r0 + i
        rs = np.clip(r - NA_ROWS // 2, 0, rows - NA_ROWS)
        kr = s + j
        valid_r = (kr[None, :] >= rs[:, None]) & (kr[None, :] < rs[:, None] + NA_ROWS)
        bidx_r = np.clip(kr[None, :] - r[:, None] + NA_ROWS - 1, 0, 2 * NA_ROWS - 2)
        vals = jnp.einsum('hrck,ijr->hjkic', by_col, pick(bidx_r, 2 * NA_ROWS - 1), precision=hi)
        valid = valid_r.T[:, None, :, None] & valid_c.T[None, :, None, :]
        vals = jnp.where(jnp.asarray(valid)[None], vals, NEG_INF).reshape(H // 2, 2, win, TILE)
        tables.append(vals.transpose(0, 2, 1, 3).reshape(H // 2, win, 2 * TILE))
    tables.append(jnp.full_like(tables[0], NEG_INF))
    return jnp.stack(tables, axis=0).astype(F32) * LOG2E


def _out_router_kernel(y_ref, w_ref, x_ref, g1_ref, n2_ref, sh_ref, sc_ref, rw_ref,
                       xo_ref, h_ref, aff_ref):
    D = x_ref.shape[2]
    x = x_ref[0] + g1_ref[0, 0] * _dot(y_ref[0], w_ref[...])
    xo_ref[0] = x
    h = _rms_mod(x, n2_ref[...], sh_ref[0, 0], sc_ref[0, 0])
    h_hi, h_lo = _split_bf16(h)
    r_hi, r_lo = _split_bf16(rw_ref[...])
    logits = _dot(h_hi, r_hi) + _dot(h_lo, r_hi) + _dot(h_hi, r_lo)
    lane = lax.broadcasted_iota(jnp.int32, logits.shape, 1)
    first = lane < N_EXPERTS
    m = jnp.max(jnp.where(first, logits, -jnp.inf), axis=-1, keepdims=True)
    e = jnp.exp(logits - m)
    aff = e / jnp.sum(jnp.where(first, e, 0.0), axis=-1, keepdims=True)
    aff_ref[0] = aff[:, :N_EXPERTS]
    hi = aff.astype(BF16)
    rem = aff - hi.astype(F32)
    mid = rem.astype(BF16)
    lo = (rem - mid.astype(F32)).astype(BF16)
    zero = jnp.zeros_like(hi)
    pieces = jnp.where(first, hi, jnp.where(lane < 2 * N_EXPERTS, mid,
                                            jnp.where(lane < 3 * N_EXPERTS, lo, zero)))
    h_ref[0, :, :D] = h_hi
    h_ref[0, :, D:] = pieces


def _out_router(y, w_out, xa, gate1, n2g, shift2, scale2, rw3):
    B, S, D = xa.shape
    nt = S // TILE
    mod_spec = pl.BlockSpec((1, 1, 1, D), lambda b, t: (b, t // (nt - 1), 0, 0))
    tile_spec = pl.BlockSpec((1, TILE, D), lambda b, t: (b, t, 0))
    return pl.pallas_call(
        _out_router_kernel,
        out_shape=(jax.ShapeDtypeStruct((B, S, D), F32),
                   jax.ShapeDtypeStruct((B, S, D + GATE_COLS), BF16),
                   jax.ShapeDtypeStruct((B, S, N_EXPERTS), F32)),
        grid=(B, nt),
        in_specs=[tile_spec,
                  pl.BlockSpec((D, D), lambda b, t: (0, 0)),
                  tile_spec, mod_spec,
                  pl.BlockSpec((1, D), lambda b, t: (0, 0)),
                  mod_spec, mod_spec,
                  pl.BlockSpec((D, GATE_COLS), lambda b, t: (0, 0))],
        out_specs=(tile_spec,
                   pl.BlockSpec((1, TILE, D + GATE_COLS), lambda b, t: (b, t, 0)),
                   pl.BlockSpec((1, TILE, N_EXPERTS), lambda b, t: (b, t, 0))),
        compiler_params=_params(2),
        name="out_router",
    )(y, w_out, xa, gate1, n2g, shift2, scale2, rw3)


def _route_kernel(aff_ref, slot_ref, start_ref, cnt_ref, *, T, cap_x, cap_c):
    S = aff_ref.shape[1]
    bits = lax.bitcast_convert_type(aff_ref[0], jnp.int32)
    ri = lax.broadcasted_iota(jnp.int32, (TILE, TILE), 0)
    ci = lax.broadcasted_iota(jnp.int32, (TILE, TILE), 1)
    ltri = (ri > ci).astype(BF16)

    def count(mask):
        return jnp.sum(mask.astype(F32), axis=0, keepdims=True)

    for lo_row, hi_row, cap, base in ((0, T, cap_x, 0), (T, S, cap_c, cap_x)):
        b = bits[lo_row:hi_row]

        def step(i, thr):
            cand = thr | lax.shift_left(jnp.int32(1), 30 - i)
            return jnp.where(count(b >= cand) >= cap, cand, thr)

        thr = lax.fori_loop(0, 31, step, jnp.zeros((1, N_EXPERTS), jnp.int32))
        need = cap - count(b > thr)
        seen_eq = jnp.zeros((1, N_EXPERTS), F32)
        seen = jnp.zeros((1, N_EXPERTS), F32)
        for j in range((hi_row - lo_row) // TILE):
            blk = b[j * TILE:(j + 1) * TILE]
            gt = blk > thr
            eq = blk == thr
            eq_rank = _dot(ltri, eq.astype(BF16)) + seen_eq
            sel = gt | (eq & (eq_rank < need))
            pos = _dot(ltri, sel.astype(BF16)) + seen + base
            t = lo_row // TILE + j
            slot_ref[0, t * TILE:(t + 1) * TILE, :] = jnp.where(sel, pos, -1.0)
            n_sel = count(sel)
            start_ref[0, t:t + 1, :] = seen + base
            cnt_ref[0, t:t + 1, :] = n_sel
            seen_eq = seen_eq + count(eq)
            seen = seen + n_sel


def _route(aff, T):
    B, S, E = aff.shape
    nt = S // TILE
    cap_x = CAPACITY_FACTOR * T // N_EXPERTS
    cap_c = CAPACITY_FACTOR * (S - T) // N_EXPERTS
    plan = jax.ShapeDtypeStruct((B, nt, E), F32)
    return pl.pallas_call(
        functools.partial(_route_kernel, T=T, cap_x=cap_x, cap_c=cap_c),
        out_shape=(jax.ShapeDtypeStruct((B, S, E), F32), plan, plan),
        grid=(B,),
        in_specs=[pl.BlockSpec((1, S, E), lambda b: (b, 0, 0))],
        out_specs=(pl.BlockSpec((1, S, E), lambda b: (b, 0, 0)),
                   pl.BlockSpec((1, nt, E), lambda b: (b, 0, 0)),
                   pl.BlockSpec((1, nt, E), lambda b: (b, 0, 0))),
        compiler_params=_params(1),
        name="route",
    )(aff)


def _slot_plan(start, cnt):
    start = start.astype(jnp.int32)
    cnt = cnt.astype(jnp.int32)
    first = (start // SLOT_ALIGN) * SLOT_ALIGN
    n_pass = jnp.max((start - first + cnt + SLOT_WIN - 1) // SLOT_WIN, axis=-1)
    return first, n_pass.astype(jnp.int32)


def _window_rel(slot, first, k, cap_total):
    nominal = first + k * SLOT_WIN
    begin = jnp.minimum(nominal, float(cap_total - SLOT_WIN))
    rel = slot - nominal
    return jnp.where((rel >= 0) & (rel < SLOT_WIN), rel + (nominal - begin), 255.0)


def _window_begin(first_s, k, cap_total):
    return pl.multiple_of(jnp.minimum(first_s + k * SLOT_WIN, cap_total - SLOT_WIN), SLOT_ALIGN)


def _dispatch_kernel(first_s, npass_s, slot_ref, first_ref, h_ref, o_ref):
    b, half, t = pl.program_id(0), pl.program_id(1), pl.program_id(2)
    nt = pl.num_programs(2)
    n_e = o_ref.shape[1]
    cap_total = o_ref.shape[2]

    @pl.when(t == 0)
    def _():
        o_ref[...] = jnp.zeros_like(o_ref)

    slot = slot_ref[0]
    first = first_ref[0, 0]
    pick_r = lax.broadcasted_iota(jnp.int32, (n_e, N_EXPERTS), 0)
    pick_c = lax.broadcasted_iota(jnp.int32, (n_e, N_EXPERTS), 1)
    pick = (pick_c == pick_r + half * n_e).astype(BF16)
    sub = lax.broadcasted_iota(jnp.int32, (SLOT_WIN, TILE), 0).astype(F32)

    def one_pass(k, carry):
        rel = _window_rel(slot, first, k.astype(F32), cap_total).astype(BF16)
        rel_t = _dot_nt(pick, rel)
        onehot = jnp.concatenate(
            [(jnp.broadcast_to(rel_t[e:e + 1, :], (SLOT_WIN, TILE)) == sub).astype(BF16)
             for e in range(n_e)], axis=0)
        rows = _dot(onehot, h_ref[0])
        for e in range(n_e):
            base = (b * nt + t) * N_EXPERTS + half * n_e + e
            win = pl.ds(_window_begin(first_s[base], k, cap_total), SLOT_WIN)
            cur = o_ref[0, e, win, :].astype(F32)
            o_ref[0, e, win, :] = (cur + rows[e * SLOT_WIN:(e + 1) * SLOT_WIN]).astype(BF16)
        return carry

    lax.fori_loop(0, npass_s[b * nt + t], one_pass, 0)


def _dispatch(hx, slot, first, n_pass, cap_total):
    B, S, W = hx.shape
    nt = S // TILE
    halves = 2
    n_e = N_EXPERTS // halves
    grid_spec = pltpu.PrefetchScalarGridSpec(
        num_scalar_prefetch=2,
        grid=(B, halves, nt),
        in_specs=[pl.BlockSpec((1, TILE, N_EXPERTS), lambda b, h, t, *_: (b, t, 0)),
                  pl.BlockSpec((1, 1, 1, N_EXPERTS), lambda b, h, t, *_: (b, t, 0, 0)),
                  pl.BlockSpec((1, TILE, W), lambda b, h, t, *_: (b, t, 0))],
        out_specs=pl.BlockSpec((1, n_e, cap_total, W), lambda b, h, t, *_: (b, h, 0, 0)),
    )
    return pl.pallas_call(
        _dispatch_kernel,
        out_shape=jax.ShapeDtypeStruct((B, N_EXPERTS, cap_total, W), BF16),
        grid_spec=grid_spec,
        compiler_params=_params(3),
        name="dispatch",
    )(first.reshape(-1), n_pass.reshape(-1), slot,
      first.astype(F32).reshape(B, nt, 1, N_EXPERTS), hx)


def _ffn_kernel(x_ref, wg_ref, wu_ref, wd_ref, o_ref):
    D = wg_ref.shape[1]
    ff = wg_ref.shape[2]
    x = x_ref[0, 0, :, :D]
    pieces = x_ref[0, 0, :, D:].astype(F32)
    lane = lax.broadcasted_iota(jnp.int32, pieces.shape, 1)
    mine = ((lane & (N_EXPERTS - 1)) == pl.program_id(0)) & (lane < 3 * N_EXPERTS)
    gate = jnp.sum(jnp.where(mine, pieces, 0.0), axis=-1, keepdims=True)
    chunk = 512
    acc = jnp.zeros((x.shape[0], D), F32)
    for j in range(ff // chunk):
        cols = slice(j * chunk, (j + 1) * chunk)
        g = _dot(x, wg_ref[0, :, cols])
        u = _dot(x, wu_ref[0, :, cols])
        hid = (g * jax.nn.sigmoid(g) * u).astype(BF16)
        acc = acc + _dot(hid, wd_ref[0, cols, :])
    o_ref[0, 0] = (acc * gate).astype(BF16)


def _ffn(xin, wg, wu, wd):
    B, E, S, W = xin.shape
    D, FF = wg.shape[1], wg.shape[2]
    return pl.pallas_call(
        _ffn_kernel,
        out_shape=jax.ShapeDtypeStruct((B, E, S, D), BF16),
        grid=(E, B),
        in_specs=[pl.BlockSpec((1, 1, S, W), lambda e, b: (b, e, 0, 0)),
                  pl.BlockSpec((1, D, FF), lambda e, b: (e, 0, 0)),
                  pl.BlockSpec((1, D, FF), lambda e, b: (e, 0, 0)),
                  pl.BlockSpec((1, FF, D), lambda e, b: (e, 0, 0))],
        out_specs=pl.BlockSpec((1, 1, S, D), lambda e, b: (b, e, 0, 0)),
        compiler_params=_params(2),
        name="expert_ffn",
    )(xin, wg, wu, wd)


def _combine_kernel(first_s, npass_s, slot_ref, first_ref, expand_ref, y_ref, x_ref, g2_ref,
                    o_ref):
    b, t = pl.program_id(0), pl.program_id(1)
    nt = pl.num_programs(1)
    cap_total = y_ref.shape[2]
    o_ref[0] = x_ref[0]
    slot = slot_ref[0]
    first = first_ref[0, 0]
    lane_row = (lax.broadcasted_iota(jnp.int32, (TILE, N_EXPERTS * SLOT_WIN), 1)
                & (SLOT_WIN - 1)).astype(F32)

    def one_pass(k, carry):
        rel = _window_rel(slot, first, k.astype(F32), cap_total).astype(BF16)
        onehot = (_dot(rel, expand_ref[...]) == lane_row).astype(BF16)
        rows = jnp.concatenate(
            [y_ref[0, e, pl.ds(_window_begin(first_s[(b * nt + t) * N_EXPERTS + e], k,
                                             cap_total), SLOT_WIN), :]
             for e in range(N_EXPERTS)], axis=0)
        o_ref[0] += g2_ref[0, 0] * _dot(onehot, rows)
        return carry

    lax.fori_loop(0, npass_s[b * nt + t], one_pass, 0)


def _combine(y, slot, first, n_pass, xa, gate2):
    B, S, D = xa.shape
    nt = S // TILE
    E, cap_total = y.shape[1], y.shape[2]
    expand = np.repeat(np.eye(E, dtype=np.float32), SLOT_WIN, axis=1)
    grid_spec = pltpu.PrefetchScalarGridSpec(
        num_scalar_prefetch=2,
        grid=(B, nt),
        in_specs=[pl.BlockSpec((1, TILE, E), lambda b, t, *_: (b, t, 0)),
                  pl.BlockSpec((1, 1, 1, E), lambda b, t, *_: (b, t, 0, 0)),
                  pl.BlockSpec((E, E * SLOT_WIN), lambda b, t, *_: (0, 0)),
                  pl.BlockSpec((1, E, cap_total, D), lambda b, t, *_: (b, 0, 0, 0)),
                  pl.BlockSpec((1, TILE, D), lambda b, t, *_: (b, t, 0)),
                  pl.BlockSpec((1, 1, 1, D), lambda b, t, *_: (b, t // (nt - 1), 0, 0))],
        out_specs=pl.BlockSpec((1, TILE, D), lambda b, t, *_: (b, t, 0)),
    )
    return pl.pallas_call(
        _combine_kernel,
        out_shape=jax.ShapeDtypeStruct((B, S, D), F32),
        grid_spec=grid_spec,
        compiler_params=_params(2),
        name="combine",
    )(first.reshape(-1), n_pass.reshape(-1), slot,
      first.astype(F32).reshape(B, nt, 1, E), jnp.asarray(expand, BF16), y, xa, gate2)


def _final_norm_kernel(x_ref, g_ref, o_ref):
    x = x_ref[0]
    ms = jnp.mean(x * x, axis=-1, keepdims=True)
    o_ref[0] = x * lax.rsqrt(ms + RMS_EPS) * g_ref[...]


def _final_norm(xa, g, T):
    B, S, D = xa.shape
    return pl.pallas_call(
        _final_norm_kernel,
        out_shape=jax.ShapeDtypeStruct((B, T, D), F32),
        grid=(B, T // TILE),
        in_specs=[pl.BlockSpec((1, TILE, D), lambda b, t: (b, t, 0)),
                  pl.BlockSpec((1, D), lambda b, t: (0, 0))],
        out_specs=pl.BlockSpec((1, TILE, D), lambda b, t: (b, t, 0)),
        compiler_params=_params(2),
        name="final_norm",
    )(xa, g)


def _scale_q(w, nq):
    return jnp.concatenate([w[:, :nq] * (HEAD_DIM ** -0.5 * LOG2E), w[:, nq:]], axis=1)


def _win_weights(w_in):
    D = w_in.shape[0]
    nq = 16 * HEAD_DIM
    nk = A_KV_HEADS * HEAD_DIM
    w = _scale_q(w_in, nq)
    dup = lambda m: jnp.concatenate([m.reshape(D, A_KV_HEADS, 1, HEAD_DIM)] * 2,
                                    axis=2).reshape(D, 2 * nk)
    return jnp.concatenate([w[:, :nq], dup(w[:, nq:nq + nk]), dup(w[:, nq + nk:])],
                           axis=1).astype(BF16)


def _router_weights(rw):
    D, E = rw.shape
    return jnp.concatenate([rw, rw, rw, jnp.zeros((D, GATE_COLS - 3 * E), rw.dtype)], axis=1)


def kernel(x, c, ctx, c_ctx, ada_w, ada_b, norm1_g, norm2_g, final_g, win_w_in, win_w_out,
           win_sink, diff_w_in, diff_w_out, diff_lambda, diff_subln_g, na_w_in, na_w_out,
           na_rpb, router_w, w_gate, w_up, w_down):
    B, T, D = x.shape
    L = ctx.shape[1]
    S = T + L
    depth = ada_w.shape[0]
    assert L == TILE and T % TILE == 0 and (T // GRID_W) >= NA_WIN_ROWS
    xa = jnp.concatenate([x, ctx], axis=1)
    cs = jnp.concatenate([c, c_ctx[None, :]], axis=0)
    rope = _rope_tables(T, L)
    for i in range(depth):
        kind = i % N_MIXERS
        slot = i // N_MIXERS
        mod = _ada(cs, ada_w[i], ada_b[i][None, :])
        mod = jnp.stack([mod[:B], jnp.broadcast_to(mod[B:], (B, 6 * D))], axis=1)
        mod = mod.reshape(B, 2, 1, 6, D)
        sh1, sc1, g1, sh2, sc2, g2 = [mod[:, :, :, k, :] for k in range(6)]
        if kind == 0:
            w_in = _win_weights(win_w_in[slot])
            qkv = _norm_proj(xa, norm1_g[i][None, :], sh1, sc1, w_in, rope,
                             (16 + 2 * A_KV_HEADS) * HEAD_DIM)
            y = _attn_win(qkv, win_sink[slot], T)
            w_out = win_w_out[slot]
        elif kind == 1:
            lambda_init = 0.8 - 0.6 * math.exp(-0.3 * i)
            w_in = _scale_q(diff_w_in[slot], 16 * HEAD_DIM).astype(BF16)
            qkv = _norm_proj(xa, norm1_g[i][None, :], sh1, sc1, w_in, rope, 32 * HEAD_DIM)
            lp = diff_lambda[slot]
            lam = (jnp.exp(jnp.sum(lp[0] * lp[1])) - jnp.exp(jnp.sum(lp[2] * lp[3]))
                   + lambda_init).reshape(1)
            y = _attn_diff(qkv, lam, diff_subln_g[slot][None, :], T, 1.0 - lambda_init)
            w_out = diff_w_out[slot]
        else:
            w_in = _scale_q(na_w_in[slot], 16 * HEAD_DIM).astype(BF16)
            qkv = _norm_proj(xa, norm1_g[i][None, :], sh1, sc1, w_in, rope, 0)
            y = _attn_na(qkv, _na_bias_tables(na_rpb[slot], T), T)
            w_out = na_w_out[slot]
        xa, hx, aff = _out_router(y, w_out.astype(BF16), xa, g1, norm2_g[i][None, :],
                                  sh2, sc2, _router_weights(router_w[i]))
        tok_slot, start, cnt = _route(aff, T)
        first, n_pass = _slot_plan(start, cnt)
        cap_total = CAPACITY_FACTOR * S // N_EXPERTS
        xin = _dispatch(hx, tok_slot, first, n_pass, cap_total)
        ye = _ffn(xin, w_gate[i].astype(BF16), w_up[i].astype(BF16), w_down[i].astype(BF16))
        xa = _combine(ye, tok_slot, first, n_pass, xa, g2)
    return _final_norm(xa, final_g[None, :], T)
```

```python
import functools
import math

import numpy as np
import jax
import jax.numpy as jnp
from jax import lax
from jax.experimental import pallas as pl
from jax.experimental.pallas import tpu as pltpu

HEAD_DIM = 64
LANES = 128
GRID_W = 64
NA_ROWS = 8
NA_COLS = 16
A_WINDOW = 128
A_KV_HEADS = 4
N_MIXERS = 3
ROPE_BASE = 10000.0
ROPE_AXIS_DIM = HEAD_DIM // 2
N_EXPERTS = 16
CAPACITY_FACTOR = 2
RMS_EPS = 1e-6
NEG_INF = -1e30
LOG2E = math.log2(math.e)
TILE = 256
NA_TILE_ROWS = TILE // GRID_W
NA_WIN_ROWS = NA_TILE_ROWS + NA_ROWS
VMEM_LIMIT = 56 * 1024 * 1024
SLOT_WIN = 64
SLOT_ALIGN = 16
GATE_COLS = LANES

BF16 = jnp.bfloat16
F32 = jnp.float32


def _params(n_grid):
    return pltpu.CompilerParams(
        dimension_semantics=("arbitrary",) * n_grid, vmem_limit_bytes=VMEM_LIMIT)


def _split_bf16(a):
    hi = a.astype(BF16)
    lo = (a - hi.astype(F32)).astype(BF16)
    return hi, lo


def _dot(a, b):
    return jnp.dot(a, b, preferred_element_type=F32)


def _dot_nt(a, b):
    return lax.dot_general(a, b, (((1,), (1,)), ((), ())), preferred_element_type=F32)


def _dot_tn(a, b):
    return lax.dot_general(a, b, (((0,), (0,)), ((), ())), preferred_element_type=F32)


def _ada_kernel(c_ref, w_ref, b_ref, o_ref):
    c = c_ref[...]
    a = c * jax.nn.sigmoid(c)
    a_hi, a_lo = _split_bf16(a)
    w_hi, w_lo = _split_bf16(w_ref[...])
    o_ref[...] = _dot(a_hi, w_hi) + _dot(a_lo, w_hi) + _dot(a_hi, w_lo) + b_ref[...]


def _ada(cs, w, b):
    R, D = cs.shape
    N = w.shape[1]
    tn = 1024
    return pl.pallas_call(
        _ada_kernel,
        out_shape=jax.ShapeDtypeStruct((R, N), F32),
        grid=(N // tn,),
        in_specs=[pl.BlockSpec((R, D), lambda j: (0, 0)),
                  pl.BlockSpec((D, tn), lambda j: (0, j)),
                  pl.BlockSpec((1, tn), lambda j: (0, j))],
        out_specs=pl.BlockSpec((R, tn), lambda j: (0, j)),
        compiler_params=_params(1),
        name="ada",
    )(cs, w, b)


def _rms_mod(x, g, shift, scale):
    ms = jnp.mean(x * x, axis=-1, keepdims=True)
    y = x * lax.rsqrt(ms + RMS_EPS) * g
    return y * (1.0 + scale) + shift


def _norm_proj_kernel(x_ref, g_ref, sh_ref, sc_ref, w_ref, cos_ref, sa_ref, sb_ref, o_ref,
                      *, n_rope):
    h = _rms_mod(x_ref[0], g_ref[...], sh_ref[0, 0], sc_ref[0, 0]).astype(BF16)
    n_cols = w_ref.shape[1]
    chunk = 512
    for j in range(n_cols // chunk):
        acc = _dot(h, w_ref[:, j * chunk:(j + 1) * chunk])
        for t in range(chunk // LANES):
            col = j * chunk + t * LANES
            a = acc[:, t * LANES:(t + 1) * LANES]
            if col < n_rope:
                a = (a * cos_ref[...]
                     + pltpu.roll(a, LANES - 16, 1) * sa_ref[...]
                     + pltpu.roll(a, 16, 1) * sb_ref[...])
            o_ref[0, :, col:col + LANES] = a.astype(BF16)


def _norm_proj(xa, g, shift, scale, w, rope, n_rope):
    B, S, D = xa.shape
    N = w.shape[1]
    nt = S // TILE
    mod_spec = pl.BlockSpec((1, 1, 1, D), lambda b, t: (b, t // (nt - 1), 0, 0))
    rope_spec = pl.BlockSpec((TILE, LANES), lambda b, t: (t, 0))
    return pl.pallas_call(
        functools.partial(_norm_proj_kernel, n_rope=n_rope),
        out_shape=jax.ShapeDtypeStruct((B, S, N), BF16),
        grid=(B, nt),
        in_specs=[pl.BlockSpec((1, TILE, D), lambda b, t: (b, t, 0)),
                  pl.BlockSpec((1, D), lambda b, t: (0, 0)),
                  mod_spec, mod_spec,
                  pl.BlockSpec((D, N), lambda b, t: (0, 0)),
                  rope_spec, rope_spec, rope_spec],
        out_specs=pl.BlockSpec((1, TILE, N), lambda b, t: (b, t, 0)),
        compiler_params=_params(2),
        name="norm_proj",
    )(xa, g, shift, scale, w, *rope)


def _rope_tables(T, L):
    t = np.arange(T)
    pos = np.stack([t // GRID_W, t % GRID_W], axis=0).astype(np.float32)
    inv = (1.0 / (ROPE_BASE ** (np.arange(0, ROPE_AXIS_DIM, 2, dtype=np.float32)
                                / ROPE_AXIS_DIM))).astype(np.float32)
    d = np.arange(LANES) % HEAD_DIM
    axis = d // ROPE_AXIS_DIM
    half = (d % ROPE_AXIS_DIM) // (ROPE_AXIS_DIM // 2)
    freq = d % (ROPE_AXIS_DIM // 2)
    ang = jnp.asarray(pos[axis].T) * jnp.asarray(inv[freq])[None, :]
    cos = jnp.cos(ang)
    sin = jnp.sin(ang)
    first = jnp.asarray(half == 0)[None, :]
    sa = jnp.where(first, -sin, 0.0)
    sb = jnp.where(first, 0.0, sin)
    pad = lambda a, v: jnp.concatenate([a, jnp.full((L, LANES), v, F32)], axis=0)
    return pad(cos, 1.0), pad(sa, 0.0), pad(sb, 0.0)


def _half_masks(q):
    lane = lax.broadcasted_iota(jnp.int32, q.shape, 1)
    zero = jnp.zeros_like(q)
    return jnp.where(lane < HEAD_DIM, q, zero), jnp.where(lane >= HEAD_DIM, q, zero)


def _merge_halves(o_first, o_second):
    lane = lax.broadcasted_iota(jnp.int32, o_first.shape, 1)
    return jnp.where(lane < HEAD_DIM, o_first, o_second)


WIN_KV_PER_STEP = 4


def _win_kernel(sink_ref, q_ref, k_ref, v_ref, o_ref, *, T, win):
    hb = pl.program_id(1)
    t = pl.program_id(2)
    n_x = T // TILE
    is_x = t < n_x
    start = pl.multiple_of(jnp.clip(t * TILE - A_WINDOW, 0, T - win), LANES)
    kpos = start + lax.broadcasted_iota(jnp.int32, (win, TILE), 0)
    qpos = t * TILE + lax.broadcasted_iota(jnp.int32, (win, TILE), 1)
    band = (jnp.abs(qpos - kpos) <= A_WINDOW) & is_x
    band = jnp.concatenate([band] * 4, axis=1)
    scores = []
    for j in range(WIN_KV_PER_STEP):
        q = q_ref[0, :, 2 * j * LANES:2 * (j + 1) * LANES]
        qa, qb = _half_masks(q[:, :LANES])
        qc, qd = _half_masks(q[:, LANES:])
        qs = jnp.concatenate([qa, qb, qc, qd], axis=0)
        cols = slice(j * LANES, (j + 1) * LANES)
        scores.append((_dot_nt(k_ref[0, T:, cols], qs),
                       _dot_nt(k_ref[0, pl.ds(start, win), cols], qs)))
    for j, (s_c, s_w) in enumerate(scores):
        cols = slice(j * LANES, (j + 1) * LANES)
        head0 = (hb * WIN_KV_PER_STEP + j) * 4
        s_w = jnp.where(band, s_w, NEG_INF)
        sink = jnp.concatenate(
            [jnp.full((1, TILE), sink_ref[head0 + g] * LOG2E, F32) for g in range(4)], axis=1)
        m = jnp.maximum(jnp.maximum(jnp.max(s_c, axis=0, keepdims=True),
                                    jnp.max(s_w, axis=0, keepdims=True)), sink)
        ec = jnp.exp2(s_c - m)
        ew = jnp.exp2(s_w - m)
        r = 1.0 / (jnp.sum(ec, axis=0, keepdims=True) + jnp.sum(ew, axis=0, keepdims=True)
                   + jnp.exp2(sink - m))
        o = ((_dot_tn(v_ref[0, T:, cols], ec.astype(BF16))
              + _dot_tn(v_ref[0, pl.ds(start, win), cols], ew.astype(BF16))) * r).T
        first = 2 * j * LANES
        o_ref[0, :, first:first + LANES] = _merge_halves(o[:TILE], o[TILE:2 * TILE]).astype(BF16)
        o_ref[0, :, first + LANES:first + 2 * LANES] = _merge_halves(
            o[2 * TILE:3 * TILE], o[3 * TILE:]).astype(BF16)


def _attn_win(qkv, sink, T):
    B, S, _ = qkv.shape
    nt = S // TILE
    nq = 16 * HEAD_DIM
    win = TILE + 2 * A_WINDOW
    hb = A_KV_HEADS // WIN_KV_PER_STEP
    wq = WIN_KV_PER_STEP * 2 * LANES
    wk = WIN_KV_PER_STEP * LANES
    kb = nq // wk
    return pl.pallas_call(
        functools.partial(_win_kernel, T=T, win=win),
        out_shape=jax.ShapeDtypeStruct((B, S, nq), BF16),
        grid=(B, hb, nt),
        in_specs=[pl.BlockSpec(memory_space=pltpu.SMEM),
                  pl.BlockSpec((1, TILE, wq), lambda b, h, t: (b, t, h)),
                  pl.BlockSpec((1, S, wk), lambda b, h, t: (b, 0, kb + h)),
                  pl.BlockSpec((1, S, wk), lambda b, h, t: (b, 0, kb + hb + h))],
        out_specs=pl.BlockSpec((1, TILE, wq), lambda b, h, t: (b, t, h)),
        compiler_params=_params(3),
        name="attn_win",
    )(sink, qkv, qkv, qkv)


DIFF_HEADS_PER_STEP = 2


def _diff_kernel(lam_ref, q_ref, k_ref, v_ref, g_ref, o_ref, *, T, out_scale):
    t = pl.program_id(2)
    n_x = T // TILE
    lam = lam_ref[0]
    heads = [slice(j * LANES, (j + 1) * LANES) for j in range(DIFF_HEADS_PER_STEP)]

    def attend(keys):
        scores = []
        for cols in heads:
            qa, qb = _half_masks(q_ref[0, :, cols])
            qs = jnp.concatenate([qa, qb], axis=0)
            scores.append(_dot_nt(qs, k_ref[0, keys, cols]))
        for cols, s in zip(heads, scores):
            m = jnp.max(s, axis=-1, keepdims=True)
            e = jnp.exp2(s - m)
            l = jnp.sum(e, axis=-1, keepdims=True)
            a = e[:TILE] - e[TILE:] * (lam * l[:TILE] / l[TILE:])
            o = _dot(a.astype(BF16), v_ref[0, keys, cols]) / l[:TILE]
            ms = jnp.mean(o * o, axis=-1, keepdims=True)
            y = o * lax.rsqrt(ms + RMS_EPS) * g_ref[...] * out_scale
            o_ref[0, :, cols] = y.astype(BF16)

    @pl.when(t < n_x)
    def _():
        attend(slice(None))

    @pl.when(t >= n_x)
    def _():
        attend(slice(T, None))


def _attn_diff(qkv, lam, subln_g, T, out_scale):
    B, S, _ = qkv.shape
    nt = S // TILE
    H = 8
    hb = H // DIFF_HEADS_PER_STEP
    w = DIFF_HEADS_PER_STEP * LANES
    return pl.pallas_call(
        functools.partial(_diff_kernel, T=T, out_scale=out_scale),
        out_shape=jax.ShapeDtypeStruct((B, S, H * LANES), BF16),
        grid=(B, hb, nt),
        in_specs=[pl.BlockSpec(memory_space=pltpu.SMEM),
                  pl.BlockSpec((1, TILE, w), lambda b, h, t: (b, t, h)),
                  pl.BlockSpec((1, S, w), lambda b, h, t: (b, 0, hb + h)),
                  pl.BlockSpec((1, S, w), lambda b, h, t: (b, 0, 2 * hb + h)),
                  pl.BlockSpec((1, LANES), lambda b, h, t: (0, 0))],
        out_specs=pl.BlockSpec((1, TILE, w), lambda b, h, t: (b, t, h)),
        compiler_params=_params(3),
        name="attn_diff",
    )(lam, qkv, qkv, qkv, subln_g)


NA_PAIRS_PER_STEP = 2


def _na_kernel(q_ref, k_ref, v_ref, bias_ref, o_ref, *, T):
    t = pl.program_id(2)
    n_x = T // TILE
    rows = T // GRID_W
    win = NA_WIN_ROWS * GRID_W
    cls = jnp.where(t >= n_x, 3, jnp.where(t == 0, 0, jnp.where(t == n_x - 1, 2, 1)))
    row0 = jnp.clip(t * NA_TILE_ROWS - NA_ROWS // 2, 0, rows - NA_WIN_ROWS)
    start = pl.multiple_of(row0 * GRID_W, GRID_W)
    scores = []
    for j in range(NA_PAIRS_PER_STEP):
        cols = slice(j * LANES, (j + 1) * LANES)
        qa, qb = _half_masks(q_ref[0, :, cols])
        qs = jnp.concatenate([qa, qb], axis=0)
        scores.append((_dot_nt(k_ref[0, T:, cols], qs),
                       _dot_nt(k_ref[0, pl.ds(start, win), cols], qs)))
    for j, (s_c, s_w) in enumerate(scores):
        cols = slice(j * LANES, (j + 1) * LANES)
        s_w = s_w + bias_ref[cls, j]
        m = jnp.maximum(jnp.max(s_c, axis=0, keepdims=True), jnp.max(s_w, axis=0, keepdims=True))
        ec = jnp.exp2(s_c - m)
        ew = jnp.exp2(s_w - m)
        r = 1.0 / (jnp.sum(ec, axis=0, keepdims=True) + jnp.sum(ew, axis=0, keepdims=True))
        o = ((_dot_tn(v_ref[0, T:, cols], ec.astype(BF16))
              + _dot_tn(v_ref[0, pl.ds(start, win), cols], ew.astype(BF16))) * r).T
        o_ref[0, :, cols] = _merge_halves(o[:TILE], o[TILE:]).astype(BF16)


def _attn_na(qkv, bias, T):
    B, S, _ = qkv.shape
    nt = S // TILE
    n = NA_PAIRS_PER_STEP
    hb = 8 // n
    w = n * LANES
    win = NA_WIN_ROWS * GRID_W
    return pl.pallas_call(
        functools.partial(_na_kernel, T=T),
        out_shape=jax.ShapeDtypeStruct((B, S, 8 * LANES), BF16),
        grid=(B, hb, nt),
        in_specs=[pl.BlockSpec((1, TILE, w), lambda b, h, t: (b, t, h)),
                  pl.BlockSpec((1, S, w), lambda b, h, t: (b, 0, hb + h)),
                  pl.BlockSpec((1, S, w), lambda b, h, t: (b, 0, 2 * hb + h)),
                  pl.BlockSpec((4, n, win, 2 * TILE), lambda b, h, t: (0, h, 0, 0))],
        out_specs=pl.BlockSpec((1, TILE, w), lambda b, h, t: (b, t, h)),
        compiler_params=_params(3),
        name="attn_na",
    )(qkv, qkv, qkv, bias)


def _na_bias_tables(rpb, T):
    rows = T // GRID_W
    H = rpb.shape[0]
    win = NA_WIN_ROWS * GRID_W
    hi = lax.Precision.HIGHEST
    pick = lambda idx, n: jnp.asarray(idx[..., None] == np.arange(n), F32)
    c = np.arange(GRID_W)
    cs = np.clip(c - NA_COLS // 2, 0, GRID_W - NA_COLS)
    valid_c = (c[None, :] >= cs[:, None]) & (c[None, :] < cs[:, None] + NA_COLS)
    bidx_c = np.clip(c[None, :] - c[:, None] + NA_COLS - 1, 0, 2 * NA_COLS - 2)
    by_col = jnp.einsum('hrd,ckd->hrck', rpb, pick(bidx_c, 2 * NA_COLS - 1), precision=hi)
    i = np.arange(NA_TILE_ROWS)
    j = np.arange(NA_WIN_ROWS)
    tables = []
    for r0 in (0, NA_TILE_ROWS, rows - NA_TILE_ROWS):
        s = int(np.clip(r0 - NA_ROWS // 2, 0, rows - NA_WIN_ROWS))
        r = r0 + i
        rs = np.clip(r - NA_ROWS // 2, 0, rows - NA_ROWS)
        kr = s + j
        valid_r = (kr[None, :] >= rs[:, None]) & (kr[None, :] < rs[:, None] + NA_ROWS)
        bidx_r = np.clip(kr[None, :] - r[:, None] + NA_ROWS - 1, 0, 2 * NA_ROWS - 2)
        vals = jnp.einsum('hrck,ijr->hjkic', by_col, pick(bidx_r, 2 * NA_ROWS - 1), precision=hi)
        valid = valid_r.T[:, None, :, None] & valid_c.T[None, :, None, :]
        vals = jnp.where(jnp.asarray(valid)[None], vals, NEG_INF).reshape(H // 2, 2, win, TILE)
        tables.append(vals.transpose(0, 2, 1, 3).reshape(H // 2, win, 2 * TILE))
    tables.append(jnp.full_like(tables[0], NEG_INF))
    return jnp.stack(tables, axis=0).astype(F32) * LOG2E


def _out_router_kernel(y_ref, w_ref, x_ref, g1_ref, n2_ref, sh_ref, sc_ref, rw_ref,
                       xo_ref, h_ref, aff_ref):
    D = x_ref.shape[2]
    x = x_ref[0] + g1_ref[0, 0] * _dot(y_ref[0], w_ref[...])
    xo_ref[0] = x
    h = _rms_mod(x, n2_ref[...], sh_ref[0, 0], sc_ref[0, 0])
    h_hi, h_lo = _split_bf16(h)
    r_hi, r_lo = _split_bf16(rw_ref[...])
    logits = _dot(h_hi, r_hi) + _dot(h_lo, r_hi) + _dot(h_hi, r_lo)
    lane = lax.broadcasted_iota(jnp.int32, logits.shape, 1)
    first = lane < N_EXPERTS
    m = jnp.max(jnp.where(first, logits, -jnp.inf), axis=-1, keepdims=True)
    e = jnp.exp(logits - m)
    aff = e / jnp.sum(jnp.where(first, e, 0.0), axis=-1, keepdims=True)
    aff_ref[0] = aff[:, :N_EXPERTS]
    hi = aff.astype(BF16)
    rem = aff - hi.astype(F32)
    mid = rem.astype(BF16)
    lo = (rem - mid.astype(F32)).astype(BF16)
    zero = jnp.zeros_like(hi)
    pieces = jnp.where(first, hi, jnp.where(lane < 2 * N_EXPERTS, mid,
                                            jnp.where(lane < 3 * N_EXPERTS, lo, zero)))
    h_ref[0, :, :D] = h_hi
    h_ref[0, :, D:] = pieces


def _out_router(y, w_out, xa, gate1, n2g, shift2, scale2, rw3):
    B, S, D = xa.shape
    nt = S // TILE
    mod_spec = pl.BlockSpec((1, 1, 1, D), lambda b, t: (b, t // (nt - 1), 0, 0))
    tile_spec = pl.BlockSpec((1, TILE, D), lambda b, t: (b, t, 0))
    return pl.pallas_call(
        _out_router_kernel,
        out_shape=(jax.ShapeDtypeStruct((B, S, D), F32),
                   jax.ShapeDtypeStruct((B, S, D + GATE_COLS), BF16),
                   jax.ShapeDtypeStruct((B, S, N_EXPERTS), F32)),
        grid=(B, nt),
        in_specs=[tile_spec,
                  pl.BlockSpec((D, D), lambda b, t: (0, 0)),
                  tile_spec, mod_spec,
                  pl.BlockSpec((1, D), lambda b, t: (0, 0)),
                  mod_spec, mod_spec,
                  pl.BlockSpec((D, GATE_COLS), lambda b, t: (0, 0))],
        out_specs=(tile_spec,
                   pl.BlockSpec((1, TILE, D + GATE_COLS), lambda b, t: (b, t, 0)),
                   pl.BlockSpec((1, TILE, N_EXPERTS), lambda b, t: (b, t, 0))),
        compiler_params=_params(2),
        name="out_router",
    )(y, w_out, xa, gate1, n2g, shift2, scale2, rw3)


def _route_kernel(aff_ref, slot_ref, start_ref, cnt_ref, *, T, cap_x, cap_c):
    S = aff_ref.shape[1]
    bits = lax.bitcast_convert_type(aff_ref[0], jnp.int32)
    ri = lax.broadcasted_iota(jnp.int32, (TILE, TILE), 0)
    ci = lax.broadcasted_iota(jnp.int32, (TILE, TILE), 1)
    ltri = (ri > ci).astype(BF16)

    def count(mask):
        return jnp.sum(mask.astype(F32), axis=0, keepdims=True)

    for lo_row, hi_row, cap, base in ((0, T, cap_x, 0), (T, S, cap_c, cap_x)):
        b = bits[lo_row:hi_row]

        def step(i, thr):
            cand = thr | lax.shift_left(jnp.int32(1), 30 - i)
            return jnp.where(count(b >= cand) >= cap, cand, thr)

        thr = lax.fori_loop(0, 31, step, jnp.zeros((1, N_EXPERTS), jnp.int32))
        need = cap - count(b > thr)
        seen_eq = jnp.zeros((1, N_EXPERTS), F32)
        seen = jnp.zeros((1, N_EXPERTS), F32)
        for j in range((hi_row - lo_row) // TILE):
            blk = b[j * TILE:(j + 1) * TILE]
            gt = blk > thr
            eq = blk == thr
            eq_rank = _dot(ltri, eq.astype(BF16)) + seen_eq
            sel = gt | (eq & (eq_rank < need))
            pos = _dot(ltri, sel.astype(BF16)) + seen + base
            t = lo_row // TILE + j
            slot_ref[0, t * TILE:(t + 1) * TILE, :] = jnp.where(sel, pos, -1.0)
            n_sel = count(sel)
            start_ref[0, t:t + 1, :] = seen + base
            cnt_ref[0, t:t + 1, :] = n_sel
            seen_eq = seen_eq + count(eq)
            seen = seen + n_sel


def _route(aff, T):
    B, S, E = aff.shape
    nt = S // TILE
    cap_x = CAPACITY_FACTOR * T // N_EXPERTS
    cap_c = CAPACITY_FACTOR * (S - T) // N_EXPERTS
    plan = jax.ShapeDtypeStruct((B, nt, E), F32)
    return pl.pallas_call(
        functools.partial(_route_kernel, T=T, cap_x=cap_x, cap_c=cap_c),
        out_shape=(jax.ShapeDtypeStruct((B, S, E), F32), plan, plan),
        grid=(B,),
        in_specs=[pl.BlockSpec((1, S, E), lambda b: (b, 0, 0))],
        out_specs=(pl.BlockSpec((1, S, E), lambda b: (b, 0, 0)),
                   pl.BlockSpec((1, nt, E), lambda b: (b, 0, 0)),
                   pl.BlockSpec((1, nt, E), lambda b: (b, 0, 0))),
        compiler_params=_params(1),
        name="route",
    )(aff)


def _slot_plan(start, cnt):
    start = start.astype(jnp.int32)
    cnt = cnt.astype(jnp.int32)
    first = (start // SLOT_ALIGN) * SLOT_ALIGN
    n_pass = jnp.max((start - first + cnt + SLOT_WIN - 1) // SLOT_WIN, axis=-1)
    return first, n_pass.astype(jnp.int32)


def _window_rel(slot, first, k, cap_total):
    nominal = first + k * SLOT_WIN
    begin = jnp.minimum(nominal, float(cap_total - SLOT_WIN))
    rel = slot - nominal
    return jnp.where((rel >= 0) & (rel < SLOT_WIN), rel + (nominal - begin), 255.0)


def _window_begin(first_s, k, cap_total):
    return pl.multiple_of(jnp.minimum(first_s + k * SLOT_WIN, cap_total - SLOT_WIN), SLOT_ALIGN)


def _dispatch_kernel(first_s, npass_s, slot_ref, first_ref, h_ref, o_ref):
    b, half, t = pl.program_id(0), pl.program_id(1), pl.program_id(2)
    nt = pl.num_programs(2)
    n_e = o_ref.shape[1]
    cap_total = o_ref.shape[2]

    @pl.when(t == 0)
    def _():
        o_ref[...] = jnp.zeros_like(o_ref)

    slot = slot_ref[0]
    first = first_ref[0, 0]
    pick_r = lax.broadcasted_iota(jnp.int32, (n_e, N_EXPERTS), 0)
    pick_c = lax.broadcasted_iota(jnp.int32, (n_e, N_EXPERTS), 1)
    pick = (pick_c == pick_r + half * n_e).astype(BF16)
    sub = lax.broadcasted_iota(jnp.int32, (SLOT_WIN, TILE), 0).astype(F32)

    def one_pass(k, carry):
        rel = _window_rel(slot, first, k.astype(F32), cap_total).astype(BF16)
        rel_t = _dot_nt(pick, rel)
        onehot = jnp.concatenate(
            [(jnp.broadcast_to(rel_t[e:e + 1, :], (SLOT_WIN, TILE)) == sub).astype(BF16)
             for e in range(n_e)], axis=0)
        rows = _dot(onehot, h_ref[0])
        for e in range(n_e):
            base = (b * nt + t) * N_EXPERTS + half * n_e + e
            win = pl.ds(_window_begin(first_s[base], k, cap_total), SLOT_WIN)
            cur = o_ref[0, e, win, :].astype(F32)
            o_ref[0, e, win, :] = (cur + rows[e * SLOT_WIN:(e + 1) * SLOT_WIN]).astype(BF16)
        return carry

    lax.fori_loop(0, npass_s[b * nt + t], one_pass, 0)


def _dispatch(hx, slot, first, n_pass, cap_total):
    B, S, W = hx.shape
    nt = S // TILE
    halves = 2
    n_e = N_EXPERTS // halves
    grid_spec = pltpu.PrefetchScalarGridSpec(
        num_scalar_prefetch=2,
        grid=(B, halves, nt),
        in_specs=[pl.BlockSpec((1, TILE, N_EXPERTS), lambda b, h, t, *_: (b, t, 0)),
                  pl.BlockSpec((1, 1, 1, N_EXPERTS), lambda b, h, t, *_: (b, t, 0, 0)),
                  pl.BlockSpec((1, TILE, W), lambda b, h, t, *_: (b, t, 0))],
        out_specs=pl.BlockSpec((1, n_e, cap_total, W), lambda b, h, t, *_: (b, h, 0, 0)),
    )
    return pl.pallas_call(
        _dispatch_kernel,
        out_shape=jax.ShapeDtypeStruct((B, N_EXPERTS, cap_total, W), BF16),
        grid_spec=grid_spec,
        compiler_params=_params(3),
        name="dispatch",
    )(first.reshape(-1), n_pass.reshape(-1), slot,
      first.astype(F32).reshape(B, nt, 1, N_EXPERTS), hx)


def _ffn_kernel(x_ref, wg_ref, wu_ref, wd_ref, o_ref):
    D = wg_ref.shape[1]
    ff = wg_ref.shape[2]
    x = x_ref[0, 0, :, :D]
    pieces = x_ref[0, 0, :, D:].astype(F32)
    lane = lax.broadcasted_iota(jnp.int32, pieces.shape, 1)
    mine = ((lane & (N_EXPERTS - 1)) == pl.program_id(0)) & (lane < 3 * N_EXPERTS)
    gate = jnp.sum(jnp.where(mine, pieces, 0.0), axis=-1, keepdims=True)
    chunk = 512
    acc = jnp.zeros((x.shape[0], D), F32)
    for j in range(ff // chunk):
        cols = slice(j * chunk, (j + 1) * chunk)
        g = _dot(x, wg_ref[0, :, cols])
        u = _dot(x, wu_ref[0, :, cols])
        hid = (g * jax.nn.sigmoid(g) * u).astype(BF16)
        acc = acc + _dot(hid, wd_ref[0, cols, :])
    o_ref[0, 0] = (acc * gate).astype(BF16)


def _ffn(xin, wg, wu, wd):
    B, E, S, W = xin.shape
    D, FF = wg.shape[1], wg.shape[2]
    return pl.pallas_call(
        _ffn_kernel,
        out_shape=jax.ShapeDtypeStruct((B, E, S, D), BF16),
        grid=(E, B),
        in_specs=[pl.BlockSpec((1, 1, S, W), lambda e, b: (b, e, 0, 0)),
                  pl.BlockSpec((1, D, FF), lambda e, b: (e, 0, 0)),
                  pl.BlockSpec((1, D, FF), lambda e, b: (e, 0, 0)),
                  pl.BlockSpec((1, FF, D), lambda e, b: (e, 0, 0))],
        out_specs=pl.BlockSpec((1, 1, S, D), lambda e, b: (b, e, 0, 0)),
        compiler_params=_params(2),
        name="expert_ffn",
    )(xin, wg, wu, wd)


def _combine_kernel(first_s, npass_s, slot_ref, first_ref, expand_ref, y_ref, x_ref, g2_ref,
                    *rest, n_tiles):
    o_ref = rest[-1]
    b, t = pl.program_id(0), pl.program_id(1)
    cap_total = y_ref.shape[2]
    o_ref[0] = x_ref[0]
    slot = slot_ref[0]
    first = first_ref[0, 0]
    lane_row = (lax.broadcasted_iota(jnp.int32, (TILE, N_EXPERTS * SLOT_WIN), 1)
                & (SLOT_WIN - 1)).astype(F32)

    def one_pass(k, carry):
        rel = _window_rel(slot, first, k.astype(F32), cap_total).astype(BF16)
        onehot = (_dot(rel, expand_ref[...]) == lane_row).astype(BF16)
        rows = jnp.concatenate(
            [y_ref[0, e, pl.ds(_window_begin(first_s[(b * n_tiles + t) * N_EXPERTS + e], k,
                                             cap_total), SLOT_WIN), :]
             for e in range(N_EXPERTS)], axis=0)
        o_ref[0] += g2_ref[0, 0] * _dot(onehot, rows)
        return carry

    lax.fori_loop(0, npass_s[b * n_tiles + t], one_pass, 0)
    if len(rest) == 2:
        x = o_ref[0]
        ms = jnp.mean(x * x, axis=-1, keepdims=True)
        o_ref[0] = x * lax.rsqrt(ms + RMS_EPS) * rest[0][...]


def _combine(y, slot, first, n_pass, xa, gate2, final_g=None):
    B, S, D = xa.shape
    nt = S // TILE
    E, cap_total = y.shape[1], y.shape[2]
    expand = np.repeat(np.eye(E, dtype=np.float32), SLOT_WIN, axis=1)
    in_specs = [pl.BlockSpec((1, TILE, E), lambda b, t, *_: (b, t, 0)),
                pl.BlockSpec((1, 1, 1, E), lambda b, t, *_: (b, t, 0, 0)),
                pl.BlockSpec((E, E * SLOT_WIN), lambda b, t, *_: (0, 0)),
                pl.BlockSpec((1, E, cap_total, D), lambda b, t, *_: (b, 0, 0, 0)),
                pl.BlockSpec((1, TILE, D), lambda b, t, *_: (b, t, 0)),
                pl.BlockSpec((1, 1, 1, D), lambda b, t, *_: (b, t // (nt - 1), 0, 0))]
    args = [slot, first.astype(F32).reshape(B, nt, 1, E), jnp.asarray(expand, BF16), y, xa, gate2]
    out_tiles = nt
    if final_g is not None:
        in_specs.append(pl.BlockSpec((1, D), lambda b, t, *_: (0, 0)))
        args.append(final_g)
        out_tiles = nt - 1
    grid_spec = pltpu.PrefetchScalarGridSpec(
        num_scalar_prefetch=2,
        grid=(B, out_tiles),
        in_specs=in_specs,
        out_specs=pl.BlockSpec((1, TILE, D), lambda b, t, *_: (b, t, 0)),
    )
    return pl.pallas_call(
        functools.partial(_combine_kernel, n_tiles=nt),
        out_shape=jax.ShapeDtypeStruct((B, out_tiles * TILE, D), F32),
        grid_spec=grid_spec,
        compiler_params=_params(2),
        name="combine",
    )(first.reshape(-1), n_pass.reshape(-1), *args)


def _scale_q(w, nq):
    return jnp.concatenate([w[:, :nq] * (HEAD_DIM ** -0.5 * LOG2E), w[:, nq:]], axis=1)


def _win_weights(w_in):
    D = w_in.shape[0]
    nq = 16 * HEAD_DIM
    nk = A_KV_HEADS * HEAD_DIM
    w = _scale_q(w_in, nq)
    dup = lambda m: jnp.concatenate([m.reshape(D, A_KV_HEADS, 1, HEAD_DIM)] * 2,
                                    axis=2).reshape(D, 2 * nk)
    return jnp.concatenate([w[:, :nq], dup(w[:, nq:nq + nk]), dup(w[:, nq + nk:])],
                           axis=1).astype(BF16)


def _router_weights(rw):
    D, E = rw.shape
    return jnp.concatenate([rw, rw, rw, jnp.zeros((D, GATE_COLS - 3 * E), rw.dtype)], axis=1)


def kernel(x, c, ctx, c_ctx, ada_w, ada_b, norm1_g, norm2_g, final_g, win_w_in, win_w_out,
           win_sink, diff_w_in, diff_w_out, diff_lambda, diff_subln_g, na_w_in, na_w_out,
           na_rpb, router_w, w_gate, w_up, w_down):
    B, T, D = x.shape
    L = ctx.shape[1]
    S = T + L
    depth = ada_w.shape[0]
    assert L == TILE and T % TILE == 0 and (T // GRID_W) >= NA_WIN_ROWS
    xa = jnp.concatenate([x, ctx], axis=1)
    cs = jnp.concatenate([c, c_ctx[None, :]], axis=0)
    rope = _rope_tables(T, L)
    for i in range(depth):
        kind = i % N_MIXERS
        slot = i // N_MIXERS
        mod = _ada(cs, ada_w[i], ada_b[i][None, :])
        mod = jnp.stack([mod[:B], jnp.broadcast_to(mod[B:], (B, 6 * D))], axis=1)
        mod = mod.reshape(B, 2, 1, 6, D)
        sh1, sc1, g1, sh2, sc2, g2 = [mod[:, :, :, k, :] for k in range(6)]
        if kind == 0:
            w_in = _win_weights(win_w_in[slot])
            qkv = _norm_proj(xa, norm1_g[i][None, :], sh1, sc1, w_in, rope,
                             (16 + 2 * A_KV_HEADS) * HEAD_DIM)
            y = _attn_win(qkv, win_sink[slot], T)
            w_out = win_w_out[slot]
        elif kind == 1:
            lambda_init = 0.8 - 0.6 * math.exp(-0.3 * i)
            w_in = _scale_q(diff_w_in[slot], 16 * HEAD_DIM).astype(BF16)
            qkv = _norm_proj(xa, norm1_g[i][None, :], sh1, sc1, w_in, rope, 32 * HEAD_DIM)
            lp = diff_lambda[slot]
            lam = (jnp.exp(jnp.sum(lp[0] * lp[1])) - jnp.exp(jnp.sum(lp[2] * lp[3]))
                   + lambda_init).reshape(1)
            y = _attn_diff(qkv, lam, diff_subln_g[slot][None, :], T, 1.0 - lambda_init)
            w_out = diff_w_out[slot]
        else:
            w_in = _scale_q(na_w_in[slot], 16 * HEAD_DIM).astype(BF16)
            qkv = _norm_proj(xa, norm1_g[i][None, :], sh1, sc1, w_in, rope, 0)
            y = _attn_na(qkv, _na_bias_tables(na_rpb[slot], T), T)
            w_out = na_w_out[slot]
        xa, hx, aff = _out_router(y, w_out.astype(BF16), xa, g1, norm2_g[i][None, :],
                                  sh2, sc2, _router_weights(router_w[i]))
        tok_slot, start, cnt = _route(aff, T)
        first, n_pass = _slot_plan(start, cnt)
        cap_total = CAPACITY_FACTOR * S // N_EXPERTS
        xin = _dispatch(hx, tok_slot, first, n_pass, cap_total)
        ye = _ffn(xin, w_gate[i].astype(BF16), w_up[i].astype(BF16), w_down[i].astype(BF16))
        last = i == depth - 1
        xa = _combine(ye, tok_slot, first, n_pass, xa, g2, final_g[None, :] if last else None)
    return xa
```

```python
import functools
import math

import numpy as np
import jax
import jax.numpy as jnp
from jax import lax
from jax.experimental import pallas as pl
from jax.experimental.pallas import tpu as pltpu

HEAD_DIM = 64
LANES = 128
GRID_W = 64
NA_ROWS = 8
NA_COLS = 16
A_WINDOW = 128
A_KV_HEADS = 4
N_MIXERS = 3
ROPE_BASE = 10000.0
ROPE_AXIS_DIM = HEAD_DIM // 2
N_EXPERTS = 16
CAPACITY_FACTOR = 2
RMS_EPS = 1e-6
NEG_INF = -1e30
LOG2E = math.log2(math.e)
TILE = 256
NA_TILE_ROWS = TILE // GRID_W
NA_WIN_ROWS = NA_TILE_ROWS + NA_ROWS
VMEM_LIMIT = 56 * 1024 * 1024
SLOT_WIN = 64
SLOT_ALIGN = 16
GATE_COLS = LANES

BF16 = jnp.bfloat16
F32 = jnp.float32


def _params(n_grid):
    return pltpu.CompilerParams(
        dimension_semantics=("arbitrary",) * n_grid, vmem_limit_bytes=VMEM_LIMIT)


def _split_bf16(a):
    hi = a.astype(BF16)
    lo = (a - hi.astype(F32)).astype(BF16)
    return hi, lo


def _dot(a, b):
    return jnp.dot(a, b, preferred_element_type=F32)


def _dot_nt(a, b):
    return lax.dot_general(a, b, (((1,), (1,)), ((), ())), preferred_element_type=F32)


def _dot_tn(a, b):
    return lax.dot_general(a, b, (((0,), (0,)), ((), ())), preferred_element_type=F32)


def _ada_kernel(c_ref, w_ref, b_ref, o_ref):
    c = c_ref[...]
    a = c * jax.nn.sigmoid(c)
    a_hi, a_lo = _split_bf16(a)
    w_hi, w_lo = _split_bf16(w_ref[...])
    o_ref[...] = _dot(a_hi, w_hi) + _dot(a_lo, w_hi) + _dot(a_hi, w_lo) + b_ref[...]


def _ada(cs, w, b):
    R, D = cs.shape
    N = w.shape[1]
    tn = 1024
    return pl.pallas_call(
        _ada_kernel,
        out_shape=jax.ShapeDtypeStruct((R, N), F32),
        grid=(N // tn,),
        in_specs=[pl.BlockSpec((R, D), lambda j: (0, 0)),
                  pl.BlockSpec((D, tn), lambda j: (0, j)),
                  pl.BlockSpec((1, tn), lambda j: (0, j))],
        out_specs=pl.BlockSpec((R, tn), lambda j: (0, j)),
        compiler_params=_params(1),
        name="ada",
    )(cs, w, b)


def _rms_mod(x, g, shift, scale):
    ms = jnp.mean(x * x, axis=-1, keepdims=True)
    y = x * lax.rsqrt(ms + RMS_EPS) * g
    return y * (1.0 + scale) + shift


def _norm_proj_kernel(*refs, n_rope, joins_streams):
    if joins_streams:
        x_ref, c_ref, g_ref, sh_ref, sc_ref, w_ref, cos_ref, sa_ref, sb_ref, o_ref, xa_ref = refs
        is_ctx = pl.program_id(1) == pl.num_programs(1) - 1
        x = jnp.where(is_ctx, c_ref[0], x_ref[0])
        xa_ref[0] = x
    else:
        x_ref, g_ref, sh_ref, sc_ref, w_ref, cos_ref, sa_ref, sb_ref, o_ref = refs
        x = x_ref[0]
    h = _rms_mod(x, g_ref[...], sh_ref[0, 0], sc_ref[0, 0]).astype(BF16)
    n_cols = w_ref.shape[1]
    chunk = 512
    for j in range(n_cols // chunk):
        acc = _dot(h, w_ref[:, j * chunk:(j + 1) * chunk])
        for t in range(chunk // LANES):
            col = j * chunk + t * LANES
            a = acc[:, t * LANES:(t + 1) * LANES]
            if col < n_rope:
                a = (a * cos_ref[...]
                     + pltpu.roll(a, LANES - 16, 1) * sa_ref[...]
                     + pltpu.roll(a, 16, 1) * sb_ref[...])
            o_ref[0, :, col:col + LANES] = a.astype(BF16)


def _norm_proj(streams, g, shift, scale, w, rope, n_rope):
    joins = len(streams) == 2
    B, _, D = streams[0].shape
    S = sum(a.shape[1] for a in streams)
    N = w.shape[1]
    nt = S // TILE
    mod_spec = pl.BlockSpec((1, 1, 1, D), lambda b, t: (b, t // (nt - 1), 0, 0))
    rope_spec = pl.BlockSpec((TILE, LANES), lambda b, t: (t, 0))
    tile_spec = pl.BlockSpec((1, TILE, D), lambda b, t: (b, t, 0))
    if joins:
        stream_specs = [pl.BlockSpec((1, TILE, D), lambda b, t: (b, jnp.minimum(t, nt - 2), 0)),
                        pl.BlockSpec((1, TILE, D), lambda b, t: (b, 0, 0))]
    else:
        stream_specs = [tile_spec]
    qkv_shape = jax.ShapeDtypeStruct((B, S, N), BF16)
    qkv_spec = pl.BlockSpec((1, TILE, N), lambda b, t: (b, t, 0))
    return pl.pallas_call(
        functools.partial(_norm_proj_kernel, n_rope=n_rope, joins_streams=joins),
        out_shape=(qkv_shape, jax.ShapeDtypeStruct((B, S, D), F32)) if joins else qkv_shape,
        grid=(B, nt),
        in_specs=stream_specs + [pl.BlockSpec((1, D), lambda b, t: (0, 0)),
                                 mod_spec, mod_spec,
                                 pl.BlockSpec((D, N), lambda b, t: (0, 0)),
                                 rope_spec, rope_spec, rope_spec],
        out_specs=(qkv_spec, tile_spec) if joins else qkv_spec,
        compiler_params=_params(2),
        name="norm_proj",
    )(*streams, g, shift, scale, w, *rope)


def _rope_tables(T, L):
    t = np.arange(T)
    pos = np.stack([t // GRID_W, t % GRID_W], axis=0).astype(np.float32)
    inv = (1.0 / (ROPE_BASE ** (np.arange(0, ROPE_AXIS_DIM, 2, dtype=np.float32)
                                / ROPE_AXIS_DIM))).astype(np.float32)
    d = np.arange(LANES) % HEAD_DIM
    axis = d // ROPE_AXIS_DIM
    half = (d % ROPE_AXIS_DIM) // (ROPE_AXIS_DIM // 2)
    freq = d % (ROPE_AXIS_DIM // 2)
    ang = jnp.asarray(pos[axis].T) * jnp.asarray(inv[freq])[None, :]
    cos = jnp.cos(ang)
    sin = jnp.sin(ang)
    first = jnp.asarray(half == 0)[None, :]
    sa = jnp.where(first, -sin, 0.0)
    sb = jnp.where(first, 0.0, sin)
    pad = lambda a, v: jnp.concatenate([a, jnp.full((L, LANES), v, F32)], axis=0)
    return pad(cos, 1.0), pad(sa, 0.0), pad(sb, 0.0)


def _half_masks(q):
    lane = lax.broadcasted_iota(jnp.int32, q.shape, 1)
    zero = jnp.zeros_like(q)
    return jnp.where(lane < HEAD_DIM, q, zero), jnp.where(lane >= HEAD_DIM, q, zero)


def _merge_halves(o_first, o_second):
    lane = lax.broadcasted_iota(jnp.int32, o_first.shape, 1)
    return jnp.where(lane < HEAD_DIM, o_first, o_second)


WIN_KV_PER_STEP = 4


def _win_kernel(sink_ref, q_ref, k_ref, v_ref, o_ref, *, T, win):
    hb = pl.program_id(1)
    t = pl.program_id(2)
    n_x = T // TILE
    is_x = t < n_x
    start = pl.multiple_of(jnp.clip(t * TILE - A_WINDOW, 0, T - win), LANES)
    kpos = start + lax.broadcasted_iota(jnp.int32, (win, TILE), 0)
    qpos = t * TILE + lax.broadcasted_iota(jnp.int32, (win, TILE), 1)
    band = (jnp.abs(qpos - kpos) <= A_WINDOW) & is_x
    band = jnp.concatenate([band] * 4, axis=1)
    scores = []
    for j in range(WIN_KV_PER_STEP):
        q = q_ref[0, :, 2 * j * LANES:2 * (j + 1) * LANES]
        qa, qb = _half_masks(q[:, :LANES])
        qc, qd = _half_masks(q[:, LANES:])
        qs = jnp.concatenate([qa, qb, qc, qd], axis=0)
        cols = slice(j * LANES, (j + 1) * LANES)
        scores.append((_dot_nt(k_ref[0, T:, cols], qs),
                       _dot_nt(k_ref[0, pl.ds(start, win), cols], qs)))
    for j, (s_c, s_w) in enumerate(scores):
        cols = slice(j * LANES, (j + 1) * LANES)
        head0 = (hb * WIN_KV_PER_STEP + j) * 4
        s_w = jnp.where(band, s_w, NEG_INF)
        sink = jnp.concatenate(
            [jnp.full((1, TILE), sink_ref[head0 + g] * LOG2E, F32) for g in range(4)], axis=1)
        m = jnp.maximum(jnp.maximum(jnp.max(s_c, axis=0, keepdims=True),
                                    jnp.max(s_w, axis=0, keepdims=True)), sink)
        ec = jnp.exp2(s_c - m)
        ew = jnp.exp2(s_w - m)
        r = 1.0 / (jnp.sum(ec, axis=0, keepdims=True) + jnp.sum(ew, axis=0, keepdims=True)
                   + jnp.exp2(sink - m))
        o = ((_dot_tn(v_ref[0, T:, cols], ec.astype(BF16))
              + _dot_tn(v_ref[0, pl.ds(start, win), cols], ew.astype(BF16))) * r).T
        first = 2 * j * LANES
        o_ref[0, :, first:first + LANES] = _merge_halves(o[:TILE], o[TILE:2 * TILE]).astype(BF16)
        o_ref[0, :, first + LANES:first + 2 * LANES] = _merge_halves(
            o[2 * TILE:3 * TILE], o[3 * TILE:]).astype(BF16)


def _attn_win(qkv, sink, T):
    B, S, _ = qkv.shape
    nt = S // TILE
    nq = 16 * HEAD_DIM
    win = TILE + 2 * A_WINDOW
    hb = A_KV_HEADS // WIN_KV_PER_STEP
    wq = WIN_KV_PER_STEP * 2 * LANES
    wk = WIN_KV_PER_STEP * LANES
    kb = nq // wk
    return pl.pallas_call(
        functools.partial(_win_kernel, T=T, win=win),
        out_shape=jax.ShapeDtypeStruct((B, S, nq), BF16),
        grid=(B, hb, nt),
        in_specs=[pl.BlockSpec(memory_space=pltpu.SMEM),
                  pl.BlockSpec((1, TILE, wq), lambda b, h, t: (b, t, h)),
                  pl.BlockSpec((1, S, wk), lambda b, h, t: (b, 0, kb + h)),
                  pl.BlockSpec((1, S, wk), lambda b, h, t: (b, 0, kb + hb + h))],
        out_specs=pl.BlockSpec((1, TILE, wq), lambda b, h, t: (b, t, h)),
        compiler_params=_params(3),
        name="attn_win",
    )(sink, qkv, qkv, qkv)


DIFF_HEADS_PER_STEP = 2


def _diff_kernel(lam_ref, q_ref, k_ref, v_ref, g_ref, o_ref, *, T, out_scale):
    t = pl.program_id(2)
    n_x = T // TILE
    lam = lam_ref[0]
    heads = [slice(j * LANES, (j + 1) * LANES) for j in range(DIFF_HEADS_PER_STEP)]

    def attend(keys):
        scores = []
        for cols in heads:
            qa, qb = _half_masks(q_ref[0, :, cols])
            qs = jnp.concatenate([qa, qb], axis=0)
            scores.append(_dot_nt(qs, k_ref[0, keys, cols]))
        for cols, s in zip(heads, scores):
            m = jnp.max(s, axis=-1, keepdims=True)
            e = jnp.exp2(s - m)
            l = jnp.sum(e, axis=-1, keepdims=True)
            a = e[:TILE] - e[TILE:] * (lam * l[:TILE] / l[TILE:])
            o = _dot(a.astype(BF16), v_ref[0, keys, cols]) / l[:TILE]
            ms = jnp.mean(o * o, axis=-1, keepdims=True)
            y = o * lax.rsqrt(ms + RMS_EPS) * g_ref[...] * out_scale
            o_ref[0, :, cols] = y.astype(BF16)

    @pl.when(t < n_x)
    def _():
        attend(slice(None))

    @pl.when(t >= n_x)
    def _():
        attend(slice(T, None))


def _attn_diff(qkv, lam, subln_g, T, out_scale):
    B, S, _ = qkv.shape
    nt = S // TILE
    H = 8
    hb = H // DIFF_HEADS_PER_STEP
    w = DIFF_HEADS_PER_STEP * LANES
    return pl.pallas_call(
        functools.partial(_diff_kernel, T=T, out_scale=out_scale),
        out_shape=jax.ShapeDtypeStruct((B, S, H * LANES), BF16),
        grid=(B, hb, nt),
        in_specs=[pl.BlockSpec(memory_space=pltpu.SMEM),
                  pl.BlockSpec((1, TILE, w), lambda b, h, t: (b, t, h)),
                  pl.BlockSpec((1, S, w), lambda b, h, t: (b, 0, hb + h)),
                  pl.BlockSpec((1, S, w), lambda b, h, t: (b, 0, 2 * hb + h)),
                  pl.BlockSpec((1, LANES), lambda b, h, t: (0, 0))],
        out_specs=pl.BlockSpec((1, TILE, w), lambda b, h, t: (b, t, h)),
        compiler_params=_params(3),
        name="attn_diff",
    )(lam, qkv, qkv, qkv, subln_g)


NA_PAIRS_PER_STEP = 2


def _na_kernel(q_ref, k_ref, v_ref, bias_ref, o_ref, *, T):
    t = pl.program_id(2)
    n_x = T // TILE
    rows = T // GRID_W
    win = NA_WIN_ROWS * GRID_W
    cls = jnp.where(t >= n_x, 3, jnp.where(t == 0, 0, jnp.where(t == n_x - 1, 2, 1)))
    row0 = jnp.clip(t * NA_TILE_ROWS - NA_ROWS // 2, 0, rows - NA_WIN_ROWS)
    start = pl.multiple_of(row0 * GRID_W, GRID_W)
    scores = []
    for j in range(NA_PAIRS_PER_STEP):
        cols = slice(j * LANES, (j + 1) * LANES)
        qa, qb = _half_masks(q_ref[0, :, cols])
        qs = jnp.concatenate([qa, qb], axis=0)
        scores.append((_dot_nt(k_ref[0, T:, cols], qs),
                       _dot_nt(k_ref[0, pl.ds(start, win), cols], qs)))
    for j, (s_c, s_w) in enumerate(scores):
        cols = slice(j * LANES, (j + 1) * LANES)
        s_w = s_w + bias_ref[cls, j]
        m = jnp.maximum(jnp.max(s_c, axis=0, keepdims=True), jnp.max(s_w, axis=0, keepdims=True))
        ec = jnp.exp2(s_c - m)
        ew = jnp.exp2(s_w - m)
        r = 1.0 / (jnp.sum(ec, axis=0, keepdims=True) + jnp.sum(ew, axis=0, keepdims=True))
        o = ((_dot_tn(v_ref[0, T:, cols], ec.astype(BF16))
              + _dot_tn(v_ref[0, pl.ds(start, win), cols], ew.astype(BF16))) * r).T
        o_ref[0, :, cols] = _merge_halves(o[:TILE], o[TILE:]).astype(BF16)


def _attn_na(qkv, bias, T):
    B, S, _ = qkv.shape
    nt = S // TILE
    n = NA_PAIRS_PER_STEP
    hb = 8 // n
    w = n * LANES
    win = NA_WIN_ROWS * GRID_W
    return pl.pallas_call(
        functools.partial(_na_kernel, T=T),
        out_shape=jax.ShapeDtypeStruct((B, S, 8 * LANES), BF16),
        grid=(hb, B, nt),
        in_specs=[pl.BlockSpec((1, TILE, w), lambda h, b, t: (b, t, h)),
                  pl.BlockSpec((1, S, w), lambda h, b, t: (b, 0, hb + h)),
                  pl.BlockSpec((1, S, w), lambda h, b, t: (b, 0, 2 * hb + h)),
                  pl.BlockSpec((4, n, win, 2 * TILE), lambda h, b, t: (0, h, 0, 0))],
        out_specs=pl.BlockSpec((1, TILE, w), lambda h, b, t: (b, t, h)),
        compiler_params=_params(3),
        name="attn_na",
    )(qkv, qkv, qkv, bias)


def _na_bias_tables(rpb, T):
    rows = T // GRID_W
    H = rpb.shape[0]
    win = NA_WIN_ROWS * GRID_W
    hi = lax.Precision.HIGHEST
    pick = lambda idx, n: jnp.asarray(idx[..., None] == np.arange(n), F32)
    c = np.arange(GRID_W)
    cs = np.clip(c - NA_COLS // 2, 0, GRID_W - NA_COLS)
    valid_c = (c[None, :] >= cs[:, None]) & (c[None, :] < cs[:, None] + NA_COLS)
    bidx_c = np.clip(c[None, :] - c[:, None] + NA_COLS - 1, 0, 2 * NA_COLS - 2)
    by_col = jnp.einsum('hrd,ckd->hrck', rpb, pick(bidx_c, 2 * NA_COLS - 1), precision=hi)
    i = np.arange(NA_TILE_ROWS)
    j = np.arange(NA_WIN_ROWS)
    tables = []
    for r0 in (0, NA_TILE_ROWS, rows - NA_TILE_ROWS):
        s = int(np.clip(r0 - NA_ROWS // 2, 0, rows - NA_WIN_ROWS))
        r = r0 + i
        rs = np.clip(r - NA_ROWS // 2, 0, rows - NA_ROWS)
        kr = s + j
        valid_r = (kr[None, :] >= rs[:, None]) & (kr[None, :] < rs[:, None] + NA_ROWS)
        bidx_r = np.clip(kr[None, :] - r[:, None] + NA_ROWS - 1, 0, 2 * NA_ROWS - 2)
        vals = jnp.einsum('hrck,ijr->hjkic', by_col, pick(bidx_r, 2 * NA_ROWS - 1), precision=hi)
        valid = valid_r.T[:, None, :, None] & valid_c.T[None, :, None, :]
        vals = jnp.where(jnp.asarray(valid)[None], vals, NEG_INF).reshape(H // 2, 2, win, TILE)
        tables.append(vals.transpose(0, 2, 1, 3).reshape(H // 2, win, 2 * TILE))
    tables.append(jnp.full_like(tables[0], NEG_INF))
    return jnp.stack(tables, axis=0).astype(F32) * LOG2E


def _out_router_kernel(y_ref, w_ref, x_ref, g1_ref, n2_ref, sh_ref, sc_ref, rw_ref,
                       xo_ref, h_ref, aff_ref):
    D = x_ref.shape[2]
    x = x_ref[0] + g1_ref[0, 0] * _dot(y_ref[0], w_ref[...])
    xo_ref[0] = x
    h = _rms_mod(x, n2_ref[...], sh_ref[0, 0], sc_ref[0, 0])
    h_hi, h_lo = _split_bf16(h)
    r_hi, r_lo = _split_bf16(rw_ref[...])
    logits = _dot(h_hi, r_hi) + _dot(h_lo, r_hi) + _dot(h_hi, r_lo)
    lane = lax.broadcasted_iota(jnp.int32, logits.shape, 1)
    first = lane < N_EXPERTS
    m = jnp.max(jnp.where(first, logits, -jnp.inf), axis=-1, keepdims=True)
    e = jnp.exp(logits - m)
    aff = e / jnp.sum(jnp.where(first, e, 0.0), axis=-1, keepdims=True)
    aff_ref[0] = aff
    hi = aff.astype(BF16)
    rem = aff - hi.astype(F32)
    mid = rem.astype(BF16)
    lo = (rem - mid.astype(F32)).astype(BF16)
    zero = jnp.zeros_like(hi)
    pieces = jnp.where(first, hi, jnp.where(lane < 2 * N_EXPERTS, mid,
                                            jnp.where(lane < 3 * N_EXPERTS, lo, zero)))
    h_ref[0, :, :D] = h_hi
    h_ref[0, :, D:] = pieces


def _out_router(y, w_out, xa, gate1, n2g, shift2, scale2, rw3):
    B, S, D = xa.shape
    nt = S // TILE
    mod_spec = pl.BlockSpec((1, 1, 1, D), lambda b, t: (b, t // (nt - 1), 0, 0))
    tile_spec = pl.BlockSpec((1, TILE, D), lambda b, t: (b, t, 0))
    return pl.pallas_call(
        _out_router_kernel,
        out_shape=(jax.ShapeDtypeStruct((B, S, D), F32),
                   jax.ShapeDtypeStruct((B, S, D + GATE_COLS), BF16),
                   jax.ShapeDtypeStruct((B, S, GATE_COLS), F32)),
        grid=(B, nt),
        in_specs=[tile_spec,
                  pl.BlockSpec((D, D), lambda b, t: (0, 0)),
                  tile_spec, mod_spec,
                  pl.BlockSpec((1, D), lambda b, t: (0, 0)),
                  mod_spec, mod_spec,
                  pl.BlockSpec((D, GATE_COLS), lambda b, t: (0, 0))],
        out_specs=(tile_spec,
                   pl.BlockSpec((1, TILE, D + GATE_COLS), lambda b, t: (b, t, 0)),
                   pl.BlockSpec((1, TILE, GATE_COLS), lambda b, t: (b, t, 0))),
        compiler_params=_params(2),
        name="out_router",
    )(y, w_out, xa, gate1, n2g, shift2, scale2, rw3)


def _route_kernel(aff_ref, slot_ref, start_ref, cnt_ref, *, T, cap_x, cap_c):
    S = aff_ref.shape[1]
    aff = aff_ref[0]
    bits = lax.bitcast_convert_type(aff[:, :N_EXPERTS], jnp.int32)
    ri = lax.broadcasted_iota(jnp.int32, (TILE, TILE), 0)
    ci = lax.broadcasted_iota(jnp.int32, (TILE, TILE), 1)
    ltri = (ri > ci).astype(BF16)
    eye = (lax.broadcasted_iota(jnp.int32, (N_EXPERTS, N_EXPERTS), 0)
           == lax.broadcasted_iota(jnp.int32, (N_EXPERTS, N_EXPERTS), 1))

    def count(mask):
        return jnp.sum(mask.astype(F32), axis=0, keepdims=True)

    for lo_row, hi_row, cap, base in ((0, T, cap_x, 0), (T, S, cap_c, cap_x)):
        b = bits[lo_row:hi_row]
        dense = lax.bitcast_convert_type(jnp.concatenate(
            [aff[r:r + TILE].T[:N_EXPERTS] for r in range(lo_row, hi_row, TILE)], axis=1),
            jnp.int32)

        def step(i, thr):
            cand = thr | lax.shift_left(jnp.int32(1), 30 - i)
            n_ge = jnp.sum((dense >= cand).astype(F32), axis=1, keepdims=True)
            return jnp.where(n_ge >= cap, cand, thr)

        thr = lax.fori_loop(0, 31, step, jnp.zeros((N_EXPERTS, 1), jnp.int32))
        thr = jnp.max(jnp.where(eye, jnp.broadcast_to(thr, eye.shape), 0), axis=0,
                      keepdims=True)
        need = cap - count(b > thr)
        seen_eq = jnp.zeros((1, N_EXPERTS), F32)
        seen = jnp.zeros((1, N_EXPERTS), F32)
        for j in range((hi_row - lo_row) // TILE):
            blk = b[j * TILE:(j + 1) * TILE]
            gt = blk > thr
            eq = blk == thr
            eq_rank = _dot(ltri, eq.astype(BF16)) + seen_eq
            sel = gt | (eq & (eq_rank < need))
            pos = _dot(ltri, sel.astype(BF16)) + seen + base
            t = lo_row // TILE + j
            slot_ref[0, t * TILE:(t + 1) * TILE, :] = jnp.where(sel, pos, -1.0)
            n_sel = count(sel)
            start_ref[0, t:t + 1, :] = seen + base
            cnt_ref[0, t:t + 1, :] = n_sel
            seen_eq = seen_eq + count(eq)
            seen = seen + n_sel


def _route(aff, T):
    B, S, _ = aff.shape
    E = N_EXPERTS
    nt = S // TILE
    cap_x = CAPACITY_FACTOR * T // N_EXPERTS
    cap_c = CAPACITY_FACTOR * (S - T) // N_EXPERTS
    plan = jax.ShapeDtypeStruct((B, nt, E), F32)
    return pl.pallas_call(
        functools.partial(_route_kernel, T=T, cap_x=cap_x, cap_c=cap_c),
        out_shape=(jax.ShapeDtypeStruct((B, S, E), F32), plan, plan),
        grid=(B,),
        in_specs=[pl.BlockSpec((1, S, GATE_COLS), lambda b: (b, 0, 0))],
        out_specs=(pl.BlockSpec((1, S, E), lambda b: (b, 0, 0)),
                   pl.BlockSpec((1, nt, E), lambda b: (b, 0, 0)),
                   pl.BlockSpec((1, nt, E), lambda b: (b, 0, 0))),
        compiler_params=_params(1),
        name="route",
    )(aff)


def _slot_plan(start, cnt):
    start = start.astype(jnp.int32)
    cnt = cnt.astype(jnp.int32)
    first = (start // SLOT_ALIGN) * SLOT_ALIGN
    n_pass = jnp.max((start - first + cnt + SLOT_WIN - 1) // SLOT_WIN, axis=-1)
    return first, n_pass.astype(jnp.int32)


def _window_rel(slot, first, k, cap_total):
    nominal = first + k * SLOT_WIN
    begin = jnp.minimum(nominal, float(cap_total - SLOT_WIN))
    rel = slot - nominal
    return jnp.where((rel >= 0) & (rel < SLOT_WIN), rel + (nominal - begin), 255.0)


def _window_begin(first_s, k, cap_total):
    return pl.multiple_of(jnp.minimum(first_s + k * SLOT_WIN, cap_total - SLOT_WIN), SLOT_ALIGN)


def _dispatch_kernel(first_s, npass_s, slot_ref, first_ref, h_ref, o_ref):
    b, half, t = pl.program_id(0), pl.program_id(1), pl.program_id(2)
    nt = pl.num_programs(2)
    n_e = o_ref.shape[1]
    cap_total = o_ref.shape[2]

    @pl.when(t == 0)
    def _():
        o_ref[...] = jnp.zeros_like(o_ref)

    slot = slot_ref[0]
    first = first_ref[0, 0]
    pick_r = lax.broadcasted_iota(jnp.int32, (n_e, N_EXPERTS), 0)
    pick_c = lax.broadcasted_iota(jnp.int32, (n_e, N_EXPERTS), 1)
    pick = (pick_c == pick_r + half * n_e).astype(BF16)
    sub = lax.broadcasted_iota(jnp.int32, (SLOT_WIN, TILE), 0).astype(F32)

    def one_pass(k, carry):
        rel = _window_rel(slot, first, k.astype(F32), cap_total).astype(BF16)
        rel_t = _dot_nt(pick, rel)
        onehot = jnp.concatenate(
            [(jnp.broadcast_to(rel_t[e:e + 1, :], (SLOT_WIN, TILE)) == sub).astype(BF16)
             for e in range(n_e)], axis=0)
        rows = _dot(onehot, h_ref[0])
        for e in range(n_e):
            base = (b * nt + t) * N_EXPERTS + half * n_e + e
            win = pl.ds(_window_begin(first_s[base], k, cap_total), SLOT_WIN)
            cur = o_ref[0, e, win, :].astype(F32)
            o_ref[0, e, win, :] = (cur + rows[e * SLOT_WIN:(e + 1) * SLOT_WIN]).astype(BF16)
        return carry

    lax.fori_loop(0, npass_s[b * nt + t], one_pass, 0)


def _dispatch(hx, slot, first, n_pass, cap_total):
    B, S, W = hx.shape
    nt = S // TILE
    halves = 1
    n_e = N_EXPERTS // halves
    grid_spec = pltpu.PrefetchScalarGridSpec(
        num_scalar_prefetch=2,
        grid=(B, halves, nt),
        in_specs=[pl.BlockSpec((1, TILE, N_EXPERTS), lambda b, h, t, *_: (b, t, 0)),
                  pl.BlockSpec((1, 1, 1, N_EXPERTS), lambda b, h, t, *_: (b, t, 0, 0)),
                  pl.BlockSpec((1, TILE, W), lambda b, h, t, *_: (b, t, 0))],
        out_specs=pl.BlockSpec((1, n_e, cap_total, W), lambda b, h, t, *_: (b, h, 0, 0)),
    )
    return pl.pallas_call(
        _dispatch_kernel,
        out_shape=jax.ShapeDtypeStruct((B, N_EXPERTS, cap_total, W), BF16),
        grid_spec=grid_spec,
        compiler_params=_params(3),
        name="dispatch",
    )(first.reshape(-1), n_pass.reshape(-1), slot,
      first.astype(F32).reshape(B, nt, 1, N_EXPERTS), hx)


def _ffn_kernel(x_ref, wg_ref, wu_ref, wd_ref, o_ref):
    D = wg_ref.shape[1]
    ff = wg_ref.shape[2]
    x = x_ref[0, 0, :, :D]
    pieces = x_ref[0, 0, :, D:].astype(F32)
    lane = lax.broadcasted_iota(jnp.int32, pieces.shape, 1)
    mine = ((lane & (N_EXPERTS - 1)) == pl.program_id(0)) & (lane < 3 * N_EXPERTS)
    gate = jnp.sum(jnp.where(mine, pieces, 0.0), axis=-1, keepdims=True)
    chunk = 512
    acc = jnp.zeros((x.shape[0], D), F32)
    for j in range(ff // chunk):
        cols = slice(j * chunk, (j + 1) * chunk)
        g = _dot(x, wg_ref[0, :, cols])
        u = _dot(x, wu_ref[0, :, cols])
        hid = (g * jax.nn.sigmoid(g) * u).astype(BF16)
        acc = acc + _dot(hid, wd_ref[0, cols, :])
    o_ref[0, 0] = (acc * gate).astype(BF16)


def _ffn(xin, wg, wu, wd):
    B, E, S, W = xin.shape
    D, FF = wg.shape[1], wg.shape[2]
    return pl.pallas_call(
        _ffn_kernel,
        out_shape=jax.ShapeDtypeStruct((B, E, S, D), BF16),
        grid=(E, B),
        in_specs=[pl.BlockSpec((1, 1, S, W), lambda e, b: (b, e, 0, 0)),
                  pl.BlockSpec((1, D, FF), lambda e, b: (e, 0, 0)),
                  pl.BlockSpec((1, D, FF), lambda e, b: (e, 0, 0)),
                  pl.BlockSpec((1, FF, D), lambda e, b: (e, 0, 0))],
        out_specs=pl.BlockSpec((1, 1, S, D), lambda e, b: (b, e, 0, 0)),
        compiler_params=_params(2),
        name="expert_ffn",
    )(xin, wg, wu, wd)


def _combine_kernel(first_s, npass_s, slot_ref, first_ref, expand_ref, y_ref, x_ref, g2_ref,
                    *rest, n_tiles):
    o_ref = rest[-1]
    b, t = pl.program_id(0), pl.program_id(1)
    cap_total = y_ref.shape[2]
    o_ref[0] = x_ref[0]
    slot = slot_ref[0]
    first = first_ref[0, 0]
    lane_row = (lax.broadcasted_iota(jnp.int32, (TILE, N_EXPERTS * SLOT_WIN), 1)
                & (SLOT_WIN - 1)).astype(F32)

    def one_pass(k, carry):
        rel = _window_rel(slot, first, k.astype(F32), cap_total).astype(BF16)
        onehot = (_dot(rel, expand_ref[...]) == lane_row).astype(BF16)
        rows = jnp.concatenate(
            [y_ref[0, e, pl.ds(_window_begin(first_s[(b * n_tiles + t) * N_EXPERTS + e], k,
                                             cap_total), SLOT_WIN), :]
             for e in range(N_EXPERTS)], axis=0)
        o_ref[0] += g2_ref[0, 0] * _dot(onehot, rows)
        return carry

    lax.fori_loop(0, npass_s[b * n_tiles + t], one_pass, 0)
    if len(rest) == 2:
        x = o_ref[0]
        ms = jnp.mean(x * x, axis=-1, keepdims=True)
        o_ref[0] = x * lax.rsqrt(ms + RMS_EPS) * rest[0][...]


def _combine(y, slot, first, n_pass, xa, gate2, final_g=None):
    B, S, D = xa.shape
    nt = S // TILE
    E, cap_total = y.shape[1], y.shape[2]
    expand = np.repeat(np.eye(E, dtype=np.float32), SLOT_WIN, axis=1)
    in_specs = [pl.BlockSpec((1, TILE, E), lambda b, t, *_: (b, t, 0)),
                pl.BlockSpec((1, 1, 1, E), lambda b, t, *_: (b, t, 0, 0)),
                pl.BlockSpec((E, E * SLOT_WIN), lambda b, t, *_: (0, 0)),
                pl.BlockSpec((1, E, cap_total, D), lambda b, t, *_: (b, 0, 0, 0)),
                pl.BlockSpec((1, TILE, D), lambda b, t, *_: (b, t, 0)),
                pl.BlockSpec((1, 1, 1, D), lambda b, t, *_: (b, t // (nt - 1), 0, 0))]
    args = [slot, first.astype(F32).reshape(B, nt, 1, E), jnp.asarray(expand, BF16), y, xa, gate2]
    out_tiles = nt
    if final_g is not None:
        in_specs.append(pl.BlockSpec((1, D), lambda b, t, *_: (0, 0)))
        args.append(final_g)
        out_tiles = nt - 1
    grid_spec = pltpu.PrefetchScalarGridSpec(
        num_scalar_prefetch=2,
        grid=(B, out_tiles),
        in_specs=in_specs,
        out_specs=pl.BlockSpec((1, TILE, D), lambda b, t, *_: (b, t, 0)),
    )
    return pl.pallas_call(
        functools.partial(_combine_kernel, n_tiles=nt),
        out_shape=jax.ShapeDtypeStruct((B, out_tiles * TILE, D), F32),
        grid_spec=grid_spec,
        compiler_params=_params(2),
        name="combine",
    )(first.reshape(-1), n_pass.reshape(-1), *args)


def _scale_q(w, nq):
    return jnp.concatenate([w[:, :nq] * (HEAD_DIM ** -0.5 * LOG2E), w[:, nq:]], axis=1)


def _win_weights(w_in):
    D = w_in.shape[0]
    nq = 16 * HEAD_DIM
    nk = A_KV_HEADS * HEAD_DIM
    w = _scale_q(w_in, nq)
    dup = lambda m: jnp.concatenate([m.reshape(D, A_KV_HEADS, 1, HEAD_DIM)] * 2,
                                    axis=2).reshape(D, 2 * nk)
    return jnp.concatenate([w[:, :nq], dup(w[:, nq:nq + nk]), dup(w[:, nq + nk:])],
                           axis=1).astype(BF16)


def _router_weights(rw):
    D, E = rw.shape
    return jnp.concatenate([rw, rw, rw, jnp.zeros((D, GATE_COLS - 3 * E), rw.dtype)], axis=1)


def kernel(x, c, ctx, c_ctx, ada_w, ada_b, norm1_g, norm2_g, final_g, win_w_in, win_w_out,
           win_sink, diff_w_in, diff_w_out, diff_lambda, diff_subln_g, na_w_in, na_w_out,
           na_rpb, router_w, w_gate, w_up, w_down):
    B, T, D = x.shape
    L = ctx.shape[1]
    S = T + L
    depth = ada_w.shape[0]
    assert L == TILE and T % TILE == 0 and (T // GRID_W) >= NA_WIN_ROWS
    cs = jnp.concatenate([c, c_ctx[None, :]], axis=0)
    rope = _rope_tables(T, L)
    xa = None
    for i in range(depth):
        kind = i % N_MIXERS
        slot = i // N_MIXERS
        mod = _ada(cs, ada_w[i], ada_b[i][None, :])
        mod = jnp.stack([mod[:B], jnp.broadcast_to(mod[B:], (B, 6 * D))], axis=1)
        mod = mod.reshape(B, 2, 1, 6, D)
        sh1, sc1, g1, sh2, sc2, g2 = [mod[:, :, :, k, :] for k in range(6)]
        if kind == 0:
            w_in = _win_weights(win_w_in[slot])
            n_rope = (16 + 2 * A_KV_HEADS) * HEAD_DIM
            w_out = win_w_out[slot]
        elif kind == 1:
            w_in = _scale_q(diff_w_in[slot], 16 * HEAD_DIM).astype(BF16)
            n_rope = 32 * HEAD_DIM
            w_out = diff_w_out[slot]
        else:
            w_in = _scale_q(na_w_in[slot], 16 * HEAD_DIM).astype(BF16)
            n_rope = 0
            w_out = na_w_out[slot]
        if i == 0:
            qkv, xa = _norm_proj((x, ctx), norm1_g[i][None, :], sh1, sc1, w_in, rope, n_rope)
        else:
            qkv = _norm_proj((xa,), norm1_g[i][None, :], sh1, sc1, w_in, rope, n_rope)
        if kind == 0:
            y = _attn_win(qkv, win_sink[slot], T)
        elif kind == 1:
            lambda_init = 0.8 - 0.6 * math.exp(-0.3 * i)
            lp = diff_lambda[slot]
            lam = (jnp.exp(jnp.sum(lp[0] * lp[1])) - jnp.exp(jnp.sum(lp[2] * lp[3]))
                   + lambda_init).reshape(1)
            y = _attn_diff(qkv, lam, diff_subln_g[slot][None, :], T, 1.0 - lambda_init)
        else:
            y = _attn_na(qkv, _na_bias_tables(na_rpb[slot], T), T)
        xa, hx, aff = _out_router(y, w_out.astype(BF16), xa, g1, norm2_g[i][None, :],
                                  sh2, sc2, _router_weights(router_w[i]))
        tok_slot, start, cnt = _route(aff, T)
        first, n_pass = _slot_plan(start, cnt)
        cap_total = CAPACITY_FACTOR * S // N_EXPERTS
        xin = _dispatch(hx, tok_slot, first, n_pass, cap_total)
        ye = _ffn(xin, w_gate[i].astype(BF16), w_up[i].astype(BF16), w_down[i].astype(BF16))
        last = i == depth - 1
        xa = _combine(ye, tok_slot, first, n_pass, xa, g2, final_g[None, :] if last else None)
    return xa
```

```python
import functools
import math

import numpy as np
import jax
import jax.numpy as jnp
from jax import lax
from jax.experimental import pallas as pl
from jax.experimental.pallas import tpu as pltpu

HEAD_DIM = 64
LANES = 128
GRID_W = 64
NA_ROWS = 8
NA_COLS = 16
A_WINDOW = 128
A_KV_HEADS = 4
N_MIXERS = 3
ROPE_BASE = 10000.0
ROPE_AXIS_DIM = HEAD_DIM // 2
N_EXPERTS = 16
CAPACITY_FACTOR = 2
RMS_EPS = 1e-6
NEG_INF = -1e30
LOG2E = math.log2(math.e)
TILE = 256
NA_TILE_ROWS = TILE // GRID_W
NA_WIN_ROWS = NA_TILE_ROWS + NA_ROWS
VMEM_LIMIT = 56 * 1024 * 1024
SLOT_WIN = 64
SLOT_ALIGN = 16
GATE_COLS = LANES

BF16 = jnp.bfloat16
F32 = jnp.float32


def _params(n_grid):
    return pltpu.CompilerParams(
        dimension_semantics=("arbitrary",) * n_grid, vmem_limit_bytes=VMEM_LIMIT)


def _split_bf16(a):
    hi = a.astype(BF16)
    lo = (a - hi.astype(F32)).astype(BF16)
    return hi, lo


def _dot(a, b):
    return jnp.dot(a, b, preferred_element_type=F32)


def _dot_nt(a, b):
    return lax.dot_general(a, b, (((1,), (1,)), ((), ())), preferred_element_type=F32)


def _dot_tn(a, b):
    return lax.dot_general(a, b, (((0,), (0,)), ((), ())), preferred_element_type=F32)


def _ada_kernel(c_ref, w_ref, b_ref, o_ref):
    c = c_ref[...]
    a = c * jax.nn.sigmoid(c)
    a_hi, a_lo = _split_bf16(a)
    w_hi, w_lo = _split_bf16(w_ref[...])
    o_ref[...] = _dot(a_hi, w_hi) + _dot(a_lo, w_hi) + _dot(a_hi, w_lo) + b_ref[...]


def _ada(cs, w, b):
    R, D = cs.shape
    N = w.shape[1]
    tn = 1024
    return pl.pallas_call(
        _ada_kernel,
        out_shape=jax.ShapeDtypeStruct((R, N), F32),
        grid=(N // tn,),
        in_specs=[pl.BlockSpec((R, D), lambda j: (0, 0)),
                  pl.BlockSpec((D, tn), lambda j: (0, j)),
                  pl.BlockSpec((1, tn), lambda j: (0, j))],
        out_specs=pl.BlockSpec((R, tn), lambda j: (0, j)),
        compiler_params=_params(1),
        name="ada",
    )(cs, w, b)


def _rms_mod(x, g, shift, scale):
    ms = jnp.mean(x * x, axis=-1, keepdims=True)
    y = x * lax.rsqrt(ms + RMS_EPS) * g
    return y * (1.0 + scale) + shift


def _norm_proj_kernel(*refs, n_rope, joins_streams):
    if joins_streams:
        x_ref, c_ref, g_ref, sh_ref, sc_ref, w_ref, cos_ref, sa_ref, sb_ref, o_ref, xa_ref = refs
        is_ctx = pl.program_id(1) == pl.num_programs(1) - 1
        x = jnp.where(is_ctx, c_ref[0], x_ref[0])
        xa_ref[0] = x
    else:
        x_ref, g_ref, sh_ref, sc_ref, w_ref, cos_ref, sa_ref, sb_ref, o_ref = refs
        x = x_ref[0]
    h = _rms_mod(x, g_ref[...], sh_ref[0, 0], sc_ref[0, 0]).astype(BF16)
    n_cols = w_ref.shape[1]
    chunk = 512
    for j in range(n_cols // chunk):
        acc = _dot(h, w_ref[:, j * chunk:(j + 1) * chunk])
        for t in range(chunk // LANES):
            col = j * chunk + t * LANES
            a = acc[:, t * LANES:(t + 1) * LANES]
            if col < n_rope:
                a = (a * cos_ref[...]
                     + pltpu.roll(a, LANES - 16, 1) * sa_ref[...]
                     + pltpu.roll(a, 16, 1) * sb_ref[...])
            o_ref[0, :, col:col + LANES] = a.astype(BF16)


def _norm_proj(streams, g, shift, scale, w, rope, n_rope):
    joins = len(streams) == 2
    B, _, D = streams[0].shape
    S = sum(a.shape[1] for a in streams)
    N = w.shape[1]
    nt = S // TILE
    mod_spec = pl.BlockSpec((1, 1, 1, D), lambda b, t: (b, t // (nt - 1), 0, 0))
    rope_spec = pl.BlockSpec((TILE, LANES), lambda b, t: (t, 0))
    tile_spec = pl.BlockSpec((1, TILE, D), lambda b, t: (b, t, 0))
    if joins:
        stream_specs = [pl.BlockSpec((1, TILE, D), lambda b, t: (b, jnp.minimum(t, nt - 2), 0)),
                        pl.BlockSpec((1, TILE, D), lambda b, t: (b, 0, 0))]
    else:
        stream_specs = [tile_spec]
    qkv_shape = jax.ShapeDtypeStruct((B, S, N), BF16)
    qkv_spec = pl.BlockSpec((1, TILE, N), lambda b, t: (b, t, 0))
    return pl.pallas_call(
        functools.partial(_norm_proj_kernel, n_rope=n_rope, joins_streams=joins),
        out_shape=(qkv_shape, jax.ShapeDtypeStruct((B, S, D), F32)) if joins else qkv_shape,
        grid=(B, nt),
        in_specs=stream_specs + [pl.BlockSpec((1, D), lambda b, t: (0, 0)),
                                 mod_spec, mod_spec,
                                 pl.BlockSpec((D, N), lambda b, t: (0, 0)),
                                 rope_spec, rope_spec, rope_spec],
        out_specs=(qkv_spec, tile_spec) if joins else qkv_spec,
        compiler_params=_params(2),
        name="norm_proj",
    )(*streams, g, shift, scale, w, *rope)


def _rope_tables(T, L):
    t = np.arange(T)
    pos = np.stack([t // GRID_W, t % GRID_W], axis=0).astype(np.float32)
    inv = (1.0 / (ROPE_BASE ** (np.arange(0, ROPE_AXIS_DIM, 2, dtype=np.float32)
                                / ROPE_AXIS_DIM))).astype(np.float32)
    d = np.arange(LANES) % HEAD_DIM
    axis = d // ROPE_AXIS_DIM
    half = (d % ROPE_AXIS_DIM) // (ROPE_AXIS_DIM // 2)
    freq = d % (ROPE_AXIS_DIM // 2)
    ang = jnp.asarray(pos[axis].T) * jnp.asarray(inv[freq])[None, :]
    cos = jnp.cos(ang)
    sin = jnp.sin(ang)
    first = jnp.asarray(half == 0)[None, :]
    sa = jnp.where(first, -sin, 0.0)
    sb = jnp.where(first, 0.0, sin)
    pad = lambda a, v: jnp.concatenate([a, jnp.full((L, LANES), v, F32)], axis=0)
    return pad(cos, 1.0), pad(sa, 0.0), pad(sb, 0.0)


def _half_masks(q):
    lane = lax.broadcasted_iota(jnp.int32, q.shape, 1)
    zero = jnp.zeros_like(q)
    return jnp.where(lane < HEAD_DIM, q, zero), jnp.where(lane >= HEAD_DIM, q, zero)


def _merge_halves(o_first, o_second):
    lane = lax.broadcasted_iota(jnp.int32, o_first.shape, 1)
    return jnp.where(lane < HEAD_DIM, o_first, o_second)


SHIFT_SLACK = 1.02
L_MIN = 2.0 ** -64


def _sq_norms_row(x):
    xf = x.astype(F32)
    return _dot_nt(jnp.ones((8, LANES), BF16), (xf * xf).astype(BF16))[:1]


def _head_bounds_row(qs):
    qf = qs.astype(F32)
    n2 = jnp.sum(qf * qf, axis=-1, keepdims=True)
    return jnp.concatenate(
        [jnp.broadcast_to(jnp.max(n2[r:r + TILE], axis=0, keepdims=True), (1, TILE))
         for r in range(0, qs.shape[0], TILE)], axis=1)


def _store_key_bound(kmax_ref, j, k, scale=1.0):
    n2 = jnp.max(_sq_norms_row(k), axis=1, keepdims=True) * scale
    kmax_ref[j] = jnp.broadcast_to(n2, kmax_ref.shape[1:])


def _redo_if_underflow(attend, *args):
    den_min = attend(*args, exact=False)

    @pl.when(den_min[0, 0] < L_MIN)
    def _():
        attend(*args, exact=True)


WIN_KV_PER_STEP = 4


def _win_kernel(sink_ref, q_ref, k_ref, v_ref, o_ref, kmax_ref, *, T, win):
    hb = pl.program_id(1)
    t = pl.program_id(2)
    n_x = T // TILE
    is_x = t < n_x
    heads = [slice(j * LANES, (j + 1) * LANES) for j in range(WIN_KV_PER_STEP)]

    @pl.when(t == 0)
    def _():
        for j, cols in enumerate(heads):
            _store_key_bound(kmax_ref, j, k_ref[0, :, cols], 0.5)

    start = pl.multiple_of(jnp.clip(t * TILE - A_WINDOW, 0, T - win), LANES)
    kpos = start + lax.broadcasted_iota(jnp.int32, (win, TILE), 0)
    qpos = t * TILE + lax.broadcasted_iota(jnp.int32, (win, TILE), 1)
    band = (jnp.abs(qpos - kpos) <= A_WINDOW) & is_x
    band = jnp.concatenate([band] * 4, axis=1)

    def attend(exact):
        queries, scores = [], []
        for j, cols in enumerate(heads):
            q = q_ref[0, :, 2 * j * LANES:2 * (j + 1) * LANES]
            qa, qb = _half_masks(q[:, :LANES])
            qc, qd = _half_masks(q[:, LANES:])
            qs = jnp.concatenate([qa, qb, qc, qd], axis=0)
            queries.append(qs)
            scores.append((_dot_nt(k_ref[0, T:, cols], qs),
                           _dot_nt(k_ref[0, pl.ds(start, win), cols], qs)))
        den_min = None
        for j, cols in enumerate(heads):
            s_c, s_w = scores[j]
            head0 = (hb * WIN_KV_PER_STEP + j) * 4
            s_w = jnp.where(band, s_w, NEG_INF)
            sink = jnp.concatenate(
                [jnp.full((1, TILE), sink_ref[head0 + g] * LOG2E, F32) for g in range(4)], axis=1)
            if exact:
                m = jnp.maximum(jnp.max(s_c, axis=0, keepdims=True),
                                jnp.max(s_w, axis=0, keepdims=True))
            else:
                m = jnp.sqrt(_head_bounds_row(queries[j]) * kmax_ref[j, :1, :1]) * SHIFT_SLACK
            m = jnp.maximum(m, sink)
            ec = jnp.exp2(s_c - m)
            ew = jnp.exp2(s_w - m)
            den = (jnp.sum(ec, axis=0, keepdims=True) + jnp.sum(ew, axis=0, keepdims=True)
                   + jnp.exp2(sink - m))
            o = ((_dot_tn(v_ref[0, T:, cols], ec.astype(BF16))
                  + _dot_tn(v_ref[0, pl.ds(start, win), cols], ew.astype(BF16))) * (1.0 / den)).T
            first = 2 * j * LANES
            o_ref[0, :, first:first + LANES] = _merge_halves(
                o[:TILE], o[TILE:2 * TILE]).astype(BF16)
            o_ref[0, :, first + LANES:first + 2 * LANES] = _merge_halves(
                o[2 * TILE:3 * TILE], o[3 * TILE:]).astype(BF16)
            low = jnp.min(den, axis=1, keepdims=True)
            den_min = low if den_min is None else jnp.minimum(den_min, low)
        return den_min

    _redo_if_underflow(attend)


def _attn_win(qkv, sink, T):
    B, S, _ = qkv.shape
    nt = S // TILE
    nq = 16 * HEAD_DIM
    win = TILE + 2 * A_WINDOW
    hb = A_KV_HEADS // WIN_KV_PER_STEP
    wq = WIN_KV_PER_STEP * 2 * LANES
    wk = WIN_KV_PER_STEP * LANES
    kb = nq // wk
    return pl.pallas_call(
        functools.partial(_win_kernel, T=T, win=win),
        out_shape=jax.ShapeDtypeStruct((B, S, nq), BF16),
        grid=(B, hb, nt),
        in_specs=[pl.BlockSpec(memory_space=pltpu.SMEM),
                  pl.BlockSpec((1, TILE, wq), lambda b, h, t: (b, t, h)),
                  pl.BlockSpec((1, S, wk), lambda b, h, t: (b, 0, kb + h)),
                  pl.BlockSpec((1, S, wk), lambda b, h, t: (b, 0, kb + hb + h))],
        out_specs=pl.BlockSpec((1, TILE, wq), lambda b, h, t: (b, t, h)),
        scratch_shapes=[pltpu.VMEM((WIN_KV_PER_STEP, 8, LANES), F32)],
        compiler_params=_params(3),
        name="attn_win",
    )(sink, qkv, qkv, qkv)


DIFF_HEADS_PER_STEP = 2


def _diff_kernel(lam_ref, q_ref, k_ref, v_ref, g_ref, o_ref, kmax_ref, *, T, out_scale):
    t = pl.program_id(2)
    n_x = T // TILE
    lam = lam_ref[0]
    heads = [slice(j * LANES, (j + 1) * LANES) for j in range(DIFF_HEADS_PER_STEP)]

    @pl.when(t == 0)
    def _():
        for j, cols in enumerate(heads):
            k1, k2 = _half_masks(k_ref[0, :, cols])
            _store_key_bound(kmax_ref, 2 * j, k1)
            _store_key_bound(kmax_ref, 2 * j + 1, k2)

    def attend(keys, exact):
        queries, scores = [], []
        for cols in heads:
            qa, qb = _half_masks(q_ref[0, :, cols])
            qs = jnp.concatenate([qa, qb], axis=0)
            queries.append(qs)
            scores.append(_dot_nt(qs, k_ref[0, keys, cols]))
        l_min = None
        for j, cols in enumerate(heads):
            s = scores[j]
            if exact:
                m = jnp.max(s, axis=-1, keepdims=True)
            else:
                qf = queries[j].astype(F32)
                row = lax.broadcasted_iota(jnp.int32, (2 * TILE, 1), 0)
                kmax = jnp.where(row < TILE, kmax_ref[2 * j, :1, :1], kmax_ref[2 * j + 1, :1, :1])
                m = jnp.sqrt(jnp.sum(qf * qf, axis=-1, keepdims=True) * kmax) * SHIFT_SLACK
            e = jnp.exp2(s - m)
            l = jnp.sum(e, axis=-1, keepdims=True)
            a = e[:TILE] - e[TILE:] * (lam * l[:TILE] / l[TILE:])
            o = _dot(a.astype(BF16), v_ref[0, keys, cols]) / l[:TILE]
            ms = jnp.mean(o * o, axis=-1, keepdims=True)
            y = o * lax.rsqrt(ms + RMS_EPS) * g_ref[...] * out_scale
            o_ref[0, :, cols] = y.astype(BF16)
            low = jnp.min(l, axis=0, keepdims=True)
            l_min = low if l_min is None else jnp.minimum(l_min, low)
        return l_min

    @pl.when(t < n_x)
    def _():
        _redo_if_underflow(attend, slice(None))

    @pl.when(t >= n_x)
    def _():
        _redo_if_underflow(attend, slice(T, None))


def _attn_diff(qkv, lam, subln_g, T, out_scale):
    B, S, _ = qkv.shape
    nt = S // TILE
    H = 8
    hb = H // DIFF_HEADS_PER_STEP
    w = DIFF_HEADS_PER_STEP * LANES
    return pl.pallas_call(
        functools.partial(_diff_kernel, T=T, out_scale=out_scale),
        out_shape=jax.ShapeDtypeStruct((B, S, H * LANES), BF16),
        grid=(B, hb, nt),
        in_specs=[pl.BlockSpec(memory_space=pltpu.SMEM),
                  pl.BlockSpec((1, TILE, w), lambda b, h, t: (b, t, h)),
                  pl.BlockSpec((1, S, w), lambda b, h, t: (b, 0, hb + h)),
                  pl.BlockSpec((1, S, w), lambda b, h, t: (b, 0, 2 * hb + h)),
                  pl.BlockSpec((1, LANES), lambda b, h, t: (0, 0))],
        out_specs=pl.BlockSpec((1, TILE, w), lambda b, h, t: (b, t, h)),
        scratch_shapes=[pltpu.VMEM((2 * DIFF_HEADS_PER_STEP, 8, LANES), F32)],
        compiler_params=_params(3),
        name="attn_diff",
    )(lam, qkv, qkv, qkv, subln_g)


NA_PAIRS_PER_STEP = 2


def _na_kernel(q_ref, k_ref, v_ref, bias_ref, bmax_ref, o_ref, kmax_ref, *, T):
    t = pl.program_id(2)
    n_x = T // TILE
    rows = T // GRID_W
    win = NA_WIN_ROWS * GRID_W
    pairs = [slice(j * LANES, (j + 1) * LANES) for j in range(NA_PAIRS_PER_STEP)]

    @pl.when(t == 0)
    def _():
        for j, cols in enumerate(pairs):
            k1, k2 = _half_masks(k_ref[0, :, cols])
            _store_key_bound(kmax_ref, 2 * j, k1)
            _store_key_bound(kmax_ref, 2 * j + 1, k2)

    cls = jnp.where(t >= n_x, 3, jnp.where(t == 0, 0, jnp.where(t == n_x - 1, 2, 1)))
    row0 = jnp.clip(t * NA_TILE_ROWS - NA_ROWS // 2, 0, rows - NA_WIN_ROWS)
    start = pl.multiple_of(row0 * GRID_W, GRID_W)

    def attend(exact):
        queries, scores = [], []
        for cols in pairs:
            qa, qb = _half_masks(q_ref[0, :, cols])
            qs = jnp.concatenate([qa, qb], axis=0)
            queries.append(qs)
            scores.append((_dot_nt(k_ref[0, T:, cols], qs),
                           _dot_nt(k_ref[0, pl.ds(start, win), cols], qs)))
        den_min = None
        for j, cols in enumerate(pairs):
            s_c, s_w = scores[j]
            s_w = s_w + bias_ref[cls, j]
            if exact:
                m = jnp.maximum(jnp.max(s_c, axis=0, keepdims=True),
                                jnp.max(s_w, axis=0, keepdims=True))
            else:
                lane = lax.broadcasted_iota(jnp.int32, (1, 2 * TILE), 1)
                kmax = jnp.where(lane < TILE, kmax_ref[2 * j, :1, :1], kmax_ref[2 * j + 1, :1, :1])
                m = jnp.sqrt(_head_bounds_row(queries[j]) * kmax) * SHIFT_SLACK + bmax_ref[j]
            ec = jnp.exp2(s_c - m)
            ew = jnp.exp2(s_w - m)
            den = jnp.sum(ec, axis=0, keepdims=True) + jnp.sum(ew, axis=0, keepdims=True)
            o = ((_dot_tn(v_ref[0, T:, cols], ec.astype(BF16))
                  + _dot_tn(v_ref[0, pl.ds(start, win), cols], ew.astype(BF16))) * (1.0 / den)).T
            o_ref[0, :, cols] = _merge_halves(o[:TILE], o[TILE:]).astype(BF16)
            low = jnp.min(den, axis=1, keepdims=True)
            den_min = low if den_min is None else jnp.minimum(den_min, low)
        return den_min

    _redo_if_underflow(attend)


def _attn_na(qkv, bias, bias_max, T):
    B, S, _ = qkv.shape
    nt = S // TILE
    n = NA_PAIRS_PER_STEP
    hb = 8 // n
    w = n * LANES
    win = NA_WIN_ROWS * GRID_W
    return pl.pallas_call(
        functools.partial(_na_kernel, T=T),
        out_shape=jax.ShapeDtypeStruct((B, S, 8 * LANES), BF16),
        grid=(hb, B, nt),
        in_specs=[pl.BlockSpec((1, TILE, w), lambda h, b, t: (b, t, h)),
                  pl.BlockSpec((1, S, w), lambda h, b, t: (b, 0, hb + h)),
                  pl.BlockSpec((1, S, w), lambda h, b, t: (b, 0, 2 * hb + h)),
                  pl.BlockSpec((4, n, win, 2 * TILE), lambda h, b, t: (0, h, 0, 0)),
                  pl.BlockSpec((n, 1, 2 * TILE), lambda h, b, t: (h, 0, 0))],
        out_specs=pl.BlockSpec((1, TILE, w), lambda h, b, t: (b, t, h)),
        scratch_shapes=[pltpu.VMEM((2 * NA_PAIRS_PER_STEP, 8, LANES), F32)],
        compiler_params=_params(3),
        name="attn_na",
    )(qkv, qkv, qkv, bias, bias_max)


def _na_bias_bound(rpb):
    top = jnp.maximum(jnp.max(rpb, axis=(1, 2)), 0.0) * LOG2E
    return jnp.repeat(top.reshape(-1, 2), TILE, axis=1)[:, None, :]


def _na_bias_tables(rpb, T):
    rows = T // GRID_W
    H = rpb.shape[0]
    win = NA_WIN_ROWS * GRID_W
    hi = lax.Precision.HIGHEST
    pick = lambda idx, n: jnp.asarray(idx[..., None] == np.arange(n), F32)
    c = np.arange(GRID_W)
    cs = np.clip(c - NA_COLS // 2, 0, GRID_W - NA_COLS)
    valid_c = (c[None, :] >= cs[:, None]) & (c[None, :] < cs[:, None] + NA_COLS)
    bidx_c = np.clip(c[None, :] - c[:, None] + NA_COLS - 1, 0, 2 * NA_COLS - 2)
    by_col = jnp.einsum('hrd,ckd->hrck', rpb, pick(bidx_c, 2 * NA_COLS - 1), precision=hi)
    i = np.arange(NA_TILE_ROWS)
    j = np.arange(NA_WIN_ROWS)
    tables = []
    for r0 in (0, NA_TILE_ROWS, rows - NA_TILE_ROWS):
        s = int(np.clip(r0 - NA_ROWS // 2, 0, rows - NA_WIN_ROWS))
        r = r0 + i
        rs = np.clip(r - NA_ROWS // 2, 0, rows - NA_ROWS)
        kr = s + j
        valid_r = (kr[None, :] >= rs[:, None]) & (kr[None, :] < rs[:, None] + NA_ROWS)
        bidx_r = np.clip(kr[None, :] - r[:, None] + NA_ROWS - 1, 0, 2 * NA_ROWS - 2)
        vals = jnp.einsum('hrck,ijr->hjkic', by_col, pick(bidx_r, 2 * NA_ROWS - 1), precision=hi)
        valid = valid_r.T[:, None, :, None] & valid_c.T[None, :, None, :]
        vals = jnp.where(jnp.asarray(valid)[None], vals, NEG_INF).reshape(H // 2, 2, win, TILE)
        tables.append(vals.transpose(0, 2, 1, 3).reshape(H // 2, win, 2 * TILE))
    tables.append(jnp.full_like(tables[0], NEG_INF))
    return jnp.stack(tables, axis=0).astype(F32) * LOG2E


def _out_router_kernel(y_ref, w_ref, x_ref, g1_ref, n2_ref, sh_ref, sc_ref, rw_ref,
                       xo_ref, h_ref, aff_ref):
    D = x_ref.shape[2]
    x = x_ref[0] + g1_ref[0, 0] * _dot(y_ref[0], w_ref[...])
    xo_ref[0] = x
    h = _rms_mod(x, n2_ref[...], sh_ref[0, 0], sc_ref[0, 0])
    h_hi, h_lo = _split_bf16(h)
    r_hi, r_lo = _split_bf16(rw_ref[...])
    logits = _dot(h_hi, r_hi) + _dot(h_lo, r_hi) + _dot(h_hi, r_lo)
    lane = lax.broadcasted_iota(jnp.int32, logits.shape, 1)
    first = lane < N_EXPERTS
    m = jnp.max(jnp.where(first, logits, -jnp.inf), axis=-1, keepdims=True)
    e = jnp.exp(logits - m)
    aff = e / jnp.sum(jnp.where(first, e, 0.0), axis=-1, keepdims=True)
    aff_ref[0] = aff
    hi = aff.astype(BF16)
    rem = aff - hi.astype(F32)
    mid = rem.astype(BF16)
    lo = (rem - mid.astype(F32)).astype(BF16)
    zero = jnp.zeros_like(hi)
    pieces = jnp.where(first, hi, jnp.where(lane < 2 * N_EXPERTS, mid,
                                            jnp.where(lane < 3 * N_EXPERTS, lo, zero)))
    h_ref[0, :, :D] = h_hi
    h_ref[0, :, D:] = pieces


def _out_router(y, w_out, xa, gate1, n2g, shift2, scale2, rw3):
    B, S, D = xa.shape
    nt = S // TILE
    mod_spec = pl.BlockSpec((1, 1, 1, D), lambda b, t: (b, t // (nt - 1), 0, 0))
    tile_spec = pl.BlockSpec((1, TILE, D), lambda b, t: (b, t, 0))
    return pl.pallas_call(
        _out_router_kernel,
        out_shape=(jax.ShapeDtypeStruct((B, S, D), F32),
                   jax.ShapeDtypeStruct((B, S, D + GATE_COLS), BF16),
                   jax.ShapeDtypeStruct((B, S, GATE_COLS), F32)),
        grid=(B, nt),
        in_specs=[tile_spec,
                  pl.BlockSpec((D, D), lambda b, t: (0, 0)),
                  tile_spec, mod_spec,
                  pl.BlockSpec((1, D), lambda b, t: (0, 0)),
                  mod_spec, mod_spec,
                  pl.BlockSpec((D, GATE_COLS), lambda b, t: (0, 0))],
        out_specs=(tile_spec,
                   pl.BlockSpec((1, TILE, D + GATE_COLS), lambda b, t: (b, t, 0)),
                   pl.BlockSpec((1, TILE, GATE_COLS), lambda b, t: (b, t, 0))),
        compiler_params=_params(2),
        name="out_router",
    )(y, w_out, xa, gate1, n2g, shift2, scale2, rw3)


def _route_kernel(aff_ref, slot_ref, start_ref, cnt_ref, *, T, cap_x, cap_c):
    S = aff_ref.shape[1]
    aff = aff_ref[0]
    bits = lax.bitcast_convert_type(aff[:, :N_EXPERTS], jnp.int32)
    ri = lax.broadcasted_iota(jnp.int32, (TILE, TILE), 0)
    ci = lax.broadcasted_iota(jnp.int32, (TILE, TILE), 1)
    ltri = (ri > ci).astype(BF16)
    eye = (lax.broadcasted_iota(jnp.int32, (N_EXPERTS, N_EXPERTS), 0)
           == lax.broadcasted_iota(jnp.int32, (N_EXPERTS, N_EXPERTS), 1))

    def count(mask):
        return jnp.sum(mask.astype(F32), axis=0, keepdims=True)

    for lo_row, hi_row, cap, base in ((0, T, cap_x, 0), (T, S, cap_c, cap_x)):
        b = bits[lo_row:hi_row]
        dense = lax.bitcast_convert_type(jnp.concatenate(
            [aff[r:r + TILE].T[:N_EXPERTS] for r in range(lo_row, hi_row, TILE)], axis=1),
            jnp.int32)

        def step(i, thr):
            cand = thr | lax.shift_left(jnp.int32(1), 30 - i)
            n_ge = jnp.sum((dense >= cand).astype(F32), axis=1, keepdims=True)
            return jnp.where(n_ge >= cap, cand, thr)

        thr = lax.fori_loop(0, 31, step, jnp.zeros((N_EXPERTS, 1), jnp.int32))
        thr = jnp.max(jnp.where(eye, jnp.broadcast_to(thr, eye.shape), 0), axis=0,
                      keepdims=True)
        need = cap - count(b > thr)
        seen_eq = jnp.zeros((1, N_EXPERTS), F32)
        seen = jnp.zeros((1, N_EXPERTS), F32)
        for j in range((hi_row - lo_row) // TILE):
            blk = b[j * TILE:(j + 1) * TILE]
            gt = blk > thr
            eq = blk == thr
            eq_rank = _dot(ltri, eq.astype(BF16)) + seen_eq
            sel = gt | (eq & (eq_rank < need))
            pos = _dot(ltri, sel.astype(BF16)) + seen + base
            t = lo_row // TILE + j
            slot_ref[0, t * TILE:(t + 1) * TILE, :] = jnp.where(sel, pos, -1.0)
            n_sel = count(sel)
            start_ref[0, t:t + 1, :] = seen + base
            cnt_ref[0, t:t + 1, :] = n_sel
            seen_eq = seen_eq + count(eq)
            seen = seen + n_sel


def _route(aff, T):
    B, S, _ = aff.shape
    E = N_EXPERTS
    nt = S // TILE
    cap_x = CAPACITY_FACTOR * T // N_EXPERTS
    cap_c = CAPACITY_FACTOR * (S - T) // N_EXPERTS
    plan = jax.ShapeDtypeStruct((B, nt, E), F32)
    return pl.pallas_call(
        functools.partial(_route_kernel, T=T, cap_x=cap_x, cap_c=cap_c),
        out_shape=(jax.ShapeDtypeStruct((B, S, E), F32), plan, plan),
        grid=(B,),
        in_specs=[pl.BlockSpec((1, S, GATE_COLS), lambda b: (b, 0, 0))],
        out_specs=(pl.BlockSpec((1, S, E), lambda b: (b, 0, 0)),
                   pl.BlockSpec((1, nt, E), lambda b: (b, 0, 0)),
                   pl.BlockSpec((1, nt, E), lambda b: (b, 0, 0))),
        compiler_params=_params(1),
        name="route",
    )(aff)


def _slot_plan(start, cnt):
    start = start.astype(jnp.int32)
    cnt = cnt.astype(jnp.int32)
    first = (start // SLOT_ALIGN) * SLOT_ALIGN
    n_pass = jnp.max((start - first + cnt + SLOT_WIN - 1) // SLOT_WIN, axis=-1)
    return first, n_pass.astype(jnp.int32)


def _window_rel(slot, first, k, cap_total):
    nominal = first + k * SLOT_WIN
    begin = jnp.minimum(nominal, float(cap_total - SLOT_WIN))
    rel = slot - nominal
    return jnp.where((rel >= 0) & (rel < SLOT_WIN), rel + (nominal - begin), 255.0)


def _window_begin(first_s, k, cap_total):
    return pl.multiple_of(jnp.minimum(first_s + k * SLOT_WIN, cap_total - SLOT_WIN), SLOT_ALIGN)


def _dispatch_kernel(first_s, npass_s, slot_ref, first_ref, h_ref, o_ref):
    b, half, t = pl.program_id(0), pl.program_id(1), pl.program_id(2)
    nt = pl.num_programs(2)
    n_e = o_ref.shape[1]
    cap_total = o_ref.shape[2]

    @pl.when(t == 0)
    def _():
        o_ref[...] = jnp.zeros_like(o_ref)

    slot = slot_ref[0]
    first = first_ref[0, 0]
    pick_r = lax.broadcasted_iota(jnp.int32, (n_e, N_EXPERTS), 0)
    pick_c = lax.broadcasted_iota(jnp.int32, (n_e, N_EXPERTS), 1)
    pick = (pick_c == pick_r + half * n_e).astype(BF16)
    sub = lax.broadcasted_iota(jnp.int32, (SLOT_WIN, TILE), 0).astype(F32)

    def one_pass(k, carry):
        rel = _window_rel(slot, first, k.astype(F32), cap_total).astype(BF16)
        rel_t = _dot_nt(pick, rel)
        onehot = jnp.concatenate(
            [(jnp.broadcast_to(rel_t[e:e + 1, :], (SLOT_WIN, TILE)) == sub).astype(BF16)
             for e in range(n_e)], axis=0)
        rows = _dot(onehot, h_ref[0])
        for e in range(n_e):
            base = (b * nt + t) * N_EXPERTS + half * n_e + e
            win = pl.ds(_window_begin(first_s[base], k, cap_total), SLOT_WIN)
            cur = o_ref[0, e, win, :].astype(F32)
            o_ref[0, e, win, :] = (cur + rows[e * SLOT_WIN:(e + 1) * SLOT_WIN]).astype(BF16)
        return carry

    lax.fori_loop(0, npass_s[b * nt + t], one_pass, 0)


def _dispatch(hx, slot, first, n_pass, cap_total):
    B, S, W = hx.shape
    nt = S // TILE
    halves = 1
    n_e = N_EXPERTS // halves
    grid_spec = pltpu.PrefetchScalarGridSpec(
        num_scalar_prefetch=2,
        grid=(B, halves, nt),
        in_specs=[pl.BlockSpec((1, TILE, N_EXPERTS), lambda b, h, t, *_: (b, t, 0)),
                  pl.BlockSpec((1, 1, 1, N_EXPERTS), lambda b, h, t, *_: (b, t, 0, 0)),
                  pl.BlockSpec((1, TILE, W), lambda b, h, t, *_: (b, t, 0))],
        out_specs=pl.BlockSpec((1, n_e, cap_total, W), lambda b, h, t, *_: (b, h, 0, 0)),
    )
    return pl.pallas_call(
        _dispatch_kernel,
        out_shape=jax.ShapeDtypeStruct((B, N_EXPERTS, cap_total, W), BF16),
        grid_spec=grid_spec,
        compiler_params=_params(3),
        name="dispatch",
    )(first.reshape(-1), n_pass.reshape(-1), slot,
      first.astype(F32).reshape(B, nt, 1, N_EXPERTS), hx)


def _ffn_kernel(x_ref, wg_ref, wu_ref, wd_ref, o_ref):
    D = wg_ref.shape[2]
    ff = wg_ref.shape[3]
    x = x_ref[0, 0, :, :D]
    pieces = x_ref[0, 0, :, D:].astype(F32)
    lane = lax.broadcasted_iota(jnp.int32, pieces.shape, 1)
    mine = ((lane & (N_EXPERTS - 1)) == pl.program_id(0)) & (lane < 3 * N_EXPERTS)
    gate = jnp.sum(jnp.where(mine, pieces, 0.0), axis=-1, keepdims=True)
    chunk = 512
    acc = jnp.zeros((x.shape[0], D), F32)
    for j in range(ff // chunk):
        cols = slice(j * chunk, (j + 1) * chunk)
        g = _dot(x, wg_ref[0, 0, :, cols])
        u = _dot(x, wu_ref[0, 0, :, cols])
        hid = (g * jax.nn.sigmoid(g) * u).astype(BF16)
        acc = acc + _dot(hid, wd_ref[0, 0, cols, :])
    o_ref[0, 0] = (acc * gate).astype(BF16)


def _ffn(xin, wg, wu, wd, layer):
    B, E, S, W = xin.shape
    D, FF = wg.shape[2], wg.shape[3]
    return pl.pallas_call(
        _ffn_kernel,
        out_shape=jax.ShapeDtypeStruct((B, E, S, D), BF16),
        grid=(E, B),
        in_specs=[pl.BlockSpec((1, 1, S, W), lambda e, b: (b, e, 0, 0)),
                  pl.BlockSpec((1, 1, D, FF), lambda e, b: (layer, e, 0, 0)),
                  pl.BlockSpec((1, 1, D, FF), lambda e, b: (layer, e, 0, 0)),
                  pl.BlockSpec((1, 1, FF, D), lambda e, b: (layer, e, 0, 0))],
        out_specs=pl.BlockSpec((1, 1, S, D), lambda e, b: (b, e, 0, 0)),
        compiler_params=_params(2),
        name="expert_ffn",
    )(xin, wg, wu, wd)


def _combine_kernel(first_s, npass_s, slot_ref, first_ref, expand_ref, y_ref, x_ref, g2_ref,
                    *rest, n_tiles):
    o_ref = rest[-1]
    b, t = pl.program_id(0), pl.program_id(1)
    cap_total = y_ref.shape[2]
    o_ref[0] = x_ref[0]
    slot = slot_ref[0]
    first = first_ref[0, 0]
    lane_row = (lax.broadcasted_iota(jnp.int32, (TILE, N_EXPERTS * SLOT_WIN), 1)
                & (SLOT_WIN - 1)).astype(F32)

    def one_pass(k, carry):
        rel = _window_rel(slot, first, k.astype(F32), cap_total).astype(BF16)
        onehot = (_dot(rel, expand_ref[...]) == lane_row).astype(BF16)
        rows = jnp.concatenate(
            [y_ref[0, e, pl.ds(_window_begin(first_s[(b * n_tiles + t) * N_EXPERTS + e], k,
                                             cap_total), SLOT_WIN), :]
             for e in range(N_EXPERTS)], axis=0)
        o_ref[0] += g2_ref[0, 0] * _dot(onehot, rows)
        return carry

    lax.fori_loop(0, npass_s[b * n_tiles + t], one_pass, 0)
    if len(rest) == 2:
        x = o_ref[0]
        ms = jnp.mean(x * x, axis=-1, keepdims=True)
        o_ref[0] = x * lax.rsqrt(ms + RMS_EPS) * rest[0][...]


def _combine(y, slot, first, n_pass, xa, gate2, final_g=None):
    B, S, D = xa.shape
    nt = S // TILE
    E, cap_total = y.shape[1], y.shape[2]
    expand = np.repeat(np.eye(E, dtype=np.float32), SLOT_WIN, axis=1)
    in_specs = [pl.BlockSpec((1, TILE, E), lambda b, t, *_: (b, t, 0)),
                pl.BlockSpec((1, 1, 1, E), lambda b, t, *_: (b, t, 0, 0)),
                pl.BlockSpec((E, E * SLOT_WIN), lambda b, t, *_: (0, 0)),
                pl.BlockSpec((1, E, cap_total, D), lambda b, t, *_: (b, 0, 0, 0)),
                pl.BlockSpec((1, TILE, D), lambda b, t, *_: (b, t, 0)),
                pl.BlockSpec((1, 1, 1, D), lambda b, t, *_: (b, t // (nt - 1), 0, 0))]
    args = [slot, first.astype(F32).reshape(B, nt, 1, E), jnp.asarray(expand, BF16), y, xa, gate2]
    out_tiles = nt
    if final_g is not None:
        in_specs.append(pl.BlockSpec((1, D), lambda b, t, *_: (0, 0)))
        args.append(final_g)
        out_tiles = nt - 1
    grid_spec = pltpu.PrefetchScalarGridSpec(
        num_scalar_prefetch=2,
        grid=(B, out_tiles),
        in_specs=in_specs,
        out_specs=pl.BlockSpec((1, TILE, D), lambda b, t, *_: (b, t, 0)),
    )
    return pl.pallas_call(
        functools.partial(_combine_kernel, n_tiles=nt),
        out_shape=jax.ShapeDtypeStruct((B, out_tiles * TILE, D), F32),
        grid_spec=grid_spec,
        compiler_params=_params(2),
        name="combine",
    )(first.reshape(-1), n_pass.reshape(-1), *args)


def _scale_q(w, nq):
    return jnp.concatenate([w[:, :nq] * (HEAD_DIM ** -0.5 * LOG2E), w[:, nq:]], axis=1)


def _win_weights(w_in):
    D = w_in.shape[0]
    nq = 16 * HEAD_DIM
    nk = A_KV_HEADS * HEAD_DIM
    w = _scale_q(w_in, nq)
    dup = lambda m: jnp.concatenate([m.reshape(D, A_KV_HEADS, 1, HEAD_DIM)] * 2,
                                    axis=2).reshape(D, 2 * nk)
    return jnp.concatenate([w[:, :nq], dup(w[:, nq:nq + nk]), dup(w[:, nq + nk:])],
                           axis=1).astype(BF16)


def _router_weights(rw):
    D, E = rw.shape
    return jnp.concatenate([rw, rw, rw, jnp.zeros((D, GATE_COLS - 3 * E), rw.dtype)], axis=1)


def kernel(x, c, ctx, c_ctx, ada_w, ada_b, norm1_g, norm2_g, final_g, win_w_in, win_w_out,
           win_sink, diff_w_in, diff_w_out, diff_lambda, diff_subln_g, na_w_in, na_w_out,
           na_rpb, router_w, w_gate, w_up, w_down):
    B, T, D = x.shape
    L = ctx.shape[1]
    S = T + L
    depth = ada_w.shape[0]
    assert L == TILE and T % TILE == 0 and (T // GRID_W) >= NA_WIN_ROWS
    cs = jnp.concatenate([c, c_ctx[None, :]], axis=0)
    rope = _rope_tables(T, L)
    wg_all, wu_all, wd_all = w_gate.astype(BF16), w_up.astype(BF16), w_down.astype(BF16)
    xa = None
    for i in range(depth):
        kind = i % N_MIXERS
        slot = i // N_MIXERS
        mod = _ada(cs, ada_w[i], ada_b[i][None, :])
        mod = jnp.stack([mod[:B], jnp.broadcast_to(mod[B:], (B, 6 * D))], axis=1)
        mod = mod.reshape(B, 2, 1, 6, D)
        sh1, sc1, g1, sh2, sc2, g2 = [mod[:, :, :, k, :] for k in range(6)]
        if kind == 0:
            w_in = _win_weights(win_w_in[slot])
            n_rope = (16 + 2 * A_KV_HEADS) * HEAD_DIM
            w_out = win_w_out[slot]
        elif kind == 1:
            w_in = _scale_q(diff_w_in[slot], 16 * HEAD_DIM).astype(BF16)
            n_rope = 32 * HEAD_DIM
            w_out = diff_w_out[slot]
        else:
            w_in = _scale_q(na_w_in[slot], 16 * HEAD_DIM).astype(BF16)
            n_rope = 0
            w_out = na_w_out[slot]
        if i == 0:
            qkv, xa = _norm_proj((x, ctx), norm1_g[i][None, :], sh1, sc1, w_in, rope, n_rope)
        else:
            qkv = _norm_proj((xa,), norm1_g[i][None, :], sh1, sc1, w_in, rope, n_rope)
        if kind == 0:
            y = _attn_win(qkv, win_sink[slot], T)
        elif kind == 1:
            lambda_init = 0.8 - 0.6 * math.exp(-0.3 * i)
            lp = diff_lambda[slot]
            lam = (jnp.exp(jnp.sum(lp[0] * lp[1])) - jnp.exp(jnp.sum(lp[2] * lp[3]))
                   + lambda_init).reshape(1)
            y = _attn_diff(qkv, lam, diff_subln_g[slot][None, :], T, 1.0 - lambda_init)
        else:
            y = _attn_na(qkv, _na_bias_tables(na_rpb[slot], T), _na_bias_bound(na_rpb[slot]), T)
        xa, hx, aff = _out_router(y, w_out.astype(BF16), xa, g1, norm2_g[i][None, :],
                                  sh2, sc2, _router_weights(router_w[i]))
        tok_slot, start, cnt = _route(aff, T)
        first, n_pass = _slot_plan(start, cnt)
        cap_total = CAPACITY_FACTOR * S // N_EXPERTS
        xin = _dispatch(hx, tok_slot, first, n_pass, cap_total)
        ye = _ffn(xin, wg_all, wu_all, wd_all, i)
        last = i == depth - 1
        xa = _combine(ye, tok_slot, first, n_pass, xa, g2, final_g[None, :] if last else None)
    return xa
```

```python
import functools
import math

import numpy as np
import jax
import jax.numpy as jnp
from jax import lax
from jax.experimental import pallas as pl
from jax.experimental.pallas import tpu as pltpu

HEAD_DIM = 64
LANES = 128
GRID_W = 64
NA_ROWS = 8
NA_COLS = 16
A_WINDOW = 128
A_KV_HEADS = 4
N_MIXERS = 3
ROPE_BASE = 10000.0
ROPE_AXIS_DIM = HEAD_DIM // 2
N_EXPERTS = 16
CAPACITY_FACTOR = 2
RMS_EPS = 1e-6
NEG_INF = -1e30
LOG2E = math.log2(math.e)
TILE = 256
ROW_TILE = 2 * TILE
NA_TILE_ROWS = TILE // GRID_W
NA_WIN_ROWS = NA_TILE_ROWS + NA_ROWS
VMEM_LIMIT = 56 * 1024 * 1024
SLOT_WIN = 64
SLOT_ALIGN = 16
GATE_COLS = LANES

BF16 = jnp.bfloat16
F32 = jnp.float32


def _params(n_grid):
    return pltpu.CompilerParams(
        dimension_semantics=("arbitrary",) * n_grid, vmem_limit_bytes=VMEM_LIMIT)


def _split_bf16(a):
    hi = a.astype(BF16)
    lo = (a - hi.astype(F32)).astype(BF16)
    return hi, lo


def _dot(a, b):
    return jnp.dot(a, b, preferred_element_type=F32)


def _dot_nt(a, b):
    return lax.dot_general(a, b, (((1,), (1,)), ((), ())), preferred_element_type=F32)


def _dot_tn(a, b):
    return lax.dot_general(a, b, (((0,), (0,)), ((), ())), preferred_element_type=F32)


def _ada_kernel(c_ref, w_ref, b_ref, o_ref):
    c = c_ref[...]
    a = c * jax.nn.sigmoid(c)
    a_hi, a_lo = _split_bf16(a)
    w_hi, w_lo = _split_bf16(w_ref[...])
    o_ref[...] = _dot(a_hi, w_hi) + _dot(a_lo, w_hi) + _dot(a_hi, w_lo) + b_ref[...]


def _ada(cs, w, b):
    R, D = cs.shape
    N = w.shape[1]
    tn = 1024
    return pl.pallas_call(
        _ada_kernel,
        out_shape=jax.ShapeDtypeStruct((R, N), F32),
        grid=(N // tn,),
        in_specs=[pl.BlockSpec((R, D), lambda j: (0, 0)),
                  pl.BlockSpec((D, tn), lambda j: (0, j)),
                  pl.BlockSpec((1, tn), lambda j: (0, j))],
        out_specs=pl.BlockSpec((R, tn), lambda j: (0, j)),
        compiler_params=_params(1),
        name="ada",
    )(cs, w, b)


def _rms_mod(x, g, shift, scale):
    ms = jnp.mean(x * x, axis=-1, keepdims=True)
    y = x * lax.rsqrt(ms + RMS_EPS) * g
    return y * (1.0 + scale) + shift


def _norm_proj_kernel(*refs, n_rope, joins_streams):
    if joins_streams:
        x_ref, c_ref, g_ref, sh_ref, sc_ref, w_ref, cos_ref, sa_ref, sb_ref, o_ref, xa_ref = refs
        is_ctx = pl.program_id(1) == pl.num_programs(1) - 1
        ctx_rows = jnp.concatenate([c_ref[0]] * (ROW_TILE // TILE), axis=0)
        x = jnp.where(is_ctx, ctx_rows, x_ref[0])
        xa_ref[0] = x
    else:
        x_ref, g_ref, sh_ref, sc_ref, w_ref, cos_ref, sa_ref, sb_ref, o_ref = refs
        x = x_ref[0]
    h = _rms_mod(x, g_ref[...], sh_ref[0, 0], sc_ref[0, 0]).astype(BF16)
    n_cols = w_ref.shape[1]
    chunk = 512
    for j in range(n_cols // chunk):
        acc = _dot(h, w_ref[:, j * chunk:(j + 1) * chunk])
        for t in range(chunk // LANES):
            col = j * chunk + t * LANES
            a = acc[:, t * LANES:(t + 1) * LANES]
            if col < n_rope:
                a = (a * cos_ref[...]
                     + pltpu.roll(a, LANES - 16, 1) * sa_ref[...]
                     + pltpu.roll(a, 16, 1) * sb_ref[...])
            o_ref[0, :, col:col + LANES] = a.astype(BF16)


def _norm_proj(streams, g, shift, scale, w, rope, n_rope):
    joins = len(streams) == 2
    B, _, D = streams[0].shape
    S = sum(a.shape[1] for a in streams)
    N = w.shape[1]
    nt = pl.cdiv(S, ROW_TILE)
    mod_spec = pl.BlockSpec((1, 1, 1, D), lambda b, t: (b, t // (nt - 1), 0, 0))
    rope_spec = pl.BlockSpec((ROW_TILE, LANES), lambda b, t: (t, 0))
    tile_spec = pl.BlockSpec((1, ROW_TILE, D), lambda b, t: (b, t, 0))
    if joins:
        stream_specs = [pl.BlockSpec((1, ROW_TILE, D), lambda b, t: (b, jnp.minimum(t, nt - 2), 0)),
                        pl.BlockSpec((1, TILE, D), lambda b, t: (b, 0, 0))]
    else:
        stream_specs = [tile_spec]
    qkv_shape = jax.ShapeDtypeStruct((B, S, N), BF16)
    qkv_spec = pl.BlockSpec((1, ROW_TILE, N), lambda b, t: (b, t, 0))
    return pl.pallas_call(
        functools.partial(_norm_proj_kernel, n_rope=n_rope, joins_streams=joins),
        out_shape=(qkv_shape, jax.ShapeDtypeStruct((B, S, D), F32)) if joins else qkv_shape,
        grid=(B, nt),
        in_specs=stream_specs + [pl.BlockSpec((1, D), lambda b, t: (0, 0)),
                                 mod_spec, mod_spec,
                                 pl.BlockSpec((D, N), lambda b, t: (0, 0)),
                                 rope_spec, rope_spec, rope_spec],
        out_specs=(qkv_spec, tile_spec) if joins else qkv_spec,
        compiler_params=_params(2),
        name="norm_proj",
    )(*streams, g, shift, scale, w, *rope)


def _rope_tables(T, L):
    t = np.arange(T)
    pos = np.stack([t // GRID_W, t % GRID_W], axis=0).astype(np.float32)
    inv = (1.0 / (ROPE_BASE ** (np.arange(0, ROPE_AXIS_DIM, 2, dtype=np.float32)
                                / ROPE_AXIS_DIM))).astype(np.float32)
    d = np.arange(LANES) % HEAD_DIM
    axis = d // ROPE_AXIS_DIM
    half = (d % ROPE_AXIS_DIM) // (ROPE_AXIS_DIM // 2)
    freq = d % (ROPE_AXIS_DIM // 2)
    ang = jnp.asarray(pos[axis].T) * jnp.asarray(inv[freq])[None, :]
    cos = jnp.cos(ang)
    sin = jnp.sin(ang)
    first = jnp.asarray(half == 0)[None, :]
    sa = jnp.where(first, -sin, 0.0)
    sb = jnp.where(first, 0.0, sin)
    pad = lambda a, v: jnp.concatenate([a, jnp.full((L, LANES), v, F32)], axis=0)
    return pad(cos, 1.0), pad(sa, 0.0), pad(sb, 0.0)


def _half_masks(q):
    lane = lax.broadcasted_iota(jnp.int32, q.shape, 1)
    zero = jnp.zeros_like(q)
    return jnp.where(lane < HEAD_DIM, q, zero), jnp.where(lane >= HEAD_DIM, q, zero)


def _merge_halves(o_first, o_second):
    lane = lax.broadcasted_iota(jnp.int32, o_first.shape, 1)
    return jnp.where(lane < HEAD_DIM, o_first, o_second)


SHIFT_SLACK = 1.02
L_MIN = 2.0 ** -64


def _sq_norms_row(x):
    xf = x.astype(F32)
    return _dot_nt(jnp.ones((8, LANES), BF16), (xf * xf).astype(BF16))[:1]


def _head_bounds_row(qs):
    qf = qs.astype(F32)
    n2 = jnp.sum(qf * qf, axis=-1, keepdims=True)
    return jnp.concatenate(
        [jnp.broadcast_to(jnp.max(n2[r:r + TILE], axis=0, keepdims=True), (1, TILE))
         for r in range(0, qs.shape[0], TILE)], axis=1)


def _store_key_bound(kmax_ref, j, k, scale=1.0):
    n2 = jnp.max(_sq_norms_row(k), axis=1, keepdims=True) * scale
    kmax_ref[j] = jnp.broadcast_to(n2, kmax_ref.shape[1:])


def _redo_if_underflow(attend, *args):
    den_min = attend(*args, exact=False)

    @pl.when(den_min[0, 0] < L_MIN)
    def _():
        attend(*args, exact=True)


WIN_KV_PER_STEP = 4


def _win_kernel(sink_ref, q_ref, k_ref, v_ref, o_ref, kmax_ref, *, T, win):
    hb = pl.program_id(1)
    t = pl.program_id(2)
    n_x = T // TILE
    is_x = t < n_x
    heads = [slice(j * LANES, (j + 1) * LANES) for j in range(WIN_KV_PER_STEP)]

    @pl.when(t == 0)
    def _():
        for j, cols in enumerate(heads):
            _store_key_bound(kmax_ref, j, k_ref[0, :, cols], 0.5)

    start = pl.multiple_of(jnp.clip(t * TILE - A_WINDOW, 0, T - win), LANES)
    kpos = start + lax.broadcasted_iota(jnp.int32, (win, TILE), 0)
    qpos = t * TILE + lax.broadcasted_iota(jnp.int32, (win, TILE), 1)
    band = (jnp.abs(qpos - kpos) <= A_WINDOW) & is_x
    band = jnp.concatenate([band] * 4, axis=1)

    def attend(exact):
        queries, scores = [], []
        for j, cols in enumerate(heads):
            q = q_ref[0, :, 2 * j * LANES:2 * (j + 1) * LANES]
            qa, qb = _half_masks(q[:, :LANES])
            qc, qd = _half_masks(q[:, LANES:])
            qs = jnp.concatenate([qa, qb, qc, qd], axis=0)
            queries.append(qs)
            scores.append((_dot_nt(k_ref[0, T:, cols], qs),
                           _dot_nt(k_ref[0, pl.ds(start, win), cols], qs)))
        den_min = None
        for j, cols in enumerate(heads):
            s_c, s_w = scores[j]
            head0 = (hb * WIN_KV_PER_STEP + j) * 4
            s_w = jnp.where(band, s_w, NEG_INF)
            sink = jnp.concatenate(
                [jnp.full((1, TILE), sink_ref[head0 + g] * LOG2E, F32) for g in range(4)], axis=1)
            if exact:
                m = jnp.maximum(jnp.max(s_c, axis=0, keepdims=True),
                                jnp.max(s_w, axis=0, keepdims=True))
            else:
                m = jnp.sqrt(_head_bounds_row(queries[j]) * kmax_ref[j, :1, :1]) * SHIFT_SLACK
            m = jnp.maximum(m, sink)
            ec = jnp.exp2(s_c - m)
            ew = jnp.exp2(s_w - m)
            den = (jnp.sum(ec, axis=0, keepdims=True) + jnp.sum(ew, axis=0, keepdims=True)
                   + jnp.exp2(sink - m))
            o = ((_dot_tn(v_ref[0, T:, cols], ec.astype(BF16))
                  + _dot_tn(v_ref[0, pl.ds(start, win), cols], ew.astype(BF16))) * (1.0 / den)).T
            first = 2 * j * LANES
            o_ref[0, :, first:first + LANES] = _merge_halves(
                o[:TILE], o[TILE:2 * TILE]).astype(BF16)
            o_ref[0, :, first + LANES:first + 2 * LANES] = _merge_halves(
                o[2 * TILE:3 * TILE], o[3 * TILE:]).astype(BF16)
            low = jnp.min(den, axis=1, keepdims=True)
            den_min = low if den_min is None else jnp.minimum(den_min, low)
        return den_min

    _redo_if_underflow(attend)


def _attn_win(qkv, sink, T):
    B, S, _ = qkv.shape
    nt = S // TILE
    nq = 16 * HEAD_DIM
    win = TILE + 2 * A_WINDOW
    hb = A_KV_HEADS // WIN_KV_PER_STEP
    wq = WIN_KV_PER_STEP * 2 * LANES
    wk = WIN_KV_PER_STEP * LANES
    kb = nq // wk
    return pl.pallas_call(
        functools.partial(_win_kernel, T=T, win=win),
        out_shape=jax.ShapeDtypeStruct((B, S, nq), BF16),
        grid=(B, hb, nt),
        in_specs=[pl.BlockSpec(memory_space=pltpu.SMEM),
                  pl.BlockSpec((1, TILE, wq), lambda b, h, t: (b, t, h)),
                  pl.BlockSpec((1, S, wk), lambda b, h, t: (b, 0, kb + h)),
                  pl.BlockSpec((1, S, wk), lambda b, h, t: (b, 0, kb + hb + h))],
        out_specs=pl.BlockSpec((1, TILE, wq), lambda b, h, t: (b, t, h)),
        scratch_shapes=[pltpu.VMEM((WIN_KV_PER_STEP, 8, LANES), F32)],
        compiler_params=_params(3),
        name="attn_win",
    )(sink, qkv, qkv, qkv)


DIFF_HEADS_PER_STEP = 2


def _diff_kernel(lam_ref, q_ref, k_ref, v_ref, g_ref, o_ref, kmax_ref, *, T, out_scale):
    t = pl.program_id(2)
    n_x = T // TILE
    lam = lam_ref[0]
    heads = [slice(j * LANES, (j + 1) * LANES) for j in range(DIFF_HEADS_PER_STEP)]

    @pl.when(t == 0)
    def _():
        for j, cols in enumerate(heads):
            k1, k2 = _half_masks(k_ref[0, :, cols])
            _store_key_bound(kmax_ref, 2 * j, k1)
            _store_key_bound(kmax_ref, 2 * j + 1, k2)

    def attend(keys, exact):
        queries, scores = [], []
        for cols in heads:
            qa, qb = _half_masks(q_ref[0, :, cols])
            qs = jnp.concatenate([qa, qb], axis=0)
            queries.append(qs)
            scores.append(_dot_nt(qs, k_ref[0, keys, cols]))
        l_min = None
        for j, cols in enumerate(heads):
            s = scores[j]
            if exact:
                m = jnp.max(s, axis=-1, keepdims=True)
            else:
                qf = queries[j].astype(F32)
                row = lax.broadcasted_iota(jnp.int32, (2 * TILE, 1), 0)
                kmax = jnp.where(row < TILE, kmax_ref[2 * j, :1, :1], kmax_ref[2 * j + 1, :1, :1])
                m = jnp.sqrt(jnp.sum(qf * qf, axis=-1, keepdims=True) * kmax) * SHIFT_SLACK
            e = jnp.exp2(s - m)
            l = jnp.sum(e, axis=-1, keepdims=True)
            a = e[:TILE] - e[TILE:] * (lam * l[:TILE] / l[TILE:])
            o = _dot(a.astype(BF16), v_ref[0, keys, cols]) / l[:TILE]
            ms = jnp.mean(o * o, axis=-1, keepdims=True)
            y = o * lax.rsqrt(ms + RMS_EPS) * g_ref[...] * out_scale
            o_ref[0, :, cols] = y.astype(BF16)
            low = jnp.min(l, axis=0, keepdims=True)
            l_min = low if l_min is None else jnp.minimum(l_min, low)
        return l_min

    @pl.when(t < n_x)
    def _():
        _redo_if_underflow(attend, slice(None))

    @pl.when(t >= n_x)
    def _():
        _redo_if_underflow(attend, slice(T, None))


def _attn_diff(qkv, lam, subln_g, T, out_scale):
    B, S, _ = qkv.shape
    nt = S // TILE
    H = 8
    hb = H // DIFF_HEADS_PER_STEP
    w = DIFF_HEADS_PER_STEP * LANES
    return pl.pallas_call(
        functools.partial(_diff_kernel, T=T, out_scale=out_scale),
        out_shape=jax.ShapeDtypeStruct((B, S, H * LANES), BF16),
        grid=(B, hb, nt),
        in_specs=[pl.BlockSpec(memory_space=pltpu.SMEM),
                  pl.BlockSpec((1, TILE, w), lambda b, h, t: (b, t, h)),
                  pl.BlockSpec((1, S, w), lambda b, h, t: (b, 0, hb + h)),
                  pl.BlockSpec((1, S, w), lambda b, h, t: (b, 0, 2 * hb + h)),
                  pl.BlockSpec((1, LANES), lambda b, h, t: (0, 0))],
        out_specs=pl.BlockSpec((1, TILE, w), lambda b, h, t: (b, t, h)),
        scratch_shapes=[pltpu.VMEM((2 * DIFF_HEADS_PER_STEP, 8, LANES), F32)],
        compiler_params=_params(3),
        name="attn_diff",
    )(lam, qkv, qkv, qkv, subln_g)


NA_PAIRS_PER_STEP = 2


def _na_kernel(q_ref, k_ref, v_ref, bias_ref, bmax_ref, o_ref, kmax_ref, *, T):
    t = pl.program_id(2)
    n_x = T // TILE
    rows = T // GRID_W
    win = NA_WIN_ROWS * GRID_W
    pairs = [slice(j * LANES, (j + 1) * LANES) for j in range(NA_PAIRS_PER_STEP)]

    @pl.when(t == 0)
    def _():
        for j, cols in enumerate(pairs):
            k1, k2 = _half_masks(k_ref[0, :, cols])
            _store_key_bound(kmax_ref, 2 * j, k1)
            _store_key_bound(kmax_ref, 2 * j + 1, k2)

    cls = jnp.where(t >= n_x, 3, jnp.where(t == 0, 0, jnp.where(t == n_x - 1, 2, 1)))
    row0 = jnp.clip(t * NA_TILE_ROWS - NA_ROWS // 2, 0, rows - NA_WIN_ROWS)
    start = pl.multiple_of(row0 * GRID_W, GRID_W)

    def attend(exact):
        queries, scores = [], []
        for cols in pairs:
            qa, qb = _half_masks(q_ref[0, :, cols])
            qs = jnp.concatenate([qa, qb], axis=0)
            queries.append(qs)
            scores.append((_dot_nt(k_ref[0, T:, cols], qs),
                           _dot_nt(k_ref[0, pl.ds(start, win), cols], qs)))
        den_min = None
        for j, cols in enumerate(pairs):
            s_c, s_w = scores[j]
            s_w = s_w + bias_ref[cls, j]
            if exact:
                m = jnp.maximum(jnp.max(s_c, axis=0, keepdims=True),
                                jnp.max(s_w, axis=0, keepdims=True))
            else:
                lane = lax.broadcasted_iota(jnp.int32, (1, 2 * TILE), 1)
                kmax = jnp.where(lane < TILE, kmax_ref[2 * j, :1, :1], kmax_ref[2 * j + 1, :1, :1])
                m = jnp.sqrt(_head_bounds_row(queries[j]) * kmax) * SHIFT_SLACK + bmax_ref[j]
            ec = jnp.exp2(s_c - m)
            ew = jnp.exp2(s_w - m)
            den = jnp.sum(ec, axis=0, keepdims=True) + jnp.sum(ew, axis=0, keepdims=True)
            o = ((_dot_tn(v_ref[0, T:, cols], ec.astype(BF16))
                  + _dot_tn(v_ref[0, pl.ds(start, win), cols], ew.astype(BF16))) * (1.0 / den)).T
            o_ref[0, :, cols] = _merge_halves(o[:TILE], o[TILE:]).astype(BF16)
            low = jnp.min(den, axis=1, keepdims=True)
            den_min = low if den_min is None else jnp.minimum(den_min, low)
        return den_min

    _redo_if_underflow(attend)


def _attn_na(qkv, bias, bias_max, T):
    B, S, _ = qkv.shape
    nt = S // TILE
    n = NA_PAIRS_PER_STEP
    hb = 8 // n
    w = n * LANES
    win = NA_WIN_ROWS * GRID_W
    return pl.pallas_call(
        functools.partial(_na_kernel, T=T),
        out_shape=jax.ShapeDtypeStruct((B, S, 8 * LANES), BF16),
        grid=(hb, B, nt),
        in_specs=[pl.BlockSpec((1, TILE, w), lambda h, b, t: (b, t, h)),
                  pl.BlockSpec((1, S, w), lambda h, b, t: (b, 0, hb + h)),
                  pl.BlockSpec((1, S, w), lambda h, b, t: (b, 0, 2 * hb + h)),
                  pl.BlockSpec((4, n, win, 2 * TILE), lambda h, b, t: (0, h, 0, 0)),
                  pl.BlockSpec((n, 1, 2 * TILE), lambda h, b, t: (h, 0, 0))],
        out_specs=pl.BlockSpec((1, TILE, w), lambda h, b, t: (b, t, h)),
        scratch_shapes=[pltpu.VMEM((2 * NA_PAIRS_PER_STEP, 8, LANES), F32)],
        compiler_params=_params(3),
        name="attn_na",
    )(qkv, qkv, qkv, bias, bias_max)


def _na_bias_bound(rpb):
    top = jnp.maximum(jnp.max(rpb, axis=(1, 2)), 0.0) * LOG2E
    return jnp.repeat(top.reshape(-1, 2), TILE, axis=1)[:, None, :]


def _na_bias_tables(rpb, T):
    rows = T // GRID_W
    H = rpb.shape[0]
    win = NA_WIN_ROWS * GRID_W
    hi = lax.Precision.HIGHEST
    pick = lambda idx, n: jnp.asarray(idx[..., None] == np.arange(n), F32)
    c = np.arange(GRID_W)
    cs = np.clip(c - NA_COLS // 2, 0, GRID_W - NA_COLS)
    valid_c = (c[None, :] >= cs[:, None]) & (c[None, :] < cs[:, None] + NA_COLS)
    bidx_c = np.clip(c[None, :] - c[:, None] + NA_COLS - 1, 0, 2 * NA_COLS - 2)
    by_col = jnp.einsum('hrd,ckd->hrck', rpb, pick(bidx_c, 2 * NA_COLS - 1), precision=hi)
    i = np.arange(NA_TILE_ROWS)
    j = np.arange(NA_WIN_ROWS)
    tables = []
    for r0 in (0, NA_TILE_ROWS, rows - NA_TILE_ROWS):
        s = int(np.clip(r0 - NA_ROWS // 2, 0, rows - NA_WIN_ROWS))
        r = r0 + i
        rs = np.clip(r - NA_ROWS // 2, 0, rows - NA_ROWS)
        kr = s + j
        valid_r = (kr[None, :] >= rs[:, None]) & (kr[None, :] < rs[:, None] + NA_ROWS)
        bidx_r = np.clip(kr[None, :] - r[:, None] + NA_ROWS - 1, 0, 2 * NA_ROWS - 2)
        vals = jnp.einsum('hrck,ijr->hjkic', by_col, pick(bidx_r, 2 * NA_ROWS - 1), precision=hi)
        valid = valid_r.T[:, None, :, None] & valid_c.T[None, :, None, :]
        vals = jnp.where(jnp.asarray(valid)[None], vals, NEG_INF).reshape(H // 2, 2, win, TILE)
        tables.append(vals.transpose(0, 2, 1, 3).reshape(H // 2, win, 2 * TILE))
    tables.append(jnp.full_like(tables[0], NEG_INF))
    return jnp.stack(tables, axis=0).astype(F32) * LOG2E


def _out_router_kernel(y_ref, w_ref, x_ref, g1_ref, n2_ref, sh_ref, sc_ref, rw_ref,
                       xo_ref, h_ref, aff_ref):
    D = x_ref.shape[2]
    x = x_ref[0] + g1_ref[0, 0] * _dot(y_ref[0], w_ref[...])
    xo_ref[0] = x
    h = _rms_mod(x, n2_ref[...], sh_ref[0, 0], sc_ref[0, 0])
    h_hi, h_lo = _split_bf16(h)
    n = h.shape[0]
    r_hi_lo = jnp.concatenate(_split_bf16(rw_ref[...]), axis=1)
    prod = _dot(jnp.concatenate([h_hi, h_lo], axis=0), r_hi_lo)
    logits = prod[:n, :GATE_COLS] + prod[:n, GATE_COLS:] + prod[n:, :GATE_COLS]
    lane = lax.broadcasted_iota(jnp.int32, logits.shape, 1)
    first = lane < N_EXPERTS
    m = jnp.max(jnp.where(first, logits, -jnp.inf), axis=-1, keepdims=True)
    e = jnp.exp(logits - m)
    aff = e / jnp.sum(jnp.where(first, e, 0.0), axis=-1, keepdims=True)
    aff_ref[0] = aff
    hi = aff.astype(BF16)
    rem = aff - hi.astype(F32)
    mid = rem.astype(BF16)
    lo = (rem - mid.astype(F32)).astype(BF16)
    zero = jnp.zeros_like(hi)
    pieces = jnp.where(first, hi, jnp.where(lane < 2 * N_EXPERTS, mid,
                                            jnp.where(lane < 3 * N_EXPERTS, lo, zero)))
    h_ref[0, :, :D] = h_hi
    h_ref[0, :, D:] = pieces


def _out_router(y, w_out, xa, gate1, n2g, shift2, scale2, rw3):
    B, S, D = xa.shape
    nt = pl.cdiv(S, ROW_TILE)
    mod_spec = pl.BlockSpec((1, 1, 1, D), lambda b, t: (b, t // (nt - 1), 0, 0))
    tile_spec = pl.BlockSpec((1, ROW_TILE, D), lambda b, t: (b, t, 0))
    return pl.pallas_call(
        _out_router_kernel,
        out_shape=(jax.ShapeDtypeStruct((B, S, D), F32),
                   jax.ShapeDtypeStruct((B, S, D + GATE_COLS), BF16),
                   jax.ShapeDtypeStruct((B, S, GATE_COLS), F32)),
        grid=(B, nt),
        in_specs=[tile_spec,
                  pl.BlockSpec((D, D), lambda b, t: (0, 0)),
                  tile_spec, mod_spec,
                  pl.BlockSpec((1, D), lambda b, t: (0, 0)),
                  mod_spec, mod_spec,
                  pl.BlockSpec((D, GATE_COLS), lambda b, t: (0, 0))],
        out_specs=(tile_spec,
                   pl.BlockSpec((1, ROW_TILE, D + GATE_COLS), lambda b, t: (b, t, 0)),
                   pl.BlockSpec((1, ROW_TILE, GATE_COLS), lambda b, t: (b, t, 0))),
        compiler_params=_params(2),
        name="out_router",
    )(y, w_out, xa, gate1, n2g, shift2, scale2, rw3)


def _route_kernel(aff_ref, slot_ref, start_ref, cnt_ref, *, T, cap_x, cap_c):
    S = aff_ref.shape[1]
    aff = aff_ref[0]
    bits = lax.bitcast_convert_type(aff[:, :N_EXPERTS], jnp.int32)
    ri = lax.broadcasted_iota(jnp.int32, (TILE, TILE), 0)
    ci = lax.broadcasted_iota(jnp.int32, (TILE, TILE), 1)
    ltri = (ri > ci).astype(BF16)
    eye = (lax.broadcasted_iota(jnp.int32, (N_EXPERTS, N_EXPERTS), 0)
           == lax.broadcasted_iota(jnp.int32, (N_EXPERTS, N_EXPERTS), 1))

    def count(mask):
        return jnp.sum(mask.astype(F32), axis=0, keepdims=True)

    for lo_row, hi_row, cap, base in ((0, T, cap_x, 0), (T, S, cap_c, cap_x)):
        b = bits[lo_row:hi_row]
        dense = lax.bitcast_convert_type(jnp.concatenate(
            [aff[r:r + TILE].T[:N_EXPERTS] for r in range(lo_row, hi_row, TILE)], axis=1),
            jnp.int32)

        def step(i, thr):
            cand = thr | lax.shift_left(jnp.int32(1), 30 - i)
            n_ge = jnp.sum((dense >= cand).astype(F32), axis=1, keepdims=True)
            return jnp.where(n_ge >= cap, cand, thr)

        thr = lax.fori_loop(0, 31, step, jnp.zeros((N_EXPERTS, 1), jnp.int32))
        thr = jnp.max(jnp.where(eye, jnp.broadcast_to(thr, eye.shape), 0), axis=0,
                      keepdims=True)
        need = cap - count(b > thr)
        seen_eq = jnp.zeros((1, N_EXPERTS), F32)
        seen = jnp.zeros((1, N_EXPERTS), F32)
        for j in range((hi_row - lo_row) // TILE):
            blk = b[j * TILE:(j + 1) * TILE]
            gt = blk > thr
            eq = blk == thr
            eq_rank = _dot(ltri, eq.astype(BF16)) + seen_eq
            sel = gt | (eq & (eq_rank < need))
            pos = _dot(ltri, sel.astype(BF16)) + seen + base
            t = lo_row // TILE + j
            slot_ref[0, t * TILE:(t + 1) * TILE, :] = jnp.where(sel, pos, -1.0)
            n_sel = count(sel)
            start_ref[0, t:t + 1, :] = seen + base
            cnt_ref[0, t:t + 1, :] = n_sel
            seen_eq = seen_eq + count(eq)
            seen = seen + n_sel


def _route(aff, T):
    B, S, _ = aff.shape
    E = N_EXPERTS
    nt = S // TILE
    cap_x = CAPACITY_FACTOR * T // N_EXPERTS
    cap_c = CAPACITY_FACTOR * (S - T) // N_EXPERTS
    plan = jax.ShapeDtypeStruct((B, nt, E), F32)
    return pl.pallas_call(
        functools.partial(_route_kernel, T=T, cap_x=cap_x, cap_c=cap_c),
        out_shape=(jax.ShapeDtypeStruct((B, S, E), F32), plan, plan),
        grid=(B,),
        in_specs=[pl.BlockSpec((1, S, GATE_COLS), lambda b: (b, 0, 0))],
        out_specs=(pl.BlockSpec((1, S, E), lambda b: (b, 0, 0)),
                   pl.BlockSpec((1, nt, E), lambda b: (b, 0, 0)),
                   pl.BlockSpec((1, nt, E), lambda b: (b, 0, 0))),
        compiler_params=_params(1),
        name="route",
    )(aff)


def _slot_plan(start, cnt):
    start = start.astype(jnp.int32)
    cnt = cnt.astype(jnp.int32)
    first = (start // SLOT_ALIGN) * SLOT_ALIGN
    n_pass = jnp.max((start - first + cnt + SLOT_WIN - 1) // SLOT_WIN, axis=-1)
    return first, n_pass.astype(jnp.int32)


def _window_rel(slot, first, k, cap_total):
    nominal = first + k * SLOT_WIN
    begin = jnp.minimum(nominal, float(cap_total - SLOT_WIN))
    rel = slot - nominal
    return jnp.where((rel >= 0) & (rel < SLOT_WIN), rel + (nominal - begin), 255.0)


def _window_begin(first_s, k, cap_total):
    return pl.multiple_of(jnp.minimum(first_s + k * SLOT_WIN, cap_total - SLOT_WIN), SLOT_ALIGN)


def _dispatch_kernel(first_s, npass_s, slot_ref, first_ref, h_ref, o_ref):
    b, half, t = pl.program_id(0), pl.program_id(1), pl.program_id(2)
    nt = pl.num_programs(2)
    n_e = o_ref.shape[1]
    cap_total = o_ref.shape[2]

    @pl.when(t == 0)
    def _():
        o_ref[...] = jnp.zeros_like(o_ref)

    slot = slot_ref[0]
    first = first_ref[0, 0]
    pick_r = lax.broadcasted_iota(jnp.int32, (n_e, N_EXPERTS), 0)
    pick_c = lax.broadcasted_iota(jnp.int32, (n_e, N_EXPERTS), 1)
    pick = (pick_c == pick_r + half * n_e).astype(BF16)
    sub = lax.broadcasted_iota(jnp.int32, (SLOT_WIN, TILE), 0).astype(F32)

    def one_pass(k, carry):
        rel = _window_rel(slot, first, k.astype(F32), cap_total).astype(BF16)
        rel_t = _dot_nt(pick, rel)
        onehot = jnp.concatenate(
            [(jnp.broadcast_to(rel_t[e:e + 1, :], (SLOT_WIN, TILE)) == sub).astype(BF16)
             for e in range(n_e)], axis=0)
        rows = _dot(onehot, h_ref[0])
        for e in range(n_e):
            base = (b * nt + t) * N_EXPERTS + half * n_e + e
            win = pl.ds(_window_begin(first_s[base], k, cap_total), SLOT_WIN)
            cur = o_ref[0, e, win, :].astype(F32)
            o_ref[0, e, win, :] = (cur + rows[e * SLOT_WIN:(e + 1) * SLOT_WIN]).astype(BF16)
        return carry

    lax.fori_loop(0, npass_s[b * nt + t], one_pass, 0)


def _dispatch(hx, slot, first, n_pass, cap_total):
    B, S, W = hx.shape
    nt = S // TILE
    halves = 1
    n_e = N_EXPERTS // halves
    grid_spec = pltpu.PrefetchScalarGridSpec(
        num_scalar_prefetch=2,
        grid=(B, halves, nt),
        in_specs=[pl.BlockSpec((1, TILE, N_EXPERTS), lambda b, h, t, *_: (b, t, 0)),
                  pl.BlockSpec((1, 1, 1, N_EXPERTS), lambda b, h, t, *_: (b, t, 0, 0)),
                  pl.BlockSpec((1, TILE, W), lambda b, h, t, *_: (b, t, 0))],
        out_specs=pl.BlockSpec((1, n_e, cap_total, W), lambda b, h, t, *_: (b, h, 0, 0)),
    )
    return pl.pallas_call(
        _dispatch_kernel,
        out_shape=jax.ShapeDtypeStruct((B, N_EXPERTS, cap_total, W), BF16),
        grid_spec=grid_spec,
        compiler_params=_params(3),
        name="dispatch",
    )(first.reshape(-1), n_pass.reshape(-1), slot,
      first.astype(F32).reshape(B, nt, 1, N_EXPERTS), hx)


def _ffn_kernel(x_ref, wg_ref, wu_ref, wd_ref, o_ref):
    D = wg_ref.shape[2]
    ff = wg_ref.shape[3]
    x = x_ref[0, 0, :, :D]
    pieces = x_ref[0, 0, :, D:].astype(F32)
    lane = lax.broadcasted_iota(jnp.int32, pieces.shape, 1)
    mine = ((lane & (N_EXPERTS - 1)) == pl.program_id(0)) & (lane < 3 * N_EXPERTS)
    gate = jnp.sum(jnp.where(mine, pieces, 0.0), axis=-1, keepdims=True)
    chunk = 512
    acc = jnp.zeros((x.shape[0], D), F32)
    for j in range(ff // chunk):
        cols = slice(j * chunk, (j + 1) * chunk)
        g = _dot(x, wg_ref[0, 0, :, cols])
        u = _dot(x, wu_ref[0, 0, :, cols])
        hid = (g * jax.nn.sigmoid(g) * u).astype(BF16)
        acc = acc + _dot(hid, wd_ref[0, 0, cols, :])
    o_ref[0, 0] = (acc * gate).astype(BF16)


def _ffn(xin, wg, wu, wd, layer):
    B, E, S, W = xin.shape
    D, FF = wg.shape[2], wg.shape[3]
    return pl.pallas_call(
        _ffn_kernel,
        out_shape=jax.ShapeDtypeStruct((B, E, S, D), BF16),
        grid=(E, B),
        in_specs=[pl.BlockSpec((1, 1, S, W), lambda e, b: (b, e, 0, 0)),
                  pl.BlockSpec((1, 1, D, FF), lambda e, b: (layer, e, 0, 0)),
                  pl.BlockSpec((1, 1, D, FF), lambda e, b: (layer, e, 0, 0)),
                  pl.BlockSpec((1, 1, FF, D), lambda e, b: (layer, e, 0, 0))],
        out_specs=pl.BlockSpec((1, 1, S, D), lambda e, b: (b, e, 0, 0)),
        compiler_params=_params(2),
        name="expert_ffn",
    )(xin, wg, wu, wd)


def _combine_kernel(first_s, npass_s, slot_ref, first_ref, expand_ref, y_ref, x_ref, g2_ref,
                    *rest, n_tiles):
    o_ref = rest[-1]
    b, t = pl.program_id(0), pl.program_id(1)
    cap_total = y_ref.shape[2]
    o_ref[0] = x_ref[0]
    slot = slot_ref[0]
    first = first_ref[0, 0]
    lane_row = (lax.broadcasted_iota(jnp.int32, (TILE, N_EXPERTS * SLOT_WIN), 1)
                & (SLOT_WIN - 1)).astype(F32)

    def one_pass(k, carry):
        rel = _window_rel(slot, first, k.astype(F32), cap_total).astype(BF16)
        onehot = (_dot(rel, expand_ref[...]) == lane_row).astype(BF16)
        rows = jnp.concatenate(
            [y_ref[0, e, pl.ds(_window_begin(first_s[(b * n_tiles + t) * N_EXPERTS + e], k,
                                             cap_total), SLOT_WIN), :]
             for e in range(N_EXPERTS)], axis=0)
        o_ref[0] += g2_ref[0, 0] * _dot(onehot, rows)
        return carry

    lax.fori_loop(0, npass_s[b * n_tiles + t], one_pass, 0)
    if len(rest) == 2:
        x = o_ref[0]
        ms = jnp.mean(x * x, axis=-1, keepdims=True)
        o_ref[0] = x * lax.rsqrt(ms + RMS_EPS) * rest[0][...]


def _combine(y, slot, first, n_pass, xa, gate2, final_g=None):
    B, S, D = xa.shape
    nt = S // TILE
    E, cap_total = y.shape[1], y.shape[2]
    expand = np.repeat(np.eye(E, dtype=np.float32), SLOT_WIN, axis=1)
    in_specs = [pl.BlockSpec((1, TILE, E), lambda b, t, *_: (b, t, 0)),
                pl.BlockSpec((1, 1, 1, E), lambda b, t, *_: (b, t, 0, 0)),
                pl.BlockSpec((E, E * SLOT_WIN), lambda b, t, *_: (0, 0)),
                pl.BlockSpec((1, E, cap_total, D), lambda b, t, *_: (b, 0, 0, 0)),
                pl.BlockSpec((1, TILE, D), lambda b, t, *_: (b, t, 0)),
                pl.BlockSpec((1, 1, 1, D), lambda b, t, *_: (b, t // (nt - 1), 0, 0))]
    args = [slot, first.astype(F32).reshape(B, nt, 1, E), jnp.asarray(expand, BF16), y, xa, gate2]
    out_tiles = nt
    if final_g is not None:
        in_specs.append(pl.BlockSpec((1, D), lambda b, t, *_: (0, 0)))
        args.append(final_g)
        out_tiles = nt - 1
    grid_spec = pltpu.PrefetchScalarGridSpec(
        num_scalar_prefetch=2,
        grid=(B, out_tiles),
        in_specs=in_specs,
        out_specs=pl.BlockSpec((1, TILE, D), lambda b, t, *_: (b, t, 0)),
    )
    return pl.pallas_call(
        functools.partial(_combine_kernel, n_tiles=nt),
        out_shape=jax.ShapeDtypeStruct((B, out_tiles * TILE, D), F32),
        grid_spec=grid_spec,
        compiler_params=_params(2),
        name="combine",
    )(first.reshape(-1), n_pass.reshape(-1), *args)


def _scale_q(w, nq):
    return jnp.concatenate([w[:, :nq] * (HEAD_DIM ** -0.5 * LOG2E), w[:, nq:]], axis=1)


def _win_weights(w_in):
    D = w_in.shape[0]
    nq = 16 * HEAD_DIM
    nk = A_KV_HEADS * HEAD_DIM
    w = _scale_q(w_in, nq)
    dup = lambda m: jnp.concatenate([m.reshape(D, A_KV_HEADS, 1, HEAD_DIM)] * 2,
                                    axis=2).reshape(D, 2 * nk)
    return jnp.concatenate([w[:, :nq], dup(w[:, nq:nq + nk]), dup(w[:, nq + nk:])],
                           axis=1).astype(BF16)


def _router_weights(rw):
    D, E = rw.shape
    return jnp.concatenate([rw, rw, rw, jnp.zeros((D, GATE_COLS - 3 * E), rw.dtype)], axis=1)


def kernel(x, c, ctx, c_ctx, ada_w, ada_b, norm1_g, norm2_g, final_g, win_w_in, win_w_out,
           win_sink, diff_w_in, diff_w_out, diff_lambda, diff_subln_g, na_w_in, na_w_out,
           na_rpb, router_w, w_gate, w_up, w_down):
    B, T, D = x.shape
    L = ctx.shape[1]
    S = T + L
    depth = ada_w.shape[0]
    assert L == TILE and T % ROW_TILE == 0 and (T // GRID_W) >= NA_WIN_ROWS
    cs = jnp.concatenate([c, c_ctx[None, :]], axis=0)
    rope = _rope_tables(T, L)
    wg_all, wu_all, wd_all = w_gate.astype(BF16), w_up.astype(BF16), w_down.astype(BF16)
    xa = None
    for i in range(depth):
        kind = i % N_MIXERS
        slot = i // N_MIXERS
        mod = _ada(cs, ada_w[i], ada_b[i][None, :])
        mod = jnp.stack([mod[:B], jnp.broadcast_to(mod[B:], (B, 6 * D))], axis=1)
        mod = mod.reshape(B, 2, 1, 6, D)
        sh1, sc1, g1, sh2, sc2, g2 = [mod[:, :, :, k, :] for k in range(6)]
        if kind == 0:
            w_in = _win_weights(win_w_in[slot])
            n_rope = (16 + 2 * A_KV_HEADS) * HEAD_DIM
            w_out = win_w_out[slot]
        elif kind == 1:
            w_in = _scale_q(diff_w_in[slot], 16 * HEAD_DIM).astype(BF16)
            n_rope = 32 * HEAD_DIM
            w_out = diff_w_out[slot]
        else:
            w_in = _scale_q(na_w_in[slot], 16 * HEAD_DIM).astype(BF16)
            n_rope = 0
            w_out = na_w_out[slot]
        if i == 0:
            qkv, xa = _norm_proj((x, ctx), norm1_g[i][None, :], sh1, sc1, w_in, rope, n_rope)
        else:
            qkv = _norm_proj((xa,), norm1_g[i][None, :], sh1, sc1, w_in, rope, n_rope)
        if kind == 0:
            y = _attn_win(qkv, win_sink[slot], T)
        elif kind == 1:
            lambda_init = 0.8 - 0.6 * math.exp(-0.3 * i)
            lp = diff_lambda[slot]
            lam = (jnp.exp(jnp.sum(lp[0] * lp[1])) - jnp.exp(jnp.sum(lp[2] * lp[3]))
                   + lambda_init).reshape(1)
            y = _attn_diff(qkv, lam, diff_subln_g[slot][None, :], T, 1.0 - lambda_init)
        else:
            y = _attn_na(qkv, _na_bias_tables(na_rpb[slot], T), _na_bias_bound(na_rpb[slot]), T)
        xa, hx, aff = _out_router(y, w_out.astype(BF16), xa, g1, norm2_g[i][None, :],
                                  sh2, sc2, _router_weights(router_w[i]))
        tok_slot, start, cnt = _route(aff, T)
        first, n_pass = _slot_plan(start, cnt)
        cap_total = CAPACITY_FACTOR * S // N_EXPERTS
        xin = _dispatch(hx, tok_slot, first, n_pass, cap_total)
        ye = _ffn(xin, wg_all, wu_all, wd_all, i)
        last = i == depth - 1
        xa = _combine(ye, tok_slot, first, n_pass, xa, g2, final_g[None, :] if last else None)
    return xa
```

```python
import functools
import math

import numpy as np
import jax
import jax.numpy as jnp
from jax import lax
from jax.experimental import pallas as pl
from jax.experimental.pallas import tpu as pltpu

HEAD_DIM = 64
LANES = 128
GRID_W = 64
NA_ROWS = 8
NA_COLS = 16
A_WINDOW = 128
A_KV_HEADS = 4
N_MIXERS = 3
ROPE_BASE = 10000.0
ROPE_AXIS_DIM = HEAD_DIM // 2
N_EXPERTS = 16
CAPACITY_FACTOR = 2
RMS_EPS = 1e-6
NEG_INF = -1e30
LOG2E = math.log2(math.e)
TILE = 256
ROW_TILE = 2 * TILE
NA_TILE_ROWS = TILE // GRID_W
NA_WIN_ROWS = NA_TILE_ROWS + NA_ROWS
VMEM_LIMIT = 56 * 1024 * 1024
SLOT_WIN = 64
SLOT_ALIGN = 16
GATE_COLS = LANES

BF16 = jnp.bfloat16
F32 = jnp.float32


def _params(n_grid):
    return pltpu.CompilerParams(
        dimension_semantics=("arbitrary",) * n_grid, vmem_limit_bytes=VMEM_LIMIT)


def _split_bf16(a):
    hi = a.astype(BF16)
    lo = (a - hi.astype(F32)).astype(BF16)
    return hi, lo


def _dot(a, b):
    return jnp.dot(a, b, preferred_element_type=F32)


def _dot_nt(a, b):
    return lax.dot_general(a, b, (((1,), (1,)), ((), ())), preferred_element_type=F32)


def _dot_tn(a, b):
    return lax.dot_general(a, b, (((0,), (0,)), ((), ())), preferred_element_type=F32)


def _ada_kernel(c_ref, w_ref, b_ref, o_ref):
    c = c_ref[...]
    a = c * jax.nn.sigmoid(c)
    a_hi, a_lo = _split_bf16(a)
    w_hi, w_lo = _split_bf16(w_ref[...])
    o_ref[...] = _dot(a_hi, w_hi) + _dot(a_lo, w_hi) + _dot(a_hi, w_lo) + b_ref[...]


def _ada(cs, w, b):
    R, D = cs.shape
    N = w.shape[1]
    tn = 1024
    return pl.pallas_call(
        _ada_kernel,
        out_shape=jax.ShapeDtypeStruct((R, N), F32),
        grid=(N // tn,),
        in_specs=[pl.BlockSpec((R, D), lambda j: (0, 0)),
                  pl.BlockSpec((D, tn), lambda j: (0, j)),
                  pl.BlockSpec((1, tn), lambda j: (0, j))],
        out_specs=pl.BlockSpec((R, tn), lambda j: (0, j)),
        compiler_params=_params(1),
        name="ada",
    )(cs, w, b)


def _rms_mod(x, g, shift, scale):
    ms = jnp.mean(x * x, axis=-1, keepdims=True)
    y = x * lax.rsqrt(ms + RMS_EPS) * g
    return y * (1.0 + scale) + shift


def _norm_proj_kernel(*refs, n_rope, joins_streams):
    if joins_streams:
        x_ref, c_ref, g_ref, sh_ref, sc_ref, w_ref, cos_ref, sa_ref, sb_ref, o_ref, xa_ref = refs
        is_ctx = pl.program_id(1) == pl.num_programs(1) - 1
        ctx_rows = jnp.concatenate([c_ref[0]] * (ROW_TILE // TILE), axis=0)
        x = jnp.where(is_ctx, ctx_rows, x_ref[0])
        xa_ref[0] = x
    else:
        x_ref, g_ref, sh_ref, sc_ref, w_ref, cos_ref, sa_ref, sb_ref, o_ref = refs
        x = x_ref[0]
    h = _rms_mod(x, g_ref[...], sh_ref[0, 0], sc_ref[0, 0]).astype(BF16)
    n_cols = w_ref.shape[1]
    chunk = 512
    for j in range(n_cols // chunk):
        acc = _dot(h, w_ref[:, j * chunk:(j + 1) * chunk])
        for t in range(chunk // LANES):
            col = j * chunk + t * LANES
            a = acc[:, t * LANES:(t + 1) * LANES]
            if col < n_rope:
                a = (a * cos_ref[...]
                     + pltpu.roll(a, LANES - 16, 1) * sa_ref[...]
                     + pltpu.roll(a, 16, 1) * sb_ref[...])
            o_ref[0, :, col:col + LANES] = a.astype(BF16)


def _norm_proj(streams, g, shift, scale, w, rope, n_rope):
    joins = len(streams) == 2
    B, _, D = streams[0].shape
    S = sum(a.shape[1] for a in streams)
    N = w.shape[1]
    nt = pl.cdiv(S, ROW_TILE)
    mod_spec = pl.BlockSpec((1, 1, 1, D), lambda b, t: (b, t // (nt - 1), 0, 0))
    rope_spec = pl.BlockSpec((ROW_TILE, LANES), lambda b, t: (t, 0))
    tile_spec = pl.BlockSpec((1, ROW_TILE, D), lambda b, t: (b, t, 0))
    if joins:
        stream_specs = [pl.BlockSpec((1, ROW_TILE, D), lambda b, t: (b, jnp.minimum(t, nt - 2), 0)),
                        pl.BlockSpec((1, TILE, D), lambda b, t: (b, 0, 0))]
    else:
        stream_specs = [tile_spec]
    qkv_shape = jax.ShapeDtypeStruct((B, S, N), BF16)
    qkv_spec = pl.BlockSpec((1, ROW_TILE, N), lambda b, t: (b, t, 0))
    return pl.pallas_call(
        functools.partial(_norm_proj_kernel, n_rope=n_rope, joins_streams=joins),
        out_shape=(qkv_shape, jax.ShapeDtypeStruct((B, S, D), F32)) if joins else qkv_shape,
        grid=(B, nt),
        in_specs=stream_specs + [pl.BlockSpec((1, D), lambda b, t: (0, 0)),
                                 mod_spec, mod_spec,
                                 pl.BlockSpec((D, N), lambda b, t: (0, 0)),
                                 rope_spec, rope_spec, rope_spec],
        out_specs=(qkv_spec, tile_spec) if joins else qkv_spec,
        compiler_params=_params(2),
        name="norm_proj",
    )(*streams, g, shift, scale, w, *rope)


def _rope_tables(T, L):
    t = np.arange(T)
    pos = np.stack([t // GRID_W, t % GRID_W], axis=0).astype(np.float32)
    inv = (1.0 / (ROPE_BASE ** (np.arange(0, ROPE_AXIS_DIM, 2, dtype=np.float32)
                                / ROPE_AXIS_DIM))).astype(np.float32)
    d = np.arange(LANES) % HEAD_DIM
    axis = d // ROPE_AXIS_DIM
    half = (d % ROPE_AXIS_DIM) // (ROPE_AXIS_DIM // 2)
    freq = d % (ROPE_AXIS_DIM // 2)
    ang = jnp.asarray(pos[axis].T) * jnp.asarray(inv[freq])[None, :]
    cos = jnp.cos(ang)
    sin = jnp.sin(ang)
    first = jnp.asarray(half == 0)[None, :]
    sa = jnp.where(first, -sin, 0.0)
    sb = jnp.where(first, 0.0, sin)
    pad = lambda a, v: jnp.concatenate([a, jnp.full((L, LANES), v, F32)], axis=0)
    return pad(cos, 1.0), pad(sa, 0.0), pad(sb, 0.0)


def _half_masks(q):
    lane = lax.broadcasted_iota(jnp.int32, q.shape, 1)
    zero = jnp.zeros_like(q)
    return jnp.where(lane < HEAD_DIM, q, zero), jnp.where(lane >= HEAD_DIM, q, zero)


def _merge_halves(o_first, o_second):
    lane = lax.broadcasted_iota(jnp.int32, o_first.shape, 1)
    return jnp.where(lane < HEAD_DIM, o_first, o_second)


SHIFT_SLACK = 1.02
L_MIN = 2.0 ** -64


def _sq_norms_row(x):
    xf = x.astype(F32)
    return _dot_nt(jnp.ones((8, LANES), BF16), (xf * xf).astype(BF16))[:1]


def _head_bounds_row(qs):
    qf = qs.astype(F32)
    n2 = jnp.sum(qf * qf, axis=-1, keepdims=True)
    return jnp.concatenate(
        [jnp.broadcast_to(jnp.max(n2[r:r + TILE], axis=0, keepdims=True), (1, TILE))
         for r in range(0, qs.shape[0], TILE)], axis=1)


def _store_key_bound(kmax_ref, j, k, scale=1.0):
    n2 = jnp.max(_sq_norms_row(k), axis=1, keepdims=True) * scale
    kmax_ref[j] = jnp.broadcast_to(n2, kmax_ref.shape[1:])


def _redo_if_underflow(attend, *args):
    den_min = attend(*args, exact=False)

    @pl.when(den_min[0, 0] < L_MIN)
    def _():
        attend(*args, exact=True)


WIN_KV_PER_STEP = 4


def _win_kernel(sink_ref, q_ref, k_ref, v_ref, o_ref, kmax_ref, *, T, win):
    hb = pl.program_id(1)
    t = pl.program_id(2)
    n_x = T // TILE
    is_x = t < n_x
    heads = [slice(j * LANES, (j + 1) * LANES) for j in range(WIN_KV_PER_STEP)]

    @pl.when(t == 0)
    def _():
        for j, cols in enumerate(heads):
            _store_key_bound(kmax_ref, j, k_ref[0, :, cols], 0.5)

    start = pl.multiple_of(jnp.clip(t * TILE - A_WINDOW, 0, T - win), LANES)
    kpos = start + lax.broadcasted_iota(jnp.int32, (win, TILE), 0)
    qpos = t * TILE + lax.broadcasted_iota(jnp.int32, (win, TILE), 1)
    band = (jnp.abs(qpos - kpos) <= A_WINDOW) & is_x
    band = jnp.concatenate([band] * 4, axis=1)

    def attend(exact):
        queries, scores = [], []
        for j, cols in enumerate(heads):
            q = q_ref[0, :, 2 * j * LANES:2 * (j + 1) * LANES]
            qa, qb = _half_masks(q[:, :LANES])
            qc, qd = _half_masks(q[:, LANES:])
            qs = jnp.concatenate([qa, qb, qc, qd], axis=0)
            queries.append(qs)
            scores.append((_dot_nt(k_ref[0, T:, cols], qs),
                           _dot_nt(k_ref[0, pl.ds(start, win), cols], qs)))
        den_min = None
        for j, cols in enumerate(heads):
            s_c, s_w = scores[j]
            head0 = (hb * WIN_KV_PER_STEP + j) * 4
            s_w = jnp.where(band, s_w, NEG_INF)
            sink = jnp.concatenate(
                [jnp.full((1, TILE), sink_ref[head0 + g] * LOG2E, F32) for g in range(4)], axis=1)
            if exact:
                m = jnp.maximum(jnp.max(s_c, axis=0, keepdims=True),
                                jnp.max(s_w, axis=0, keepdims=True))
            else:
                m = jnp.sqrt(_head_bounds_row(queries[j]) * kmax_ref[j, :1, :1]) * SHIFT_SLACK
            m = jnp.maximum(m, sink)
            ec = jnp.exp2(s_c - m)
            ew = jnp.exp2(s_w - m)
            den = (jnp.sum(ec, axis=0, keepdims=True) + jnp.sum(ew, axis=0, keepdims=True)
                   + jnp.exp2(sink - m))
            o = ((_dot_tn(v_ref[0, T:, cols], ec.astype(BF16))
                  + _dot_tn(v_ref[0, pl.ds(start, win), cols], ew.astype(BF16))) * (1.0 / den)).T
            first = 2 * j * LANES
            o_ref[0, :, first:first + LANES] = _merge_halves(
                o[:TILE], o[TILE:2 * TILE]).astype(BF16)
            o_ref[0, :, first + LANES:first + 2 * LANES] = _merge_halves(
                o[2 * TILE:3 * TILE], o[3 * TILE:]).astype(BF16)
            low = jnp.min(den, axis=1, keepdims=True)
            den_min = low if den_min is None else jnp.minimum(den_min, low)
        return den_min

    _redo_if_underflow(attend)


def _attn_win(qkv, sink, T):
    B, S, _ = qkv.shape
    nt = S // TILE
    nq = 16 * HEAD_DIM
    win = TILE + 2 * A_WINDOW
    hb = A_KV_HEADS // WIN_KV_PER_STEP
    wq = WIN_KV_PER_STEP * 2 * LANES
    wk = WIN_KV_PER_STEP * LANES
    kb = nq // wk
    return pl.pallas_call(
        functools.partial(_win_kernel, T=T, win=win),
        out_shape=jax.ShapeDtypeStruct((B, S, nq), BF16),
        grid=(B, hb, nt),
        in_specs=[pl.BlockSpec(memory_space=pltpu.SMEM),
                  pl.BlockSpec((1, TILE, wq), lambda b, h, t: (b, t, h)),
                  pl.BlockSpec((1, S, wk), lambda b, h, t: (b, 0, kb + h)),
                  pl.BlockSpec((1, S, wk), lambda b, h, t: (b, 0, kb + hb + h))],
        out_specs=pl.BlockSpec((1, TILE, wq), lambda b, h, t: (b, t, h)),
        scratch_shapes=[pltpu.VMEM((WIN_KV_PER_STEP, 8, LANES), F32)],
        compiler_params=_params(3),
        name="attn_win",
    )(sink, qkv, qkv, qkv)


DIFF_HEADS_PER_STEP = 2


def _diff_kernel(lam_ref, q_ref, k_ref, v_ref, g_ref, o_ref, kmax_ref, *, T, out_scale):
    t = pl.program_id(2)
    n_x = T // TILE
    lam = lam_ref[0]
    heads = [slice(j * LANES, (j + 1) * LANES) for j in range(DIFF_HEADS_PER_STEP)]

    @pl.when(t == 0)
    def _():
        for j, cols in enumerate(heads):
            k1, k2 = _half_masks(k_ref[0, :, cols])
            _store_key_bound(kmax_ref, 2 * j, k1)
            _store_key_bound(kmax_ref, 2 * j + 1, k2)

    def attend(keys, exact):
        queries, scores = [], []
        for cols in heads:
            qa, qb = _half_masks(q_ref[0, :, cols])
            qs = jnp.concatenate([qa, qb], axis=0)
            queries.append(qs)
            scores.append(_dot_nt(qs, k_ref[0, keys, cols]))
        l_min = None
        for j, cols in enumerate(heads):
            s = scores[j]
            if exact:
                m = jnp.max(s, axis=-1, keepdims=True)
            else:
                qf = queries[j].astype(F32)
                row = lax.broadcasted_iota(jnp.int32, (2 * TILE, 1), 0)
                kmax = jnp.where(row < TILE, kmax_ref[2 * j, :1, :1], kmax_ref[2 * j + 1, :1, :1])
                m = jnp.sqrt(jnp.sum(qf * qf, axis=-1, keepdims=True) * kmax) * SHIFT_SLACK
            e = jnp.exp2(s - m)
            l = jnp.sum(e, axis=-1, keepdims=True)
            a = e[:TILE] - e[TILE:] * (lam * l[:TILE] / l[TILE:])
            o = _dot(a.astype(BF16), v_ref[0, keys, cols]) / l[:TILE]
            ms = jnp.mean(o * o, axis=-1, keepdims=True)
            y = o * lax.rsqrt(ms + RMS_EPS) * g_ref[...] * out_scale
            o_ref[0, :, cols] = y.astype(BF16)
            low = jnp.min(l, axis=0, keepdims=True)
            l_min = low if l_min is None else jnp.minimum(l_min, low)
        return l_min

    @pl.when(t < n_x)
    def _():
        _redo_if_underflow(attend, slice(None))

    @pl.when(t >= n_x)
    def _():
        _redo_if_underflow(attend, slice(T, None))


def _attn_diff(qkv, lam, subln_g, T, out_scale):
    B, S, _ = qkv.shape
    nt = S // TILE
    H = 8
    hb = H // DIFF_HEADS_PER_STEP
    w = DIFF_HEADS_PER_STEP * LANES
    return pl.pallas_call(
        functools.partial(_diff_kernel, T=T, out_scale=out_scale),
        out_shape=jax.ShapeDtypeStruct((B, S, H * LANES), BF16),
        grid=(B, hb, nt),
        in_specs=[pl.BlockSpec(memory_space=pltpu.SMEM),
                  pl.BlockSpec((1, TILE, w), lambda b, h, t: (b, t, h)),
                  pl.BlockSpec((1, S, w), lambda b, h, t: (b, 0, hb + h)),
                  pl.BlockSpec((1, S, w), lambda b, h, t: (b, 0, 2 * hb + h)),
                  pl.BlockSpec((1, LANES), lambda b, h, t: (0, 0))],
        out_specs=pl.BlockSpec((1, TILE, w), lambda b, h, t: (b, t, h)),
        scratch_shapes=[pltpu.VMEM((2 * DIFF_HEADS_PER_STEP, 8, LANES), F32)],
        compiler_params=_params(3),
        name="attn_diff",
    )(lam, qkv, qkv, qkv, subln_g)


NA_PAIRS_PER_STEP = 2


def _na_kernel(q_ref, k_ref, v_ref, bias_ref, bmax_ref, o_ref, kmax_ref, *, T):
    t = pl.program_id(2)
    n_x = T // TILE
    rows = T // GRID_W
    win = NA_WIN_ROWS * GRID_W
    pairs = [slice(j * LANES, (j + 1) * LANES) for j in range(NA_PAIRS_PER_STEP)]

    @pl.when(t == 0)
    def _():
        for j, cols in enumerate(pairs):
            k1, k2 = _half_masks(k_ref[0, :, cols])
            _store_key_bound(kmax_ref, 2 * j, k1)
            _store_key_bound(kmax_ref, 2 * j + 1, k2)

    cls = jnp.where(t >= n_x, 3, jnp.where(t == 0, 0, jnp.where(t == n_x - 1, 2, 1)))
    row0 = jnp.clip(t * NA_TILE_ROWS - NA_ROWS // 2, 0, rows - NA_WIN_ROWS)
    start = pl.multiple_of(row0 * GRID_W, GRID_W)

    def attend(exact):
        queries, scores = [], []
        for cols in pairs:
            qa, qb = _half_masks(q_ref[0, :, cols])
            qs = jnp.concatenate([qa, qb], axis=0)
            queries.append(qs)
            scores.append((_dot_nt(k_ref[0, T:, cols], qs),
                           _dot_nt(k_ref[0, pl.ds(start, win), cols], qs)))
        den_min = None
        for j, cols in enumerate(pairs):
            s_c, s_w = scores[j]
            s_w = s_w + bias_ref[cls, j]
            if exact:
                m = jnp.maximum(jnp.max(s_c, axis=0, keepdims=True),
                                jnp.max(s_w, axis=0, keepdims=True))
            else:
                lane = lax.broadcasted_iota(jnp.int32, (1, 2 * TILE), 1)
                kmax = jnp.where(lane < TILE, kmax_ref[2 * j, :1, :1], kmax_ref[2 * j + 1, :1, :1])
                m = jnp.sqrt(_head_bounds_row(queries[j]) * kmax) * SHIFT_SLACK + bmax_ref[j]
            ec = jnp.exp2(s_c - m)
            ew = jnp.exp2(s_w - m)
            den = jnp.sum(ec, axis=0, keepdims=True) + jnp.sum(ew, axis=0, keepdims=True)
            o = ((_dot_tn(v_ref[0, T:, cols], ec.astype(BF16))
                  + _dot_tn(v_ref[0, pl.ds(start, win), cols], ew.astype(BF16))) * (1.0 / den)).T
            o_ref[0, :, cols] = _merge_halves(o[:TILE], o[TILE:]).astype(BF16)
            low = jnp.min(den, axis=1, keepdims=True)
            den_min = low if den_min is None else jnp.minimum(den_min, low)
        return den_min

    _redo_if_underflow(attend)


def _attn_na(qkv, bias, bias_max, T):
    B, S, _ = qkv.shape
    nt = S // TILE
    n = NA_PAIRS_PER_STEP
    hb = 8 // n
    w = n * LANES
    win = NA_WIN_ROWS * GRID_W
    return pl.pallas_call(
        functools.partial(_na_kernel, T=T),
        out_shape=jax.ShapeDtypeStruct((B, S, 8 * LANES), BF16),
        grid=(hb, B, nt),
        in_specs=[pl.BlockSpec((1, TILE, w), lambda h, b, t: (b, t, h)),
                  pl.BlockSpec((1, S, w), lambda h, b, t: (b, 0, hb + h)),
                  pl.BlockSpec((1, S, w), lambda h, b, t: (b, 0, 2 * hb + h)),
                  pl.BlockSpec((4, n, win, 2 * TILE), lambda h, b, t: (0, h, 0, 0)),
                  pl.BlockSpec((n, 1, 2 * TILE), lambda h, b, t: (h, 0, 0))],
        out_specs=pl.BlockSpec((1, TILE, w), lambda h, b, t: (b, t, h)),
        scratch_shapes=[pltpu.VMEM((2 * NA_PAIRS_PER_STEP, 8, LANES), F32)],
        compiler_params=_params(3),
        name="attn_na",
    )(qkv, qkv, qkv, bias, bias_max)


def _na_bias_bound(rpb):
    top = jnp.maximum(jnp.max(rpb, axis=(1, 2)), 0.0) * LOG2E
    return jnp.repeat(top.reshape(-1, 2), TILE, axis=1)[:, None, :]


def _na_bias_tables(rpb, T):
    rows = T // GRID_W
    H = rpb.shape[0]
    win = NA_WIN_ROWS * GRID_W
    hi = lax.Precision.HIGHEST
    pick = lambda idx, n: jnp.asarray(idx[..., None] == np.arange(n), F32)
    c = np.arange(GRID_W)
    cs = np.clip(c - NA_COLS // 2, 0, GRID_W - NA_COLS)
    valid_c = (c[None, :] >= cs[:, None]) & (c[None, :] < cs[:, None] + NA_COLS)
    bidx_c = np.clip(c[None, :] - c[:, None] + NA_COLS - 1, 0, 2 * NA_COLS - 2)
    by_col = jnp.einsum('hrd,ckd->hrck', rpb, pick(bidx_c, 2 * NA_COLS - 1), precision=hi)
    i = np.arange(NA_TILE_ROWS)
    j = np.arange(NA_WIN_ROWS)
    tables = []
    for r0 in (0, NA_TILE_ROWS, rows - NA_TILE_ROWS):
        s = int(np.clip(r0 - NA_ROWS // 2, 0, rows - NA_WIN_ROWS))
        r = r0 + i
        rs = np.clip(r - NA_ROWS // 2, 0, rows - NA_ROWS)
        kr = s + j
        valid_r = (kr[None, :] >= rs[:, None]) & (kr[None, :] < rs[:, None] + NA_ROWS)
        bidx_r = np.clip(kr[None, :] - r[:, None] + NA_ROWS - 1, 0, 2 * NA_ROWS - 2)
        vals = jnp.einsum('hrck,ijr->hjkic', by_col, pick(bidx_r, 2 * NA_ROWS - 1), precision=hi)
        valid = valid_r.T[:, None, :, None] & valid_c.T[None, :, None, :]
        vals = jnp.where(jnp.asarray(valid)[None], vals, NEG_INF).reshape(H // 2, 2, win, TILE)
        tables.append(vals.transpose(0, 2, 1, 3).reshape(H // 2, win, 2 * TILE))
    tables.append(jnp.full_like(tables[0], NEG_INF))
    return jnp.stack(tables, axis=0).astype(F32) * LOG2E


def _out_router_kernel(y_ref, w_ref, x_ref, g1_ref, n2_ref, sh_ref, sc_ref, rw_ref,
                       xo_ref, h_ref, aff_ref):
    D = x_ref.shape[2]
    x = x_ref[0] + g1_ref[0, 0] * _dot(y_ref[0], w_ref[...])
    xo_ref[0] = x
    h = _rms_mod(x, n2_ref[...], sh_ref[0, 0], sc_ref[0, 0])
    h_hi, h_lo = _split_bf16(h)
    n = h.shape[0]
    r_hi_lo = jnp.concatenate(_split_bf16(rw_ref[...]), axis=1)
    prod = _dot(jnp.concatenate([h_hi, h_lo], axis=0), r_hi_lo)
    logits = prod[:n, :GATE_COLS] + prod[:n, GATE_COLS:] + prod[n:, :GATE_COLS]
    lane = lax.broadcasted_iota(jnp.int32, logits.shape, 1)
    first = lane < N_EXPERTS
    m = jnp.max(jnp.where(first, logits, -jnp.inf), axis=-1, keepdims=True)
    e = jnp.exp(logits - m)
    aff = e / jnp.sum(jnp.where(first, e, 0.0), axis=-1, keepdims=True)
    aff_ref[0] = aff
    hi = aff.astype(BF16)
    rem = aff - hi.astype(F32)
    mid = rem.astype(BF16)
    lo = (rem - mid.astype(F32)).astype(BF16)
    zero = jnp.zeros_like(hi)
    pieces = jnp.where(first, hi, jnp.where(lane < 2 * N_EXPERTS, mid,
                                            jnp.where(lane < 3 * N_EXPERTS, lo, zero)))
    h_ref[0, :, :D] = h_hi
    h_ref[0, :, D:] = pieces


def _out_router(y, w_out, xa, gate1, n2g, shift2, scale2, rw3):
    B, S, D = xa.shape
    nt = pl.cdiv(S, ROW_TILE)
    mod_spec = pl.BlockSpec((1, 1, 1, D), lambda b, t: (b, t // (nt - 1), 0, 0))
    tile_spec = pl.BlockSpec((1, ROW_TILE, D), lambda b, t: (b, t, 0))
    return pl.pallas_call(
        _out_router_kernel,
        out_shape=(jax.ShapeDtypeStruct((B, S, D), F32),
                   jax.ShapeDtypeStruct((B, S, D + GATE_COLS), BF16),
                   jax.ShapeDtypeStruct((B, S, GATE_COLS), F32)),
        grid=(B, nt),
        in_specs=[tile_spec,
                  pl.BlockSpec((D, D), lambda b, t: (0, 0)),
                  tile_spec, mod_spec,
                  pl.BlockSpec((1, D), lambda b, t: (0, 0)),
                  mod_spec, mod_spec,
                  pl.BlockSpec((D, GATE_COLS), lambda b, t: (0, 0))],
        out_specs=(tile_spec,
                   pl.BlockSpec((1, ROW_TILE, D + GATE_COLS), lambda b, t: (b, t, 0)),
                   pl.BlockSpec((1, ROW_TILE, GATE_COLS), lambda b, t: (b, t, 0))),
        compiler_params=_params(2),
        name="out_router",
    )(y, w_out, xa, gate1, n2g, shift2, scale2, rw3)


def _route_kernel(aff_ref, slot_ref, start_ref, cnt_ref, *, T, cap_x, cap_c):
    S = aff_ref.shape[1]
    aff = aff_ref[0]
    bits = lax.bitcast_convert_type(aff[:, :N_EXPERTS], jnp.int32)
    ri = lax.broadcasted_iota(jnp.int32, (TILE, TILE), 0)
    ci = lax.broadcasted_iota(jnp.int32, (TILE, TILE), 1)
    ltri = (ri > ci).astype(BF16)
    eye = (lax.broadcasted_iota(jnp.int32, (N_EXPERTS, N_EXPERTS), 0)
           == lax.broadcasted_iota(jnp.int32, (N_EXPERTS, N_EXPERTS), 1))

    def count(mask):
        return jnp.sum(mask.astype(F32), axis=0, keepdims=True)

    for lo_row, hi_row, cap, base in ((0, T, cap_x, 0), (T, S, cap_c, cap_x)):
        b = bits[lo_row:hi_row]
        dense = lax.bitcast_convert_type(jnp.concatenate(
            [aff[r:r + TILE].T[:N_EXPERTS] for r in range(lo_row, hi_row, TILE)], axis=1),
            jnp.int32)

        def step(i, thr):
            cand = thr | lax.shift_left(jnp.int32(1), 30 - i)
            n_ge = jnp.sum((dense >= cand).astype(F32), axis=1, keepdims=True)
            return jnp.where(n_ge >= cap, cand, thr)

        thr = lax.fori_loop(0, 31, step, jnp.zeros((N_EXPERTS, 1), jnp.int32))
        thr = jnp.max(jnp.where(eye, jnp.broadcast_to(thr, eye.shape), 0), axis=0,
                      keepdims=True)
        need = cap - count(b > thr)
        seen_eq = jnp.zeros((1, N_EXPERTS), F32)
        seen = jnp.zeros((1, N_EXPERTS), F32)
        for j in range((hi_row - lo_row) // TILE):
            blk = b[j * TILE:(j + 1) * TILE]
            gt = blk > thr
            eq = blk == thr
            eq_rank = _dot(ltri, eq.astype(BF16)) + seen_eq
            sel = gt | (eq & (eq_rank < need))
            pos = _dot(ltri, sel.astype(BF16)) + seen + base
            t = lo_row // TILE + j
            slot_ref[0, t * TILE:(t + 1) * TILE, :] = jnp.where(sel, pos, -1.0)
            n_sel = count(sel)
            start_ref[0, t:t + 1, :] = seen + base
            cnt_ref[0, t:t + 1, :] = n_sel
            seen_eq = seen_eq + count(eq)
            seen = seen + n_sel


def _route(aff, T):
    B, S, _ = aff.shape
    E = N_EXPERTS
    nt = S // TILE
    cap_x = CAPACITY_FACTOR * T // N_EXPERTS
    cap_c = CAPACITY_FACTOR * (S - T) // N_EXPERTS
    plan = jax.ShapeDtypeStruct((B, nt, E), F32)
    return pl.pallas_call(
        functools.partial(_route_kernel, T=T, cap_x=cap_x, cap_c=cap_c),
        out_shape=(jax.ShapeDtypeStruct((B, S, E), F32), plan, plan),
        grid=(B,),
        in_specs=[pl.BlockSpec((1, S, GATE_COLS), lambda b: (b, 0, 0))],
        out_specs=(pl.BlockSpec((1, S, E), lambda b: (b, 0, 0)),
                   pl.BlockSpec((1, nt, E), lambda b: (b, 0, 0)),
                   pl.BlockSpec((1, nt, E), lambda b: (b, 0, 0))),
        compiler_params=_params(1),
        name="route",
    )(aff)


def _slot_plan(start, cnt):
    start = start.astype(jnp.int32)
    cnt = cnt.astype(jnp.int32)
    first = (start // SLOT_ALIGN) * SLOT_ALIGN
    n_pass = jnp.max((start - first + cnt + SLOT_WIN - 1) // SLOT_WIN, axis=-1)
    return first, n_pass.astype(jnp.int32)


def _window_rel(slot, first, k, cap_total):
    nominal = first + k * SLOT_WIN
    begin = jnp.minimum(nominal, float(cap_total - SLOT_WIN))
    rel = slot - nominal
    return jnp.where((rel >= 0) & (rel < SLOT_WIN), rel + (nominal - begin), 255.0)


def _window_begin(first_s, k, cap_total):
    return pl.multiple_of(jnp.minimum(first_s + k * SLOT_WIN, cap_total - SLOT_WIN), SLOT_ALIGN)


TILES_PER_STEP = 2


def _dispatch_kernel(first_s, npass_s, slot_ref, first_ref, h_ref, o_ref, *, n_tiles):
    b, step = pl.program_id(0), pl.program_id(1)
    n_e = o_ref.shape[1]
    cap_total = o_ref.shape[2]

    @pl.when(step == 0)
    def _():
        o_ref[...] = jnp.zeros_like(o_ref)

    sub = lax.broadcasted_iota(jnp.int32, (SLOT_WIN, TILE), 0).astype(F32)
    eye = (lax.broadcasted_iota(jnp.int32, (n_e, N_EXPERTS), 0)
           == lax.broadcasted_iota(jnp.int32, (n_e, N_EXPERTS), 1)).astype(BF16)

    def one_tile(i, t):
        rows_in = slice(i * TILE, (i + 1) * TILE)
        slot = slot_ref[0, rows_in, :]
        first = first_ref[0, i]

        def one_pass(k, carry):
            rel = _window_rel(slot, first, k.astype(F32), cap_total).astype(BF16)
            rel_t = _dot_nt(eye, rel)
            onehot = jnp.concatenate(
                [(jnp.broadcast_to(rel_t[e:e + 1, :], (SLOT_WIN, TILE)) == sub).astype(BF16)
                 for e in range(n_e)], axis=0)
            rows = _dot(onehot, h_ref[0, rows_in, :])
            for e in range(n_e):
                begin = _window_begin(first_s[(b * n_tiles + t) * N_EXPERTS + e], k, cap_total)
                o_ref[0, e, pl.ds(begin, SLOT_WIN), :] += (
                    rows[e * SLOT_WIN:(e + 1) * SLOT_WIN].astype(BF16))
            return carry

        lax.fori_loop(0, npass_s[b * n_tiles + t], one_pass, 0)

    for i in range(TILES_PER_STEP):
        t = step * TILES_PER_STEP + i
        pl.when(t < n_tiles)(functools.partial(one_tile, i, t))


def _dispatch(hx, slot, first, n_pass, cap_total):
    B, S, W = hx.shape
    nt = S // TILE
    rows = TILES_PER_STEP * TILE
    grid_spec = pltpu.PrefetchScalarGridSpec(
        num_scalar_prefetch=2,
        grid=(B, pl.cdiv(nt, TILES_PER_STEP)),
        in_specs=[pl.BlockSpec((1, rows, N_EXPERTS), lambda b, t, *_: (b, t, 0)),
                  pl.BlockSpec((1, TILES_PER_STEP, 1, N_EXPERTS), lambda b, t, *_: (b, t, 0, 0)),
                  pl.BlockSpec((1, rows, W), lambda b, t, *_: (b, t, 0))],
        out_specs=pl.BlockSpec((1, N_EXPERTS, cap_total, W), lambda b, t, *_: (b, 0, 0, 0)),
    )
    return pl.pallas_call(
        functools.partial(_dispatch_kernel, n_tiles=nt),
        out_shape=jax.ShapeDtypeStruct((B, N_EXPERTS, cap_total, W), BF16),
        grid_spec=grid_spec,
        compiler_params=_params(2),
        name="dispatch",
    )(first.reshape(-1), n_pass.reshape(-1), slot,
      first.astype(F32).reshape(B, nt, 1, N_EXPERTS), hx)


def _ffn_kernel(x_ref, wg_ref, wu_ref, wd_ref, o_ref):
    D = wg_ref.shape[2]
    ff = wg_ref.shape[3]
    x = x_ref[0, 0, :, :D]
    pieces = x_ref[0, 0, :, D:].astype(F32)
    lane = lax.broadcasted_iota(jnp.int32, pieces.shape, 1)
    mine = ((lane & (N_EXPERTS - 1)) == pl.program_id(0)) & (lane < 3 * N_EXPERTS)
    gate = jnp.sum(jnp.where(mine, pieces, 0.0), axis=-1, keepdims=True)
    chunk = 512
    acc = jnp.zeros((x.shape[0], D), F32)
    for j in range(ff // chunk):
        cols = slice(j * chunk, (j + 1) * chunk)
        g = _dot(x, wg_ref[0, 0, :, cols])
        u = _dot(x, wu_ref[0, 0, :, cols])
        hid = (g * jax.nn.sigmoid(g) * u).astype(BF16)
        acc = acc + _dot(hid, wd_ref[0, 0, cols, :])
    o_ref[0, 0] = (acc * gate).astype(BF16)


def _ffn(xin, wg, wu, wd, layer):
    B, E, S, W = xin.shape
    D, FF = wg.shape[2], wg.shape[3]
    return pl.pallas_call(
        _ffn_kernel,
        out_shape=jax.ShapeDtypeStruct((B, E, S, D), BF16),
        grid=(E, B),
        in_specs=[pl.BlockSpec((1, 1, S, W), lambda e, b: (b, e, 0, 0)),
                  pl.BlockSpec((1, 1, D, FF), lambda e, b: (layer, e, 0, 0)),
                  pl.BlockSpec((1, 1, D, FF), lambda e, b: (layer, e, 0, 0)),
                  pl.BlockSpec((1, 1, FF, D), lambda e, b: (layer, e, 0, 0))],
        out_specs=pl.BlockSpec((1, 1, S, D), lambda e, b: (b, e, 0, 0)),
        compiler_params=_params(2),
        name="expert_ffn",
    )(xin, wg, wu, wd)


def _combine_kernel(first_s, npass_s, slot_ref, first_ref, expand_ref, y_ref, x_ref, g2_ref,
                    *rest, n_tiles, out_tiles):
    o_ref = rest[-1]
    b, step = pl.program_id(0), pl.program_id(1)
    cap_total = y_ref.shape[2]
    lane_row = (lax.broadcasted_iota(jnp.int32, (TILE, N_EXPERTS * SLOT_WIN), 1)
                & (SLOT_WIN - 1)).astype(F32)

    def one_tile(i, t):
        rows_out = slice(i * TILE, (i + 1) * TILE)
        o_ref[0, rows_out, :] = x_ref[0, rows_out, :]
        slot = slot_ref[0, rows_out, :]
        first = first_ref[0, i]
        g2 = jnp.where(t == n_tiles - 1, g2_ref[0, 1], g2_ref[0, 0])

        def one_pass(k, carry):
            rel = _window_rel(slot, first, k.astype(F32), cap_total).astype(BF16)
            onehot = (_dot(rel, expand_ref[...]) == lane_row).astype(BF16)
            rows = jnp.concatenate(
                [y_ref[0, e, pl.ds(_window_begin(first_s[(b * n_tiles + t) * N_EXPERTS + e], k,
                                                 cap_total), SLOT_WIN), :]
                 for e in range(N_EXPERTS)], axis=0)
            o_ref[0, rows_out, :] += g2 * _dot(onehot, rows)
            return carry

        lax.fori_loop(0, npass_s[b * n_tiles + t], one_pass, 0)
        if len(rest) == 2:
            x = o_ref[0, rows_out, :]
            ms = jnp.mean(x * x, axis=-1, keepdims=True)
            o_ref[0, rows_out, :] = x * lax.rsqrt(ms + RMS_EPS) * rest[0][...]

    for i in range(TILES_PER_STEP):
        t = step * TILES_PER_STEP + i
        pl.when(t < out_tiles)(functools.partial(one_tile, i, t))


def _combine(y, slot, first, n_pass, xa, gate2, final_g=None):
    B, S, D = xa.shape
    nt = S // TILE
    E, cap_total = y.shape[1], y.shape[2]
    rows = TILES_PER_STEP * TILE
    expand = np.repeat(np.eye(E, dtype=np.float32), SLOT_WIN, axis=1)
    in_specs = [pl.BlockSpec((1, rows, E), lambda b, t, *_: (b, t, 0)),
                pl.BlockSpec((1, TILES_PER_STEP, 1, E), lambda b, t, *_: (b, t, 0, 0)),
                pl.BlockSpec((E, E * SLOT_WIN), lambda b, t, *_: (0, 0)),
                pl.BlockSpec((1, E, cap_total, D), lambda b, t, *_: (b, 0, 0, 0)),
                pl.BlockSpec((1, rows, D), lambda b, t, *_: (b, t, 0)),
                pl.BlockSpec((1, 2, 1, D), lambda b, t, *_: (b, 0, 0, 0))]
    args = [slot, first.astype(F32).reshape(B, nt, 1, E), jnp.asarray(expand, BF16), y, xa, gate2]
    out_tiles = nt
    if final_g is not None:
        in_specs.append(pl.BlockSpec((1, D), lambda b, t, *_: (0, 0)))
        args.append(final_g)
        out_tiles = nt - 1
    grid_spec = pltpu.PrefetchScalarGridSpec(
        num_scalar_prefetch=2,
        grid=(B, pl.cdiv(out_tiles, TILES_PER_STEP)),
        in_specs=in_specs,
        out_specs=pl.BlockSpec((1, rows, D), lambda b, t, *_: (b, t, 0)),
    )
    return pl.pallas_call(
        functools.partial(_combine_kernel, n_tiles=nt, out_tiles=out_tiles),
        out_shape=jax.ShapeDtypeStruct((B, out_tiles * TILE, D), F32),
        grid_spec=grid_spec,
        compiler_params=_params(2),
        name="combine",
    )(first.reshape(-1), n_pass.reshape(-1), *args)


def _scale_q(w, nq):
    return jnp.concatenate([w[:, :nq] * (HEAD_DIM ** -0.5 * LOG2E), w[:, nq:]], axis=1)


def _win_weights(w_in):
    D = w_in.shape[0]
    nq = 16 * HEAD_DIM
    nk = A_KV_HEADS * HEAD_DIM
    w = _scale_q(w_in, nq)
    dup = lambda m: jnp.concatenate([m.reshape(D, A_KV_HEADS, 1, HEAD_DIM)] * 2,
                                    axis=2).reshape(D, 2 * nk)
    return jnp.concatenate([w[:, :nq], dup(w[:, nq:nq + nk]), dup(w[:, nq + nk:])],
                           axis=1).astype(BF16)


def _router_weights(rw):
    D, E = rw.shape
    return jnp.concatenate([rw, rw, rw, jnp.zeros((D, GATE_COLS - 3 * E), rw.dtype)], axis=1)


def kernel(x, c, ctx, c_ctx, ada_w, ada_b, norm1_g, norm2_g, final_g, win_w_in, win_w_out,
           win_sink, diff_w_in, diff_w_out, diff_lambda, diff_subln_g, na_w_in, na_w_out,
           na_rpb, router_w, w_gate, w_up, w_down):
    B, T, D = x.shape
    L = ctx.shape[1]
    S = T + L
    depth = ada_w.shape[0]
    assert L == TILE and T % ROW_TILE == 0 and (T // GRID_W) >= NA_WIN_ROWS
    cs = jnp.concatenate([c, c_ctx[None, :]], axis=0)
    rope = _rope_tables(T, L)
    wg_all, wu_all, wd_all = w_gate.astype(BF16), w_up.astype(BF16), w_down.astype(BF16)
    xa = None
    for i in range(depth):
        kind = i % N_MIXERS
        slot = i // N_MIXERS
        mod = _ada(cs, ada_w[i], ada_b[i][None, :])
        mod = jnp.stack([mod[:B], jnp.broadcast_to(mod[B:], (B, 6 * D))], axis=1)
        mod = mod.reshape(B, 2, 1, 6, D)
        sh1, sc1, g1, sh2, sc2, g2 = [mod[:, :, :, k, :] for k in range(6)]
        if kind == 0:
            w_in = _win_weights(win_w_in[slot])
            n_rope = (16 + 2 * A_KV_HEADS) * HEAD_DIM
            w_out = win_w_out[slot]
        elif kind == 1:
            w_in = _scale_q(diff_w_in[slot], 16 * HEAD_DIM).astype(BF16)
            n_rope = 32 * HEAD_DIM
            w_out = diff_w_out[slot]
        else:
            w_in = _scale_q(na_w_in[slot], 16 * HEAD_DIM).astype(BF16)
            n_rope = 0
            w_out = na_w_out[slot]
        if i == 0:
            qkv, xa = _norm_proj((x, ctx), norm1_g[i][None, :], sh1, sc1, w_in, rope, n_rope)
        else:
            qkv = _norm_proj((xa,), norm1_g[i][None, :], sh1, sc1, w_in, rope, n_rope)
        if kind == 0:
            y = _attn_win(qkv, win_sink[slot], T)
        elif kind == 1:
            lambda_init = 0.8 - 0.6 * math.exp(-0.3 * i)
            lp = diff_lambda[slot]
            lam = (jnp.exp(jnp.sum(lp[0] * lp[1])) - jnp.exp(jnp.sum(lp[2] * lp[3]))
                   + lambda_init).reshape(1)
            y = _attn_diff(qkv, lam, diff_subln_g[slot][None, :], T, 1.0 - lambda_init)
        else:
            y = _attn_na(qkv, _na_bias_tables(na_rpb[slot], T), _na_bias_bound(na_rpb[slot]), T)
        xa, hx, aff = _out_router(y, w_out.astype(BF16), xa, g1, norm2_g[i][None, :],
                                  sh2, sc2, _router_weights(router_w[i]))
        tok_slot, start, cnt = _route(aff, T)
        first, n_pass = _slot_plan(start, cnt)
        cap_total = CAPACITY_FACTOR * S // N_EXPERTS
        xin = _dispatch(hx, tok_slot, first, n_pass, cap_total)
        ye = _ffn(xin, wg_all, wu_all, wd_all, i)
        last = i == depth - 1
        xa = _combine(ye, tok_slot, first, n_pass, xa, g2, final_g[None, :] if last else None)
    return xa
```

```python
import functools
import math

import numpy as np
import jax
import jax.numpy as jnp
from jax import lax
from jax.experimental import pallas as pl
from jax.experimental.pallas import tpu as pltpu

HEAD_DIM = 64
LANES = 128
GRID_W = 64
NA_ROWS = 8
NA_COLS = 16
A_WINDOW = 128
A_KV_HEADS = 4
N_MIXERS = 3
ROPE_BASE = 10000.0
ROPE_AXIS_DIM = HEAD_DIM // 2
N_EXPERTS = 16
CAPACITY_FACTOR = 2
RMS_EPS = 1e-6
NEG_INF = -1e30
LOG2E = math.log2(math.e)
TILE = 256
ROW_TILE = 2 * TILE
NA_TILE_ROWS = TILE // GRID_W
NA_WIN_ROWS = NA_TILE_ROWS + NA_ROWS
VMEM_LIMIT = 56 * 1024 * 1024
SLOT_WIN = 64
SLOT_ALIGN = 16
GATE_COLS = LANES

BF16 = jnp.bfloat16
F32 = jnp.float32


def _params(n_grid):
    return pltpu.CompilerParams(
        dimension_semantics=("arbitrary",) * n_grid, vmem_limit_bytes=VMEM_LIMIT)


def _split_bf16(a):
    hi = a.astype(BF16)
    lo = (a - hi.astype(F32)).astype(BF16)
    return hi, lo


def _dot(a, b):
    return jnp.dot(a, b, preferred_element_type=F32)


def _dot_nt(a, b):
    return lax.dot_general(a, b, (((1,), (1,)), ((), ())), preferred_element_type=F32)


def _dot_tn(a, b):
    return lax.dot_general(a, b, (((0,), (0,)), ((), ())), preferred_element_type=F32)


def _ada_kernel(c_ref, w_ref, b_ref, o_ref):
    c = c_ref[...]
    a = c * jax.nn.sigmoid(c)
    a_hi, a_lo = _split_bf16(a)
    w_hi, w_lo = _split_bf16(w_ref[...])
    o_ref[...] = _dot(a_hi, w_hi) + _dot(a_lo, w_hi) + _dot(a_hi, w_lo) + b_ref[...]


def _ada(cs, w, b):
    R, D = cs.shape
    N = w.shape[1]
    tn = 1024
    return pl.pallas_call(
        _ada_kernel,
        out_shape=jax.ShapeDtypeStruct((R, N), F32),
        grid=(N // tn,),
        in_specs=[pl.BlockSpec((R, D), lambda j: (0, 0)),
                  pl.BlockSpec((D, tn), lambda j: (0, j)),
                  pl.BlockSpec((1, tn), lambda j: (0, j))],
        out_specs=pl.BlockSpec((R, tn), lambda j: (0, j)),
        compiler_params=_params(1),
        name="ada",
    )(cs, w, b)


def _rms_mod(x, g, shift, scale):
    ms = jnp.mean(x * x, axis=-1, keepdims=True)
    y = x * lax.rsqrt(ms + RMS_EPS) * g
    return y * (1.0 + scale) + shift


def _norm_proj_kernel(*refs, n_rope, joins_streams):
    if joins_streams:
        x_ref, c_ref, g_ref, sh_ref, sc_ref, w_ref, cos_ref, sa_ref, sb_ref, o_ref, xa_ref = refs
        is_ctx = pl.program_id(1) == pl.num_programs(1) - 1
        ctx_rows = jnp.concatenate([c_ref[0]] * (ROW_TILE // TILE), axis=0)
        x = jnp.where(is_ctx, ctx_rows, x_ref[0])
        xa_ref[0] = x
    else:
        x_ref, g_ref, sh_ref, sc_ref, w_ref, cos_ref, sa_ref, sb_ref, o_ref = refs
        x = x_ref[0]
    h = _rms_mod(x, g_ref[...], sh_ref[0, 0], sc_ref[0, 0]).astype(BF16)
    n_cols = w_ref.shape[1]
    chunk = 512
    for j in range(n_cols // chunk):
        acc = _dot(h, w_ref[:, j * chunk:(j + 1) * chunk])
        for t in range(chunk // LANES):
            col = j * chunk + t * LANES
            a = acc[:, t * LANES:(t + 1) * LANES]
            if col < n_rope:
                a = (a * cos_ref[...]
                     + pltpu.roll(a, LANES - 16, 1) * sa_ref[...]
                     + pltpu.roll(a, 16, 1) * sb_ref[...])
            o_ref[0, :, col:col + LANES] = a.astype(BF16)


def _norm_proj(streams, g, shift, scale, w, rope, n_rope):
    joins = len(streams) == 2
    B, _, D = streams[0].shape
    S = sum(a.shape[1] for a in streams)
    N = w.shape[1]
    nt = pl.cdiv(S, ROW_TILE)
    mod_spec = pl.BlockSpec((1, 1, 1, D), lambda b, t: (b, t // (nt - 1), 0, 0))
    rope_spec = pl.BlockSpec((ROW_TILE, LANES), lambda b, t: (t, 0))
    tile_spec = pl.BlockSpec((1, ROW_TILE, D), lambda b, t: (b, t, 0))
    if joins:
        stream_specs = [pl.BlockSpec((1, ROW_TILE, D), lambda b, t: (b, jnp.minimum(t, nt - 2), 0)),
                        pl.BlockSpec((1, TILE, D), lambda b, t: (b, 0, 0))]
    else:
        stream_specs = [tile_spec]
    qkv_shape = jax.ShapeDtypeStruct((B, S, N), BF16)
    qkv_spec = pl.BlockSpec((1, ROW_TILE, N), lambda b, t: (b, t, 0))
    return pl.pallas_call(
        functools.partial(_norm_proj_kernel, n_rope=n_rope, joins_streams=joins),
        out_shape=(qkv_shape, jax.ShapeDtypeStruct((B, S, D), F32)) if joins else qkv_shape,
        grid=(B, nt),
        in_specs=stream_specs + [pl.BlockSpec((1, D), lambda b, t: (0, 0)),
                                 mod_spec, mod_spec,
                                 pl.BlockSpec((D, N), lambda b, t: (0, 0)),
                                 rope_spec, rope_spec, rope_spec],
        out_specs=(qkv_spec, tile_spec) if joins else qkv_spec,
        compiler_params=_params(2),
        name="norm_proj",
    )(*streams, g, shift, scale, w, *rope)


def _rope_tables(T, L):
    t = np.arange(T)
    pos = np.stack([t // GRID_W, t % GRID_W], axis=0).astype(np.float32)
    inv = (1.0 / (ROPE_BASE ** (np.arange(0, ROPE_AXIS_DIM, 2, dtype=np.float32)
                                / ROPE_AXIS_DIM))).astype(np.float32)
    d = np.arange(LANES) % HEAD_DIM
    axis = d // ROPE_AXIS_DIM
    half = (d % ROPE_AXIS_DIM) // (ROPE_AXIS_DIM // 2)
    freq = d % (ROPE_AXIS_DIM // 2)
    ang = jnp.asarray(pos[axis].T) * jnp.asarray(inv[freq])[None, :]
    cos = jnp.cos(ang)
    sin = jnp.sin(ang)
    first = jnp.asarray(half == 0)[None, :]
    sa = jnp.where(first, -sin, 0.0)
    sb = jnp.where(first, 0.0, sin)
    pad = lambda a, v: jnp.concatenate([a, jnp.full((L, LANES), v, F32)], axis=0)
    return pad(cos, 1.0), pad(sa, 0.0), pad(sb, 0.0)


def _half_masks(q):
    lane = lax.broadcasted_iota(jnp.int32, q.shape, 1)
    zero = jnp.zeros_like(q)
    return jnp.where(lane < HEAD_DIM, q, zero), jnp.where(lane >= HEAD_DIM, q, zero)


def _merge_halves(o_first, o_second):
    lane = lax.broadcasted_iota(jnp.int32, o_first.shape, 1)
    return jnp.where(lane < HEAD_DIM, o_first, o_second)


SHIFT_SLACK = 1.02
L_MIN = 2.0 ** -64


def _sq_norms_row(x):
    xf = x.astype(F32)
    return _dot_nt(jnp.ones((8, LANES), BF16), (xf * xf).astype(BF16))[:1]


def _head_bounds_row(qs):
    qf = qs.astype(F32)
    n2 = jnp.sum(qf * qf, axis=-1, keepdims=True)
    return jnp.concatenate(
        [jnp.broadcast_to(jnp.max(n2[r:r + TILE], axis=0, keepdims=True), (1, TILE))
         for r in range(0, qs.shape[0], TILE)], axis=1)


def _store_key_bound(kmax_ref, j, k, scale=1.0):
    n2 = jnp.max(_sq_norms_row(k), axis=1, keepdims=True) * scale
    kmax_ref[j] = jnp.broadcast_to(n2, kmax_ref.shape[1:])


def _redo_if_underflow(attend, *args):
    den_min = attend(*args, exact=False)

    @pl.when(den_min[0, 0] < L_MIN)
    def _():
        attend(*args, exact=True)


WIN_KV_PER_STEP = 4


def _win_kernel(sink_ref, q_ref, k_ref, v_ref, o_ref, kmax_ref, *, T, win):
    hb = pl.program_id(1)
    t = pl.program_id(2)
    n_x = T // TILE
    is_x = t < n_x
    heads = [slice(j * LANES, (j + 1) * LANES) for j in range(WIN_KV_PER_STEP)]

    @pl.when(t == 0)
    def _():
        for j, cols in enumerate(heads):
            _store_key_bound(kmax_ref, j, k_ref[0, :, cols], 0.5)

    start = pl.multiple_of(jnp.clip(t * TILE - A_WINDOW, 0, T - win), LANES)
    kpos = start + lax.broadcasted_iota(jnp.int32, (win, TILE), 0)
    qpos = t * TILE + lax.broadcasted_iota(jnp.int32, (win, TILE), 1)
    band = (jnp.abs(qpos - kpos) <= A_WINDOW) & is_x
    band = jnp.concatenate([band] * 4, axis=1)

    def attend(exact):
        queries, scores = [], []
        for j, cols in enumerate(heads):
            q = q_ref[0, :, 2 * j * LANES:2 * (j + 1) * LANES]
            qa, qb = _half_masks(q[:, :LANES])
            qc, qd = _half_masks(q[:, LANES:])
            qs = jnp.concatenate([qa, qb, qc, qd], axis=0)
            queries.append(qs)
            scores.append((_dot_nt(k_ref[0, T:, cols], qs),
                           _dot_nt(k_ref[0, pl.ds(start, win), cols], qs)))
        den_min = None
        for j, cols in enumerate(heads):
            s_c, s_w = scores[j]
            head0 = (hb * WIN_KV_PER_STEP + j) * 4
            s_w = jnp.where(band, s_w, NEG_INF)
            sink = jnp.concatenate(
                [jnp.full((1, TILE), sink_ref[head0 + g] * LOG2E, F32) for g in range(4)], axis=1)
            if exact:
                m = jnp.maximum(jnp.max(s_c, axis=0, keepdims=True),
                                jnp.max(s_w, axis=0, keepdims=True))
            else:
                m = jnp.sqrt(_head_bounds_row(queries[j]) * kmax_ref[j, :1, :1]) * SHIFT_SLACK
            m = jnp.maximum(m, sink)
            ec = jnp.exp2(s_c - m)
            ew = jnp.exp2(s_w - m)
            den = (jnp.sum(ec, axis=0, keepdims=True) + jnp.sum(ew, axis=0, keepdims=True)
                   + jnp.exp2(sink - m))
            o = ((_dot_tn(v_ref[0, T:, cols], ec.astype(BF16))
                  + _dot_tn(v_ref[0, pl.ds(start, win), cols], ew.astype(BF16))) * (1.0 / den)).T
            first = 2 * j * LANES
            o_ref[0, :, first:first + LANES] = _merge_halves(
                o[:TILE], o[TILE:2 * TILE]).astype(BF16)
            o_ref[0, :, first + LANES:first + 2 * LANES] = _merge_halves(
                o[2 * TILE:3 * TILE], o[3 * TILE:]).astype(BF16)
            low = jnp.min(den, axis=1, keepdims=True)
            den_min = low if den_min is None else jnp.minimum(den_min, low)
        return den_min

    _redo_if_underflow(attend)


def _attn_win(qkv, sink, T):
    B, S, _ = qkv.shape
    nt = S // TILE
    nq = 16 * HEAD_DIM
    win = TILE + 2 * A_WINDOW
    hb = A_KV_HEADS // WIN_KV_PER_STEP
    wq = WIN_KV_PER_STEP * 2 * LANES
    wk = WIN_KV_PER_STEP * LANES
    kb = nq // wk
    return pl.pallas_call(
        functools.partial(_win_kernel, T=T, win=win),
        out_shape=jax.ShapeDtypeStruct((B, S, nq), BF16),
        grid=(B, hb, nt),
        in_specs=[pl.BlockSpec(memory_space=pltpu.SMEM),
                  pl.BlockSpec((1, TILE, wq), lambda b, h, t: (b, t, h)),
                  pl.BlockSpec((1, S, wk), lambda b, h, t: (b, 0, kb + h)),
                  pl.BlockSpec((1, S, wk), lambda b, h, t: (b, 0, kb + hb + h))],
        out_specs=pl.BlockSpec((1, TILE, wq), lambda b, h, t: (b, t, h)),
        scratch_shapes=[pltpu.VMEM((WIN_KV_PER_STEP, 8, LANES), F32)],
        compiler_params=_params(3),
        name="attn_win",
    )(sink, qkv, qkv, qkv)


DIFF_HEADS_PER_STEP = 2


def _diff_kernel(lam_ref, q_ref, k_ref, v_ref, g_ref, o_ref, kmax_ref, *, T, out_scale):
    t = pl.program_id(2)
    n_x = T // TILE
    lam = lam_ref[0]
    heads = [slice(j * LANES, (j + 1) * LANES) for j in range(DIFF_HEADS_PER_STEP)]

    @pl.when(t == 0)
    def _():
        for j, cols in enumerate(heads):
            k1, k2 = _half_masks(k_ref[0, :, cols])
            _store_key_bound(kmax_ref, 2 * j, k1)
            _store_key_bound(kmax_ref, 2 * j + 1, k2)

    def attend(keys, exact):
        queries, scores = [], []
        for cols in heads:
            qa, qb = _half_masks(q_ref[0, :, cols])
            qs = jnp.concatenate([qa, qb], axis=0)
            queries.append(qs)
            scores.append(_dot_nt(qs, k_ref[0, keys, cols]))
        l_min = None
        for j, cols in enumerate(heads):
            s = scores[j]
            if exact:
                m = jnp.max(s, axis=-1, keepdims=True)
            else:
                qf = queries[j].astype(F32)
                row = lax.broadcasted_iota(jnp.int32, (2 * TILE, 1), 0)
                kmax = jnp.where(row < TILE, kmax_ref[2 * j, :1, :1], kmax_ref[2 * j + 1, :1, :1])
                m = jnp.sqrt(jnp.sum(qf * qf, axis=-1, keepdims=True) * kmax) * SHIFT_SLACK
            e = jnp.exp2(s - m)
            l = jnp.sum(e, axis=-1, keepdims=True)
            a = e[:TILE] - e[TILE:] * (lam * l[:TILE] / l[TILE:])
            o = _dot(a.astype(BF16), v_ref[0, keys, cols]) / l[:TILE]
            ms = jnp.mean(o * o, axis=-1, keepdims=True)
            y = o * lax.rsqrt(ms + RMS_EPS) * g_ref[...] * out_scale
            o_ref[0, :, cols] = y.astype(BF16)
            low = jnp.min(l, axis=0, keepdims=True)
            l_min = low if l_min is None else jnp.minimum(l_min, low)
        return l_min

    @pl.when(t < n_x)
    def _():
        _redo_if_underflow(attend, slice(None))

    @pl.when(t >= n_x)
    def _():
        _redo_if_underflow(attend, slice(T, None))


def _attn_diff(qkv, lam, subln_g, T, out_scale):
    B, S, _ = qkv.shape
    nt = S // TILE
    H = 8
    hb = H // DIFF_HEADS_PER_STEP
    w = DIFF_HEADS_PER_STEP * LANES
    return pl.pallas_call(
        functools.partial(_diff_kernel, T=T, out_scale=out_scale),
        out_shape=jax.ShapeDtypeStruct((B, S, H * LANES), BF16),
        grid=(B, hb, nt),
        in_specs=[pl.BlockSpec(memory_space=pltpu.SMEM),
                  pl.BlockSpec((1, TILE, w), lambda b, h, t: (b, t, h)),
                  pl.BlockSpec((1, S, w), lambda b, h, t: (b, 0, hb + h)),
                  pl.BlockSpec((1, S, w), lambda b, h, t: (b, 0, 2 * hb + h)),
                  pl.BlockSpec((1, LANES), lambda b, h, t: (0, 0))],
        out_specs=pl.BlockSpec((1, TILE, w), lambda b, h, t: (b, t, h)),
        scratch_shapes=[pltpu.VMEM((2 * DIFF_HEADS_PER_STEP, 8, LANES), F32)],
        compiler_params=_params(3),
        name="attn_diff",
    )(lam, qkv, qkv, qkv, subln_g)


NA_PAIRS_PER_STEP = 2


def _na_kernel(q_ref, k_ref, v_ref, bias_ref, bmax_ref, o_ref, kmax_ref, *, T):
    t = pl.program_id(2)
    n_x = T // TILE
    rows = T // GRID_W
    win = NA_WIN_ROWS * GRID_W
    pairs = [slice(j * LANES, (j + 1) * LANES) for j in range(NA_PAIRS_PER_STEP)]

    @pl.when(t == 0)
    def _():
        for j, cols in enumerate(pairs):
            k1, k2 = _half_masks(k_ref[0, :, cols])
            _store_key_bound(kmax_ref, 2 * j, k1)
            _store_key_bound(kmax_ref, 2 * j + 1, k2)

    cls = jnp.where(t >= n_x, 3, jnp.where(t == 0, 0, jnp.where(t == n_x - 1, 2, 1)))
    row0 = jnp.clip(t * NA_TILE_ROWS - NA_ROWS // 2, 0, rows - NA_WIN_ROWS)
    start = pl.multiple_of(row0 * GRID_W, GRID_W)

    def attend(exact):
        queries, scores = [], []
        for cols in pairs:
            qa, qb = _half_masks(q_ref[0, :, cols])
            qs = jnp.concatenate([qa, qb], axis=0)
            queries.append(qs)
            scores.append((_dot_nt(k_ref[0, T:, cols], qs),
                           _dot_nt(k_ref[0, pl.ds(start, win), cols], qs)))
        den_min = None
        for j, cols in enumerate(pairs):
            s_c, s_w = scores[j]
            s_w = s_w + bias_ref[cls, j]
            if exact:
                m = jnp.maximum(jnp.max(s_c, axis=0, keepdims=True),
                                jnp.max(s_w, axis=0, keepdims=True))
            else:
                lane = lax.broadcasted_iota(jnp.int32, (1, 2 * TILE), 1)
                kmax = jnp.where(lane < TILE, kmax_ref[2 * j, :1, :1], kmax_ref[2 * j + 1, :1, :1])
                m = jnp.sqrt(_head_bounds_row(queries[j]) * kmax) * SHIFT_SLACK + bmax_ref[j]
            ec = jnp.exp2(s_c - m)
            ew = jnp.exp2(s_w - m)
            den = jnp.sum(ec, axis=0, keepdims=True) + jnp.sum(ew, axis=0, keepdims=True)
            o = ((_dot_tn(v_ref[0, T:, cols], ec.astype(BF16))
                  + _dot_tn(v_ref[0, pl.ds(start, win), cols], ew.astype(BF16))) * (1.0 / den)).T
            o_ref[0, :, cols] = _merge_halves(o[:TILE], o[TILE:]).astype(BF16)
            low = jnp.min(den, axis=1, keepdims=True)
            den_min = low if den_min is None else jnp.minimum(den_min, low)
        return den_min

    _redo_if_underflow(attend)


def _attn_na(qkv, bias, bias_max, T):
    B, S, _ = qkv.shape
    nt = S // TILE
    n = NA_PAIRS_PER_STEP
    hb = 8 // n
    w = n * LANES
    win = NA_WIN_ROWS * GRID_W
    return pl.pallas_call(
        functools.partial(_na_kernel, T=T),
        out_shape=jax.ShapeDtypeStruct((B, S, 8 * LANES), BF16),
        grid=(hb, B, nt),
        in_specs=[pl.BlockSpec((1, TILE, w), lambda h, b, t: (b, t, h)),
                  pl.BlockSpec((1, S, w), lambda h, b, t: (b, 0, hb + h)),
                  pl.BlockSpec((1, S, w), lambda h, b, t: (b, 0, 2 * hb + h)),
                  pl.BlockSpec((4, n, win, 2 * TILE), lambda h, b, t: (0, h, 0, 0)),
                  pl.BlockSpec((n, 1, 2 * TILE), lambda h, b, t: (h, 0, 0))],
        out_specs=pl.BlockSpec((1, TILE, w), lambda h, b, t: (b, t, h)),
        scratch_shapes=[pltpu.VMEM((2 * NA_PAIRS_PER_STEP, 8, LANES), F32)],
        compiler_params=_params(3),
        name="attn_na",
    )(qkv, qkv, qkv, bias, bias_max)


def _na_bias_bound(rpb):
    top = jnp.maximum(jnp.max(rpb, axis=(1, 2)), 0.0) * LOG2E
    return jnp.repeat(top.reshape(-1, 2), TILE, axis=1)[:, None, :]


def _na_bias_tables(rpb, T):
    rows = T // GRID_W
    H = rpb.shape[0]
    win = NA_WIN_ROWS * GRID_W
    hi = lax.Precision.HIGHEST
    pick = lambda idx, n: jnp.asarray(idx[..., None] == np.arange(n), F32)
    c = np.arange(GRID_W)
    cs = np.clip(c - NA_COLS // 2, 0, GRID_W - NA_COLS)
    valid_c = (c[None, :] >= cs[:, None]) & (c[None, :] < cs[:, None] + NA_COLS)
    bidx_c = np.clip(c[None, :] - c[:, None] + NA_COLS - 1, 0, 2 * NA_COLS - 2)
    by_col = jnp.einsum('hrd,ckd->hrck', rpb, pick(bidx_c, 2 * NA_COLS - 1), precision=hi)
    i = np.arange(NA_TILE_ROWS)
    j = np.arange(NA_WIN_ROWS)
    tables = []
    for r0 in (0, NA_TILE_ROWS, rows - NA_TILE_ROWS):
        s = int(np.clip(r0 - NA_ROWS // 2, 0, rows - NA_WIN_ROWS))
        r = r0 + i
        rs = np.clip(r - NA_ROWS // 2, 0, rows - NA_ROWS)
        kr = s + j
        valid_r = (kr[None, :] >= rs[:, None]) & (kr[None, :] < rs[:, None] + NA_ROWS)
        bidx_r = np.clip(kr[None, :] - r[:, None] + NA_ROWS - 1, 0, 2 * NA_ROWS - 2)
        vals = jnp.einsum('hrck,ijr->hjkic', by_col, pick(bidx_r, 2 * NA_ROWS - 1), precision=hi)
        valid = valid_r.T[:, None, :, None] & valid_c.T[None, :, None, :]
        vals = jnp.where(jnp.asarray(valid)[None], vals, NEG_INF).reshape(H // 2, 2, win, TILE)
        tables.append(vals.transpose(0, 2, 1, 3).reshape(H // 2, win, 2 * TILE))
    tables.append(jnp.full_like(tables[0], NEG_INF))
    return jnp.stack(tables, axis=0).astype(F32) * LOG2E


def _out_router_kernel(y_ref, w_ref, x_ref, g1_ref, n2_ref, sh_ref, sc_ref, rw_ref,
                       xo_ref, h_ref, aff_ref):
    D = x_ref.shape[2]
    x = x_ref[0] + g1_ref[0, 0] * _dot(y_ref[0], w_ref[...])
    xo_ref[0] = x
    h = _rms_mod(x, n2_ref[...], sh_ref[0, 0], sc_ref[0, 0])
    h_hi, h_lo = _split_bf16(h)
    n = h.shape[0]
    r_hi_lo = jnp.concatenate(_split_bf16(rw_ref[...]), axis=1)
    prod = _dot(jnp.concatenate([h_hi, h_lo], axis=0), r_hi_lo)
    logits = prod[:n, :GATE_COLS] + prod[:n, GATE_COLS:] + prod[n:, :GATE_COLS]
    lane = lax.broadcasted_iota(jnp.int32, logits.shape, 1)
    first = lane < N_EXPERTS
    m = jnp.max(jnp.where(first, logits, -jnp.inf), axis=-1, keepdims=True)
    e = jnp.exp(logits - m)
    aff = e / jnp.sum(jnp.where(first, e, 0.0), axis=-1, keepdims=True)
    aff_ref[0] = aff
    hi = aff.astype(BF16)
    rem = aff - hi.astype(F32)
    mid = rem.astype(BF16)
    lo = (rem - mid.astype(F32)).astype(BF16)
    zero = jnp.zeros_like(hi)
    pieces = jnp.where(first, hi, jnp.where(lane < 2 * N_EXPERTS, mid,
                                            jnp.where(lane < 3 * N_EXPERTS, lo, zero)))
    h_ref[0, :, :D] = h_hi
    h_ref[0, :, D:] = pieces


def _out_router(y, w_out, xa, gate1, n2g, shift2, scale2, rw3):
    B, S, D = xa.shape
    nt = pl.cdiv(S, ROW_TILE)
    mod_spec = pl.BlockSpec((1, 1, 1, D), lambda b, t: (b, t // (nt - 1), 0, 0))
    tile_spec = pl.BlockSpec((1, ROW_TILE, D), lambda b, t: (b, t, 0))
    return pl.pallas_call(
        _out_router_kernel,
        out_shape=(jax.ShapeDtypeStruct((B, S, D), F32),
                   jax.ShapeDtypeStruct((B, S, D + GATE_COLS), BF16),
                   jax.ShapeDtypeStruct((B, S, GATE_COLS), F32)),
        grid=(B, nt),
        in_specs=[tile_spec,
                  pl.BlockSpec((D, D), lambda b, t: (0, 0)),
                  tile_spec, mod_spec,
                  pl.BlockSpec((1, D), lambda b, t: (0, 0)),
                  mod_spec, mod_spec,
                  pl.BlockSpec((D, GATE_COLS), lambda b, t: (0, 0))],
        out_specs=(tile_spec,
                   pl.BlockSpec((1, ROW_TILE, D + GATE_COLS), lambda b, t: (b, t, 0)),
                   pl.BlockSpec((1, ROW_TILE, GATE_COLS), lambda b, t: (b, t, 0))),
        compiler_params=_params(2),
        name="out_router",
    )(y, w_out, xa, gate1, n2g, shift2, scale2, rw3)


def _route_kernel(aff_ref, slot_ref, start_ref, cnt_ref, *, T, cap_x, cap_c):
    S = aff_ref.shape[1]
    aff = aff_ref[0]
    bits = lax.bitcast_convert_type(aff[:, :N_EXPERTS], jnp.int32)
    ri = lax.broadcasted_iota(jnp.int32, (TILE, TILE), 0)
    ci = lax.broadcasted_iota(jnp.int32, (TILE, TILE), 1)
    ltri = (ri > ci).astype(BF16)
    eye = (lax.broadcasted_iota(jnp.int32, (N_EXPERTS, N_EXPERTS), 0)
           == lax.broadcasted_iota(jnp.int32, (N_EXPERTS, N_EXPERTS), 1))

    def count(mask):
        return jnp.sum(mask.astype(F32), axis=0, keepdims=True)

    for lo_row, hi_row, cap, base in ((0, T, cap_x, 0), (T, S, cap_c, cap_x)):
        b = bits[lo_row:hi_row]
        dense = lax.bitcast_convert_type(jnp.concatenate(
            [aff[r:r + TILE].T[:N_EXPERTS] for r in range(lo_row, hi_row, TILE)], axis=1),
            jnp.int32)

        def step(i, thr):
            cand = thr | lax.shift_left(jnp.int32(1), 30 - i)
            n_ge = jnp.sum((dense >= cand).astype(F32), axis=1, keepdims=True)
            return jnp.where(n_ge >= cap, cand, thr)

        thr = lax.fori_loop(0, 31, step, jnp.zeros((N_EXPERTS, 1), jnp.int32))
        thr = jnp.max(jnp.where(eye, jnp.broadcast_to(thr, eye.shape), 0), axis=0,
                      keepdims=True)
        need = cap - count(b > thr)
        seen_eq = jnp.zeros((1, N_EXPERTS), F32)
        seen = jnp.zeros((1, N_EXPERTS), F32)
        for j in range((hi_row - lo_row) // TILE):
            blk = b[j * TILE:(j + 1) * TILE]
            gt = blk > thr
            eq = blk == thr
            eq_rank = _dot(ltri, eq.astype(BF16)) + seen_eq
            sel = gt | (eq & (eq_rank < need))
            pos = _dot(ltri, sel.astype(BF16)) + seen + base
            t = lo_row // TILE + j
            slot_ref[0, t * TILE:(t + 1) * TILE, :] = jnp.where(sel, pos, -1.0)
            n_sel = count(sel)
            start_ref[0, t:t + 1, :] = seen + base
            cnt_ref[0, t:t + 1, :] = n_sel
            seen_eq = seen_eq + count(eq)
            seen = seen + n_sel


def _route(aff, T):
    B, S, _ = aff.shape
    E = N_EXPERTS
    nt = S // TILE
    cap_x = CAPACITY_FACTOR * T // N_EXPERTS
    cap_c = CAPACITY_FACTOR * (S - T) // N_EXPERTS
    plan = jax.ShapeDtypeStruct((B, nt, E), F32)
    return pl.pallas_call(
        functools.partial(_route_kernel, T=T, cap_x=cap_x, cap_c=cap_c),
        out_shape=(jax.ShapeDtypeStruct((B, S, E), F32), plan, plan),
        grid=(B,),
        in_specs=[pl.BlockSpec((1, S, GATE_COLS), lambda b: (b, 0, 0))],
        out_specs=(pl.BlockSpec((1, S, E), lambda b: (b, 0, 0)),
                   pl.BlockSpec((1, nt, E), lambda b: (b, 0, 0)),
                   pl.BlockSpec((1, nt, E), lambda b: (b, 0, 0))),
        compiler_params=_params(1),
        name="route",
    )(aff)


def _slot_plan(start, cnt):
    start = start.astype(jnp.int32)
    cnt = cnt.astype(jnp.int32)
    first = (start // SLOT_ALIGN) * SLOT_ALIGN
    n_pass = jnp.max((start - first + cnt + SLOT_WIN - 1) // SLOT_WIN, axis=-1)
    return first, n_pass.astype(jnp.int32)


def _window_rel(slot, first, k, cap_total):
    nominal = first + k * SLOT_WIN
    begin = jnp.minimum(nominal, float(cap_total - SLOT_WIN))
    rel = slot - nominal
    return jnp.where((rel >= 0) & (rel < SLOT_WIN), rel + (nominal - begin), 255.0)


def _window_begin(first_s, k, cap_total):
    return pl.multiple_of(jnp.minimum(first_s + k * SLOT_WIN, cap_total - SLOT_WIN), SLOT_ALIGN)


TILES_PER_STEP = 2


def _dispatch_kernel(first_s, npass_s, slot_ref, first_ref, h_ref, o_ref, *, n_tiles):
    b, step = pl.program_id(0), pl.program_id(1)
    n_e = o_ref.shape[1]
    cap_total = o_ref.shape[2]

    @pl.when(step == 0)
    def _():
        o_ref[...] = jnp.zeros_like(o_ref)

    sub = lax.broadcasted_iota(jnp.int32, (SLOT_WIN, TILE), 0).astype(F32)
    eye = (lax.broadcasted_iota(jnp.int32, (n_e, N_EXPERTS), 0)
           == lax.broadcasted_iota(jnp.int32, (n_e, N_EXPERTS), 1)).astype(BF16)

    def one_tile(i, t):
        rows_in = slice(i * TILE, (i + 1) * TILE)
        slot = slot_ref[0, rows_in, :]
        first = first_ref[0, i]

        def one_pass(k, carry):
            rel = _window_rel(slot, first, k.astype(F32), cap_total).astype(BF16)
            rel_t = _dot_nt(eye, rel)
            onehot = jnp.concatenate(
                [(jnp.broadcast_to(rel_t[e:e + 1, :], (SLOT_WIN, TILE)) == sub).astype(BF16)
                 for e in range(n_e)], axis=0)
            rows = _dot(onehot, h_ref[0, rows_in, :])
            for e in range(n_e):
                begin = _window_begin(first_s[(b * n_tiles + t) * N_EXPERTS + e], k, cap_total)
                o_ref[0, e, pl.ds(begin, SLOT_WIN), :] += (
                    rows[e * SLOT_WIN:(e + 1) * SLOT_WIN].astype(BF16))
            return carry

        lax.fori_loop(0, npass_s[b * n_tiles + t], one_pass, 0)

    for i in range(TILES_PER_STEP):
        t = step * TILES_PER_STEP + i
        pl.when(t < n_tiles)(functools.partial(one_tile, i, t))


def _dispatch(hx, slot, first, n_pass, cap_total):
    B, S, W = hx.shape
    nt = S // TILE
    rows = TILES_PER_STEP * TILE
    grid_spec = pltpu.PrefetchScalarGridSpec(
        num_scalar_prefetch=2,
        grid=(B, pl.cdiv(nt, TILES_PER_STEP)),
        in_specs=[pl.BlockSpec((1, rows, N_EXPERTS), lambda b, t, *_: (b, t, 0)),
                  pl.BlockSpec((1, TILES_PER_STEP, 1, N_EXPERTS), lambda b, t, *_: (b, t, 0, 0)),
                  pl.BlockSpec((1, rows, W), lambda b, t, *_: (b, t, 0))],
        out_specs=pl.BlockSpec((1, N_EXPERTS, cap_total, W), lambda b, t, *_: (b, 0, 0, 0)),
    )
    return pl.pallas_call(
        functools.partial(_dispatch_kernel, n_tiles=nt),
        out_shape=jax.ShapeDtypeStruct((B, N_EXPERTS, cap_total, W), BF16),
        grid_spec=grid_spec,
        compiler_params=_params(2),
        name="dispatch",
    )(first.reshape(-1), n_pass.reshape(-1), slot,
      first.astype(F32).reshape(B, nt, 1, N_EXPERTS), hx)


def _ffn_kernel(x_ref, wg_ref, wu_ref, wd_ref, *rest):
    if len(rest) == 1:
        o_ref, = rest
    else:
        o_ref = rest[3]
        for src, dst in zip(rest[:3], rest[4:]):
            dst[0] = src[0, 0].astype(BF16)
    D = wg_ref.shape[1]
    ff = wg_ref.shape[2]
    x = x_ref[0, 0, :, :D]
    pieces = x_ref[0, 0, :, D:].astype(F32)
    lane = lax.broadcasted_iota(jnp.int32, pieces.shape, 1)
    mine = ((lane & (N_EXPERTS - 1)) == pl.program_id(0)) & (lane < 3 * N_EXPERTS)
    gate = jnp.sum(jnp.where(mine, pieces, 0.0), axis=-1, keepdims=True)
    chunk = 512
    acc = jnp.zeros((x.shape[0], D), F32)
    for j in range(ff // chunk):
        cols = slice(j * chunk, (j + 1) * chunk)
        g = _dot(x, wg_ref[0, :, cols])
        u = _dot(x, wu_ref[0, :, cols])
        hid = (g * jax.nn.sigmoid(g) * u).astype(BF16)
        acc = acc + _dot(hid, wd_ref[0, cols, :])
    o_ref[0, 0] = (acc * gate).astype(BF16)


def _ffn(xin, wg, wu, wd, next_f32=None, next_layer=None):
    B, E, S, W = xin.shape
    D, FF = wg.shape[1], wg.shape[2]
    in_specs = [pl.BlockSpec((1, 1, S, W), lambda e, b: (b, e, 0, 0)),
                pl.BlockSpec((1, D, FF), lambda e, b: (e, 0, 0)),
                pl.BlockSpec((1, D, FF), lambda e, b: (e, 0, 0)),
                pl.BlockSpec((1, FF, D), lambda e, b: (e, 0, 0))]
    out_shape = [jax.ShapeDtypeStruct((B, E, S, D), BF16)]
    out_specs = [pl.BlockSpec((1, 1, S, D), lambda e, b: (b, e, 0, 0))]
    args = [xin, wg, wu, wd]
    if next_f32 is not None:
        for w in next_f32:
            assert w.shape[2] % (B * SLOT_ALIGN) == 0
            rows, cols = w.shape[2] // B, w.shape[3]
            in_specs.append(pl.BlockSpec((1, 1, rows, cols), lambda e, b: (next_layer, e, b, 0)))
            out_shape.append(jax.ShapeDtypeStruct(w.shape[1:], BF16))
            out_specs.append(pl.BlockSpec((1, rows, cols), lambda e, b: (e, b, 0)))
            args.append(w)
    out = pl.pallas_call(
        _ffn_kernel,
        out_shape=out_shape,
        grid=(E, B),
        in_specs=in_specs,
        out_specs=out_specs,
        compiler_params=_params(2),
        name="expert_ffn",
    )(*args)
    return out[0] if next_f32 is None else out


def _combine_kernel(first_s, npass_s, slot_ref, first_ref, expand_ref, y_ref, x_ref, g2_ref,
                    *rest, n_tiles, out_tiles):
    o_ref = rest[-1]
    b, step = pl.program_id(0), pl.program_id(1)
    cap_total = y_ref.shape[2]
    lane_row = (lax.broadcasted_iota(jnp.int32, (TILE, N_EXPERTS * SLOT_WIN), 1)
                & (SLOT_WIN - 1)).astype(F32)

    def one_tile(i, t):
        rows_out = slice(i * TILE, (i + 1) * TILE)
        o_ref[0, rows_out, :] = x_ref[0, rows_out, :]
        slot = slot_ref[0, rows_out, :]
        first = first_ref[0, i]
        g2 = jnp.where(t == n_tiles - 1, g2_ref[0, 1], g2_ref[0, 0])

        def one_pass(k, carry):
            rel = _window_rel(slot, first, k.astype(F32), cap_total).astype(BF16)
            onehot = (_dot(rel, expand_ref[...]) == lane_row).astype(BF16)
            rows = jnp.concatenate(
                [y_ref[0, e, pl.ds(_window_begin(first_s[(b * n_tiles + t) * N_EXPERTS + e], k,
                                                 cap_total), SLOT_WIN), :]
                 for e in range(N_EXPERTS)], axis=0)
            o_ref[0, rows_out, :] += g2 * _dot(onehot, rows)
            return carry

        lax.fori_loop(0, npass_s[b * n_tiles + t], one_pass, 0)
        if len(rest) == 2:
            x = o_ref[0, rows_out, :]
            ms = jnp.mean(x * x, axis=-1, keepdims=True)
            o_ref[0, rows_out, :] = x * lax.rsqrt(ms + RMS_EPS) * rest[0][...]

    for i in range(TILES_PER_STEP):
        t = step * TILES_PER_STEP + i
        pl.when(t < out_tiles)(functools.partial(one_tile, i, t))


def _combine(y, slot, first, n_pass, xa, gate2, final_g=None):
    B, S, D = xa.shape
    nt = S // TILE
    E, cap_total = y.shape[1], y.shape[2]
    rows = TILES_PER_STEP * TILE
    expand = np.repeat(np.eye(E, dtype=np.float32), SLOT_WIN, axis=1)
    in_specs = [pl.BlockSpec((1, rows, E), lambda b, t, *_: (b, t, 0)),
                pl.BlockSpec((1, TILES_PER_STEP, 1, E), lambda b, t, *_: (b, t, 0, 0)),
                pl.BlockSpec((E, E * SLOT_WIN), lambda b, t, *_: (0, 0)),
                pl.BlockSpec((1, E, cap_total, D), lambda b, t, *_: (b, 0, 0, 0)),
                pl.BlockSpec((1, rows, D), lambda b, t, *_: (b, t, 0)),
                pl.BlockSpec((1, 2, 1, D), lambda b, t, *_: (b, 0, 0, 0))]
    args = [slot, first.astype(F32).reshape(B, nt, 1, E), jnp.asarray(expand, BF16), y, xa, gate2]
    out_tiles = nt
    if final_g is not None:
        in_specs.append(pl.BlockSpec((1, D), lambda b, t, *_: (0, 0)))
        args.append(final_g)
        out_tiles = nt - 1
    grid_spec = pltpu.PrefetchScalarGridSpec(
        num_scalar_prefetch=2,
        grid=(B, pl.cdiv(out_tiles, TILES_PER_STEP)),
        in_specs=in_specs,
        out_specs=pl.BlockSpec((1, rows, D), lambda b, t, *_: (b, t, 0)),
    )
    return pl.pallas_call(
        functools.partial(_combine_kernel, n_tiles=nt, out_tiles=out_tiles),
        out_shape=jax.ShapeDtypeStruct((B, out_tiles * TILE, D), F32),
        grid_spec=grid_spec,
        compiler_params=_params(2),
        name="combine",
    )(first.reshape(-1), n_pass.reshape(-1), *args)


def _scale_q(w, nq):
    return jnp.concatenate([w[:, :nq] * (HEAD_DIM ** -0.5 * LOG2E), w[:, nq:]], axis=1)


def _win_weights(w_in):
    D = w_in.shape[0]
    nq = 16 * HEAD_DIM
    nk = A_KV_HEADS * HEAD_DIM
    w = _scale_q(w_in, nq)
    dup = lambda m: jnp.concatenate([m.reshape(D, A_KV_HEADS, 1, HEAD_DIM)] * 2,
                                    axis=2).reshape(D, 2 * nk)
    return jnp.concatenate([w[:, :nq], dup(w[:, nq:nq + nk]), dup(w[:, nq + nk:])],
                           axis=1).astype(BF16)


def _router_weights(rw):
    D, E = rw.shape
    return jnp.concatenate([rw, rw, rw, jnp.zeros((D, GATE_COLS - 3 * E), rw.dtype)], axis=1)


def kernel(x, c, ctx, c_ctx, ada_w, ada_b, norm1_g, norm2_g, final_g, win_w_in, win_w_out,
           win_sink, diff_w_in, diff_w_out, diff_lambda, diff_subln_g, na_w_in, na_w_out,
           na_rpb, router_w, w_gate, w_up, w_down):
    B, T, D = x.shape
    L = ctx.shape[1]
    S = T + L
    depth = ada_w.shape[0]
    assert L == TILE and T % ROW_TILE == 0 and (T // GRID_W) >= NA_WIN_ROWS
    cs = jnp.concatenate([c, c_ctx[None, :]], axis=0)
    rope = _rope_tables(T, L)
    experts = [w[0].astype(BF16) for w in (w_gate, w_up, w_down)]
    xa = None
    for i in range(depth):
        kind = i % N_MIXERS
        slot = i // N_MIXERS
        mod = _ada(cs, ada_w[i], ada_b[i][None, :])
        mod = jnp.stack([mod[:B], jnp.broadcast_to(mod[B:], (B, 6 * D))], axis=1)
        mod = mod.reshape(B, 2, 1, 6, D)
        sh1, sc1, g1, sh2, sc2, g2 = [mod[:, :, :, k, :] for k in range(6)]
        if kind == 0:
            w_in = _win_weights(win_w_in[slot])
            n_rope = (16 + 2 * A_KV_HEADS) * HEAD_DIM
            w_out = win_w_out[slot]
        elif kind == 1:
            w_in = _scale_q(diff_w_in[slot], 16 * HEAD_DIM).astype(BF16)
            n_rope = 32 * HEAD_DIM
            w_out = diff_w_out[slot]
        else:
            w_in = _scale_q(na_w_in[slot], 16 * HEAD_DIM).astype(BF16)
            n_rope = 0
            w_out = na_w_out[slot]
        if i == 0:
            qkv, xa = _norm_proj((x, ctx), norm1_g[i][None, :], sh1, sc1, w_in, rope, n_rope)
        else:
            qkv = _norm_proj((xa,), norm1_g[i][None, :], sh1, sc1, w_in, rope, n_rope)
        if kind == 0:
            y = _attn_win(qkv, win_sink[slot], T)
        elif kind == 1:
            lambda_init = 0.8 - 0.6 * math.exp(-0.3 * i)
            lp = diff_lambda[slot]
            lam = (jnp.exp(jnp.sum(lp[0] * lp[1])) - jnp.exp(jnp.sum(lp[2] * lp[3]))
                   + lambda_init).reshape(1)
            y = _attn_diff(qkv, lam, diff_subln_g[slot][None, :], T, 1.0 - lambda_init)
        else:
            y = _attn_na(qkv, _na_bias_tables(na_rpb[slot], T), _na_bias_bound(na_rpb[slot]), T)
        xa, hx, aff = _out_router(y, w_out.astype(BF16), xa, g1, norm2_g[i][None, :],
                                  sh2, sc2, _router_weights(router_w[i]))
        tok_slot, start, cnt = _route(aff, T)
        first, n_pass = _slot_plan(start, cnt)
        cap_total = CAPACITY_FACTOR * S // N_EXPERTS
        xin = _dispatch(hx, tok_slot, first, n_pass, cap_total)
        last = i == depth - 1
        if last:
            ye = _ffn(xin, *experts)
        else:
            ye, *experts = _ffn(xin, *experts, (w_gate, w_up, w_down), i + 1)
        xa = _combine(ye, tok_slot, first, n_pass, xa, g2, final_g[None, :] if last else None)
    return xa
```

```python
import functools
import math

import numpy as np
import jax
import jax.numpy as jnp
from jax import lax
from jax.experimental import pallas as pl
from jax.experimental.pallas import tpu as pltpu

HEAD_DIM = 64
LANES = 128
GRID_W = 64
NA_ROWS = 8
NA_COLS = 16
A_WINDOW = 128
A_KV_HEADS = 4
N_MIXERS = 3
ROPE_BASE = 10000.0
ROPE_AXIS_DIM = HEAD_DIM // 2
N_EXPERTS = 16
CAPACITY_FACTOR = 2
RMS_EPS = 1e-6
NEG_INF = -1e30
LOG2E = math.log2(math.e)
TILE = 256
ROW_TILE = 2 * TILE
NA_TILE_ROWS = TILE // GRID_W
NA_WIN_ROWS = NA_TILE_ROWS + NA_ROWS
VMEM_LIMIT = 56 * 1024 * 1024
SLOT_WIN = 64
SLOT_ALIGN = 16
GATE_COLS = LANES

BF16 = jnp.bfloat16
F32 = jnp.float32


def _params(n_grid):
    return pltpu.CompilerParams(
        dimension_semantics=("arbitrary",) * n_grid, vmem_limit_bytes=VMEM_LIMIT)


def _split_bf16(a):
    hi = a.astype(BF16)
    lo = (a - hi.astype(F32)).astype(BF16)
    return hi, lo


def _dot(a, b):
    return jnp.dot(a, b, preferred_element_type=F32)


def _dot_nt(a, b):
    return lax.dot_general(a, b, (((1,), (1,)), ((), ())), preferred_element_type=F32)


def _dot_tn(a, b):
    return lax.dot_general(a, b, (((0,), (0,)), ((), ())), preferred_element_type=F32)


def _ada_kernel(c_ref, w_ref, b_ref, o_ref):
    c = c_ref[...]
    a = c * jax.nn.sigmoid(c)
    a_hi, a_lo = _split_bf16(a)
    w_hi, w_lo = _split_bf16(w_ref[...])
    o_ref[...] = _dot(a_hi, w_hi) + _dot(a_lo, w_hi) + _dot(a_hi, w_lo) + b_ref[...]


def _ada(cs, w, b):
    R, D = cs.shape
    N = w.shape[1]
    tn = 1024
    return pl.pallas_call(
        _ada_kernel,
        out_shape=jax.ShapeDtypeStruct((R, N), F32),
        grid=(N // tn,),
        in_specs=[pl.BlockSpec((R, D), lambda j: (0, 0)),
                  pl.BlockSpec((D, tn), lambda j: (0, j)),
                  pl.BlockSpec((1, tn), lambda j: (0, j))],
        out_specs=pl.BlockSpec((R, tn), lambda j: (0, j)),
        compiler_params=_params(1),
        name="ada",
    )(cs, w, b)


def _rms_mod(x, g, shift, scale):
    ms = jnp.mean(x * x, axis=-1, keepdims=True)
    y = x * lax.rsqrt(ms + RMS_EPS) * g
    return y * (1.0 + scale) + shift


def _norm_proj_kernel(*refs, n_rope, joins_streams):
    if joins_streams:
        x_ref, c_ref, g_ref, sh_ref, sc_ref, w_ref, cos_ref, sa_ref, sb_ref, o_ref, xa_ref = refs
        is_ctx = pl.program_id(1) == pl.num_programs(1) - 1
        ctx_rows = jnp.concatenate([c_ref[0]] * (ROW_TILE // TILE), axis=0)
        x = jnp.where(is_ctx, ctx_rows, x_ref[0])
        xa_ref[0] = x
    else:
        x_ref, g_ref, sh_ref, sc_ref, w_ref, cos_ref, sa_ref, sb_ref, o_ref = refs
        x = x_ref[0]
    h = _rms_mod(x, g_ref[...], sh_ref[0, 0], sc_ref[0, 0]).astype(BF16)
    n_cols = w_ref.shape[1]
    chunk = 512
    for j in range(n_cols // chunk):
        acc = _dot(h, w_ref[:, j * chunk:(j + 1) * chunk])
        for t in range(chunk // LANES):
            col = j * chunk + t * LANES
            a = acc[:, t * LANES:(t + 1) * LANES]
            if col < n_rope:
                a = (a * cos_ref[...]
                     + pltpu.roll(a, LANES - 16, 1) * sa_ref[...]
                     + pltpu.roll(a, 16, 1) * sb_ref[...])
            o_ref[0, :, col:col + LANES] = a.astype(BF16)


def _norm_proj(streams, g, shift, scale, w, rope, n_rope):
    joins = len(streams) == 2
    B, _, D = streams[0].shape
    S = sum(a.shape[1] for a in streams)
    N = w.shape[1]
    nt = pl.cdiv(S, ROW_TILE)
    mod_spec = pl.BlockSpec((1, 1, 1, D), lambda b, t: (b, t // (nt - 1), 0, 0))
    rope_spec = pl.BlockSpec((ROW_TILE, LANES), lambda b, t: (t, 0))
    tile_spec = pl.BlockSpec((1, ROW_TILE, D), lambda b, t: (b, t, 0))
    if joins:
        stream_specs = [pl.BlockSpec((1, ROW_TILE, D), lambda b, t: (b, jnp.minimum(t, nt - 2), 0)),
                        pl.BlockSpec((1, TILE, D), lambda b, t: (b, 0, 0))]
    else:
        stream_specs = [tile_spec]
    qkv_shape = jax.ShapeDtypeStruct((B, S, N), BF16)
    qkv_spec = pl.BlockSpec((1, ROW_TILE, N), lambda b, t: (b, t, 0))
    return pl.pallas_call(
        functools.partial(_norm_proj_kernel, n_rope=n_rope, joins_streams=joins),
        out_shape=(qkv_shape, jax.ShapeDtypeStruct((B, S, D), F32)) if joins else qkv_shape,
        grid=(B, nt),
        in_specs=stream_specs + [pl.BlockSpec((1, D), lambda b, t: (0, 0)),
                                 mod_spec, mod_spec,
                                 pl.BlockSpec((D, N), lambda b, t: (0, 0)),
                                 rope_spec, rope_spec, rope_spec],
        out_specs=(qkv_spec, tile_spec) if joins else qkv_spec,
        compiler_params=_params(2),
        name="norm_proj",
    )(*streams, g, shift, scale, w, *rope)


def _rope_tables(T, L):
    t = np.arange(T)
    pos = np.stack([t // GRID_W, t % GRID_W], axis=0).astype(np.float32)
    inv = (1.0 / (ROPE_BASE ** (np.arange(0, ROPE_AXIS_DIM, 2, dtype=np.float32)
                                / ROPE_AXIS_DIM))).astype(np.float32)
    d = np.arange(LANES) % HEAD_DIM
    axis = d // ROPE_AXIS_DIM
    half = (d % ROPE_AXIS_DIM) // (ROPE_AXIS_DIM // 2)
    freq = d % (ROPE_AXIS_DIM // 2)
    ang = jnp.asarray(pos[axis].T) * jnp.asarray(inv[freq])[None, :]
    cos = jnp.cos(ang)
    sin = jnp.sin(ang)
    first = jnp.asarray(half == 0)[None, :]
    sa = jnp.where(first, -sin, 0.0)
    sb = jnp.where(first, 0.0, sin)
    pad = lambda a, v: jnp.concatenate([a, jnp.full((L, LANES), v, F32)], axis=0)
    return pad(cos, 1.0), pad(sa, 0.0), pad(sb, 0.0)


def _half_masks(q):
    lane = lax.broadcasted_iota(jnp.int32, q.shape, 1)
    zero = jnp.zeros_like(q)
    return jnp.where(lane < HEAD_DIM, q, zero), jnp.where(lane >= HEAD_DIM, q, zero)


def _merge_halves(o_first, o_second):
    lane = lax.broadcasted_iota(jnp.int32, o_first.shape, 1)
    return jnp.where(lane < HEAD_DIM, o_first, o_second)


SHIFT_SLACK = 1.02
L_MIN = 2.0 ** -64


def _sq_norms_row(x):
    xf = x.astype(F32)
    return _dot_nt(jnp.ones((8, LANES), BF16), (xf * xf).astype(BF16))[:1]


def _head_bounds_row(qs):
    qf = qs.astype(F32)
    n2 = jnp.sum(qf * qf, axis=-1, keepdims=True)
    return jnp.concatenate(
        [jnp.broadcast_to(jnp.max(n2[r:r + TILE], axis=0, keepdims=True), (1, TILE))
         for r in range(0, qs.shape[0], TILE)], axis=1)


def _store_key_bound(kmax_ref, j, k, scale=1.0):
    n2 = jnp.max(_sq_norms_row(k), axis=1, keepdims=True) * scale
    kmax_ref[j] = jnp.broadcast_to(n2, kmax_ref.shape[1:])


def _redo_if_underflow(attend, *args):
    den_min = attend(*args, exact=False)

    @pl.when(den_min[0, 0] < L_MIN)
    def _():
        attend(*args, exact=True)


WIN_KV_PER_STEP = 4


def _win_kernel(sink_ref, q_ref, k_ref, v_ref, o_ref, kmax_ref, *, T, win):
    hb = pl.program_id(1)
    t = pl.program_id(2)
    n_x = T // TILE
    is_x = t < n_x
    heads = [slice(j * LANES, (j + 1) * LANES) for j in range(WIN_KV_PER_STEP)]

    @pl.when(t == 0)
    def _():
        for j, cols in enumerate(heads):
            _store_key_bound(kmax_ref, j, k_ref[0, :, cols], 0.5)

    start = pl.multiple_of(jnp.clip(t * TILE - A_WINDOW, 0, T - win), LANES)
    kpos = start + lax.broadcasted_iota(jnp.int32, (win, TILE), 0)
    qpos = t * TILE + lax.broadcasted_iota(jnp.int32, (win, TILE), 1)
    band = (jnp.abs(qpos - kpos) <= A_WINDOW) & is_x
    band = jnp.concatenate([band] * 4, axis=1)

    def attend(exact):
        queries, scores = [], []
        for j, cols in enumerate(heads):
            q = q_ref[0, :, 2 * j * LANES:2 * (j + 1) * LANES]
            qa, qb = _half_masks(q[:, :LANES])
            qc, qd = _half_masks(q[:, LANES:])
            qs = jnp.concatenate([qa, qb, qc, qd], axis=0)
            queries.append(qs)
            scores.append((_dot_nt(k_ref[0, T:, cols], qs),
                           _dot_nt(k_ref[0, pl.ds(start, win), cols], qs)))
        den_min = None
        for j, cols in enumerate(heads):
            s_c, s_w = scores[j]
            head0 = (hb * WIN_KV_PER_STEP + j) * 4
            s_w = jnp.where(band, s_w, NEG_INF)
            sink = jnp.concatenate(
                [jnp.full((1, TILE), sink_ref[head0 + g] * LOG2E, F32) for g in range(4)], axis=1)
            if exact:
                m = jnp.maximum(jnp.max(s_c, axis=0, keepdims=True),
                                jnp.max(s_w, axis=0, keepdims=True))
            else:
                m = jnp.sqrt(_head_bounds_row(queries[j]) * kmax_ref[j, :1, :1]) * SHIFT_SLACK
            m = jnp.maximum(m, sink)
            ec = jnp.exp2(s_c - m)
            ew = jnp.exp2(s_w - m)
            den = (jnp.sum(ec, axis=0, keepdims=True) + jnp.sum(ew, axis=0, keepdims=True)
                   + jnp.exp2(sink - m))
            o = ((_dot_tn(v_ref[0, T:, cols], ec.astype(BF16))
                  + _dot_tn(v_ref[0, pl.ds(start, win), cols], ew.astype(BF16))) * (1.0 / den)).T
            first = 2 * j * LANES
            o_ref[0, :, first:first + LANES] = _merge_halves(
                o[:TILE], o[TILE:2 * TILE]).astype(BF16)
            o_ref[0, :, first + LANES:first + 2 * LANES] = _merge_halves(
                o[2 * TILE:3 * TILE], o[3 * TILE:]).astype(BF16)
            low = jnp.min(den, axis=1, keepdims=True)
            den_min = low if den_min is None else jnp.minimum(den_min, low)
        return den_min

    _redo_if_underflow(attend)


def _attn_win(qkv, sink, T):
    B, S, _ = qkv.shape
    nt = S // TILE
    nq = 16 * HEAD_DIM
    win = TILE + 2 * A_WINDOW
    hb = A_KV_HEADS // WIN_KV_PER_STEP
    wq = WIN_KV_PER_STEP * 2 * LANES
    wk = WIN_KV_PER_STEP * LANES
    kb = nq // wk
    return pl.pallas_call(
        functools.partial(_win_kernel, T=T, win=win),
        out_shape=jax.ShapeDtypeStruct((B, S, nq), BF16),
        grid=(B, hb, nt),
        in_specs=[pl.BlockSpec(memory_space=pltpu.SMEM),
                  pl.BlockSpec((1, TILE, wq), lambda b, h, t: (b, t, h)),
                  pl.BlockSpec((1, S, wk), lambda b, h, t: (b, 0, kb + h)),
                  pl.BlockSpec((1, S, wk), lambda b, h, t: (b, 0, kb + hb + h))],
        out_specs=pl.BlockSpec((1, TILE, wq), lambda b, h, t: (b, t, h)),
        scratch_shapes=[pltpu.VMEM((WIN_KV_PER_STEP, 8, LANES), F32)],
        compiler_params=_params(3),
        name="attn_win",
    )(sink, qkv, qkv, qkv)


DIFF_HEADS_PER_STEP = 2
Q_TILES_PER_STEP = 2


def _diff_kernel(lam_ref, q_ref, k_ref, v_ref, g_ref, o_ref, kmax_ref, *, T, out_scale):
    step = pl.program_id(2)
    n_x = T // TILE
    lam = lam_ref[0]
    heads = [slice(j * LANES, (j + 1) * LANES) for j in range(DIFF_HEADS_PER_STEP)]

    @pl.when(step == 0)
    def _():
        for j, cols in enumerate(heads):
            k1, k2 = _half_masks(k_ref[0, :, cols])
            _store_key_bound(kmax_ref, 2 * j, k1)
            _store_key_bound(kmax_ref, 2 * j + 1, k2)

    def attend(rows, keys, exact):
        queries, scores = [], []
        for cols in heads:
            qa, qb = _half_masks(q_ref[0, rows, cols])
            qs = jnp.concatenate([qa, qb], axis=0)
            queries.append(qs)
            scores.append(_dot_nt(qs, k_ref[0, keys, cols]))
        l_min = None
        for j, cols in enumerate(heads):
            s = scores[j]
            if exact:
                m = jnp.max(s, axis=-1, keepdims=True)
            else:
                qf = queries[j].astype(F32)
                row = lax.broadcasted_iota(jnp.int32, (2 * TILE, 1), 0)
                kmax = jnp.where(row < TILE, kmax_ref[2 * j, :1, :1], kmax_ref[2 * j + 1, :1, :1])
                m = jnp.sqrt(jnp.sum(qf * qf, axis=-1, keepdims=True) * kmax) * SHIFT_SLACK
            e = jnp.exp2(s - m)
            l = jnp.sum(e, axis=-1, keepdims=True)
            a = e[:TILE] - e[TILE:] * (lam * l[:TILE] / l[TILE:])
            o = _dot(a.astype(BF16), v_ref[0, keys, cols]) / l[:TILE]
            ms = jnp.mean(o * o, axis=-1, keepdims=True)
            y = o * lax.rsqrt(ms + RMS_EPS) * g_ref[...] * out_scale
            o_ref[0, rows, cols] = y.astype(BF16)
            low = jnp.min(l, axis=0, keepdims=True)
            l_min = low if l_min is None else jnp.minimum(l_min, low)
        return l_min

    for i in range(Q_TILES_PER_STEP):
        t = step * Q_TILES_PER_STEP + i
        rows = slice(i * TILE, (i + 1) * TILE)
        pl.when(t < n_x)(functools.partial(_redo_if_underflow, attend, rows, slice(None)))
        pl.when(t == n_x)(functools.partial(_redo_if_underflow, attend, rows, slice(T, None)))


def _attn_diff(qkv, lam, subln_g, T, out_scale):
    B, S, _ = qkv.shape
    nt = S // TILE
    H = 8
    hb = H // DIFF_HEADS_PER_STEP
    w = DIFF_HEADS_PER_STEP * LANES
    rows = Q_TILES_PER_STEP * TILE
    return pl.pallas_call(
        functools.partial(_diff_kernel, T=T, out_scale=out_scale),
        out_shape=jax.ShapeDtypeStruct((B, S, H * LANES), BF16),
        grid=(B, hb, pl.cdiv(nt, Q_TILES_PER_STEP)),
        in_specs=[pl.BlockSpec(memory_space=pltpu.SMEM),
                  pl.BlockSpec((1, rows, w), lambda b, h, t: (b, t, h)),
                  pl.BlockSpec((1, S, w), lambda b, h, t: (b, 0, hb + h)),
                  pl.BlockSpec((1, S, w), lambda b, h, t: (b, 0, 2 * hb + h)),
                  pl.BlockSpec((1, LANES), lambda b, h, t: (0, 0))],
        out_specs=pl.BlockSpec((1, rows, w), lambda b, h, t: (b, t, h)),
        scratch_shapes=[pltpu.VMEM((2 * DIFF_HEADS_PER_STEP, 8, LANES), F32)],
        compiler_params=_params(3),
        name="attn_diff",
    )(lam, qkv, qkv, qkv, subln_g)


NA_PAIRS_PER_STEP = 2


def _na_kernel(q_ref, k_ref, v_ref, bias_ref, bmax_ref, o_ref, kmax_ref, *, T):
    step = pl.program_id(2)
    n_x = T // TILE
    grid_rows = T // GRID_W
    win = NA_WIN_ROWS * GRID_W
    pairs = [slice(j * LANES, (j + 1) * LANES) for j in range(NA_PAIRS_PER_STEP)]

    @pl.when(step == 0)
    def _():
        for j, cols in enumerate(pairs):
            k1, k2 = _half_masks(k_ref[0, :, cols])
            _store_key_bound(kmax_ref, 2 * j, k1)
            _store_key_bound(kmax_ref, 2 * j + 1, k2)

    def attend(rows, t, exact):
        cls = jnp.where(t >= n_x, 3, jnp.where(t == 0, 0, jnp.where(t == n_x - 1, 2, 1)))
        row0 = jnp.clip(t * NA_TILE_ROWS - NA_ROWS // 2, 0, grid_rows - NA_WIN_ROWS)
        start = pl.multiple_of(row0 * GRID_W, GRID_W)
        queries, scores = [], []
        for cols in pairs:
            qa, qb = _half_masks(q_ref[0, rows, cols])
            qs = jnp.concatenate([qa, qb], axis=0)
            queries.append(qs)
            scores.append((_dot_nt(k_ref[0, T:, cols], qs),
                           _dot_nt(k_ref[0, pl.ds(start, win), cols], qs)))
        den_min = None
        for j, cols in enumerate(pairs):
            s_c, s_w = scores[j]
            s_w = s_w + bias_ref[cls, j]
            if exact:
                m = jnp.maximum(jnp.max(s_c, axis=0, keepdims=True),
                                jnp.max(s_w, axis=0, keepdims=True))
            else:
                lane = lax.broadcasted_iota(jnp.int32, (1, 2 * TILE), 1)
                kmax = jnp.where(lane < TILE, kmax_ref[2 * j, :1, :1], kmax_ref[2 * j + 1, :1, :1])
                m = jnp.sqrt(_head_bounds_row(queries[j]) * kmax) * SHIFT_SLACK + bmax_ref[j]
            ec = jnp.exp2(s_c - m)
            ew = jnp.exp2(s_w - m)
            den = jnp.sum(ec, axis=0, keepdims=True) + jnp.sum(ew, axis=0, keepdims=True)
            o = ((_dot_tn(v_ref[0, T:, cols], ec.astype(BF16))
                  + _dot_tn(v_ref[0, pl.ds(start, win), cols], ew.astype(BF16))) * (1.0 / den)).T
            o_ref[0, rows, cols] = _merge_halves(o[:TILE], o[TILE:]).astype(BF16)
            low = jnp.min(den, axis=1, keepdims=True)
            den_min = low if den_min is None else jnp.minimum(den_min, low)
        return den_min

    for i in range(Q_TILES_PER_STEP):
        t = step * Q_TILES_PER_STEP + i
        rows = slice(i * TILE, (i + 1) * TILE)
        pl.when(t <= n_x)(functools.partial(_redo_if_underflow, attend, rows, t))


def _attn_na(qkv, bias, bias_max, T):
    B, S, _ = qkv.shape
    nt = S // TILE
    n = NA_PAIRS_PER_STEP
    hb = 8 // n
    w = n * LANES
    win = NA_WIN_ROWS * GRID_W
    rows = Q_TILES_PER_STEP * TILE
    return pl.pallas_call(
        functools.partial(_na_kernel, T=T),
        out_shape=jax.ShapeDtypeStruct((B, S, 8 * LANES), BF16),
        grid=(hb, B, pl.cdiv(nt, Q_TILES_PER_STEP)),
        in_specs=[pl.BlockSpec((1, rows, w), lambda h, b, t: (b, t, h)),
                  pl.BlockSpec((1, S, w), lambda h, b, t: (b, 0, hb + h)),
                  pl.BlockSpec((1, S, w), lambda h, b, t: (b, 0, 2 * hb + h)),
                  pl.BlockSpec((4, n, win, 2 * TILE), lambda h, b, t: (0, h, 0, 0)),
                  pl.BlockSpec((n, 1, 2 * TILE), lambda h, b, t: (h, 0, 0))],
        out_specs=pl.BlockSpec((1, rows, w), lambda h, b, t: (b, t, h)),
        scratch_shapes=[pltpu.VMEM((2 * NA_PAIRS_PER_STEP, 8, LANES), F32)],
        compiler_params=_params(3),
        name="attn_na",
    )(qkv, qkv, qkv, bias, bias_max)


def _na_bias_bound(rpb):
    top = jnp.maximum(jnp.max(rpb, axis=(1, 2)), 0.0) * LOG2E
    return jnp.repeat(top.reshape(-1, 2), TILE, axis=1)[:, None, :]


def _na_bias_tables(rpb, T):
    rows = T // GRID_W
    H = rpb.shape[0]
    win = NA_WIN_ROWS * GRID_W
    hi = lax.Precision.HIGHEST
    pick = lambda idx, n: jnp.asarray(idx[..., None] == np.arange(n), F32)
    c = np.arange(GRID_W)
    cs = np.clip(c - NA_COLS // 2, 0, GRID_W - NA_COLS)
    valid_c = (c[None, :] >= cs[:, None]) & (c[None, :] < cs[:, None] + NA_COLS)
    bidx_c = np.clip(c[None, :] - c[:, None] + NA_COLS - 1, 0, 2 * NA_COLS - 2)
    by_col = jnp.einsum('hrd,ckd->hrck', rpb, pick(bidx_c, 2 * NA_COLS - 1), precision=hi)
    i = np.arange(NA_TILE_ROWS)
    j = np.arange(NA_WIN_ROWS)
    tables = []
    for r0 in (0, NA_TILE_ROWS, rows - NA_TILE_ROWS):
        s = int(np.clip(r0 - NA_ROWS // 2, 0, rows - NA_WIN_ROWS))
        r = r0 + i
        rs = np.clip(r - NA_ROWS // 2, 0, rows - NA_ROWS)
        kr = s + j
        valid_r = (kr[None, :] >= rs[:, None]) & (kr[None, :] < rs[:, None] + NA_ROWS)
        bidx_r = np.clip(kr[None, :] - r[:, None] + NA_ROWS - 1, 0, 2 * NA_ROWS - 2)
        vals = jnp.einsum('hrck,ijr->hjkic', by_col, pick(bidx_r, 2 * NA_ROWS - 1), precision=hi)
        valid = valid_r.T[:, None, :, None] & valid_c.T[None, :, None, :]
        vals = jnp.where(jnp.asarray(valid)[None], vals, NEG_INF).reshape(H // 2, 2, win, TILE)
        tables.append(vals.transpose(0, 2, 1, 3).reshape(H // 2, win, 2 * TILE))
    tables.append(jnp.full_like(tables[0], NEG_INF))
    return jnp.stack(tables, axis=0).astype(F32) * LOG2E


def _out_router_kernel(y_ref, w_ref, x_ref, g1_ref, n2_ref, sh_ref, sc_ref, rw_ref,
                       xo_ref, h_ref, aff_ref):
    D = x_ref.shape[2]
    x = x_ref[0] + g1_ref[0, 0] * _dot(y_ref[0], w_ref[...])
    xo_ref[0] = x
    h = _rms_mod(x, n2_ref[...], sh_ref[0, 0], sc_ref[0, 0])
    h_hi, h_lo = _split_bf16(h)
    n = h.shape[0]
    r_hi_lo = jnp.concatenate(_split_bf16(rw_ref[...]), axis=1)
    prod = _dot(jnp.concatenate([h_hi, h_lo], axis=0), r_hi_lo)
    logits = prod[:n, :GATE_COLS] + prod[:n, GATE_COLS:] + prod[n:, :GATE_COLS]
    lane = lax.broadcasted_iota(jnp.int32, logits.shape, 1)
    first = lane < N_EXPERTS
    m = jnp.max(jnp.where(first, logits, -jnp.inf), axis=-1, keepdims=True)
    e = jnp.exp(logits - m)
    aff = e / jnp.sum(jnp.where(first, e, 0.0), axis=-1, keepdims=True)
    aff_ref[0] = aff
    hi = aff.astype(BF16)
    rem = aff - hi.astype(F32)
    mid = rem.astype(BF16)
    lo = (rem - mid.astype(F32)).astype(BF16)
    zero = jnp.zeros_like(hi)
    pieces = jnp.where(first, hi, jnp.where(lane < 2 * N_EXPERTS, mid,
                                            jnp.where(lane < 3 * N_EXPERTS, lo, zero)))
    h_ref[0, :, :D] = h_hi
    h_ref[0, :, D:] = pieces


def _out_router(y, w_out, xa, gate1, n2g, shift2, scale2, rw3):
    B, S, D = xa.shape
    nt = pl.cdiv(S, ROW_TILE)
    mod_spec = pl.BlockSpec((1, 1, 1, D), lambda b, t: (b, t // (nt - 1), 0, 0))
    tile_spec = pl.BlockSpec((1, ROW_TILE, D), lambda b, t: (b, t, 0))
    return pl.pallas_call(
        _out_router_kernel,
        out_shape=(jax.ShapeDtypeStruct((B, S, D), F32),
                   jax.ShapeDtypeStruct((B, S, D + GATE_COLS), BF16),
                   jax.ShapeDtypeStruct((B, S, GATE_COLS), F32)),
        grid=(B, nt),
        in_specs=[tile_spec,
                  pl.BlockSpec((D, D), lambda b, t: (0, 0)),
                  tile_spec, mod_spec,
                  pl.BlockSpec((1, D), lambda b, t: (0, 0)),
                  mod_spec, mod_spec,
                  pl.BlockSpec((D, GATE_COLS), lambda b, t: (0, 0))],
        out_specs=(tile_spec,
                   pl.BlockSpec((1, ROW_TILE, D + GATE_COLS), lambda b, t: (b, t, 0)),
                   pl.BlockSpec((1, ROW_TILE, GATE_COLS), lambda b, t: (b, t, 0))),
        compiler_params=_params(2),
        name="out_router",
    )(y, w_out, xa, gate1, n2g, shift2, scale2, rw3)


def _route_kernel(aff_ref, slot_ref, start_ref, cnt_ref, *, T, cap_x, cap_c):
    S = aff_ref.shape[1]
    aff = aff_ref[0]
    bits = lax.bitcast_convert_type(aff[:, :N_EXPERTS], jnp.int32)
    ri = lax.broadcasted_iota(jnp.int32, (TILE, TILE), 0)
    ci = lax.broadcasted_iota(jnp.int32, (TILE, TILE), 1)
    ltri = (ri > ci).astype(BF16)
    eye = (lax.broadcasted_iota(jnp.int32, (N_EXPERTS, N_EXPERTS), 0)
           == lax.broadcasted_iota(jnp.int32, (N_EXPERTS, N_EXPERTS), 1))

    def count(mask):
        return jnp.sum(mask.astype(F32), axis=0, keepdims=True)

    for lo_row, hi_row, cap, base in ((0, T, cap_x, 0), (T, S, cap_c, cap_x)):
        b = bits[lo_row:hi_row]
        dense = lax.bitcast_convert_type(jnp.concatenate(
            [aff[r:r + TILE].T[:N_EXPERTS] for r in range(lo_row, hi_row, TILE)], axis=1),
            jnp.int32)

        def step(i, thr):
            cand = thr | lax.shift_left(jnp.int32(1), 30 - i)
            n_ge = jnp.sum((dense >= cand).astype(F32), axis=1, keepdims=True)
            return jnp.where(n_ge >= cap, cand, thr)

        thr = lax.fori_loop(0, 31, step, jnp.zeros((N_EXPERTS, 1), jnp.int32))
        thr = jnp.max(jnp.where(eye, jnp.broadcast_to(thr, eye.shape), 0), axis=0,
                      keepdims=True)
        need = cap - count(b > thr)
        seen_eq = jnp.zeros((1, N_EXPERTS), F32)
        seen = jnp.zeros((1, N_EXPERTS), F32)
        for j in range((hi_row - lo_row) // TILE):
            blk = b[j * TILE:(j + 1) * TILE]
            gt = blk > thr
            eq = blk == thr
            eq_rank = _dot(ltri, eq.astype(BF16)) + seen_eq
            sel = gt | (eq & (eq_rank < need))
            pos = _dot(ltri, sel.astype(BF16)) + seen + base
            t = lo_row // TILE + j
            slot_ref[0, t * TILE:(t + 1) * TILE, :] = jnp.where(sel, pos, -1.0)
            n_sel = count(sel)
            start_ref[0, t:t + 1, :] = seen + base
            cnt_ref[0, t:t + 1, :] = n_sel
            seen_eq = seen_eq + count(eq)
            seen = seen + n_sel


def _route(aff, T):
    B, S, _ = aff.shape
    E = N_EXPERTS
    nt = S // TILE
    cap_x = CAPACITY_FACTOR * T // N_EXPERTS
    cap_c = CAPACITY_FACTOR * (S - T) // N_EXPERTS
    plan = jax.ShapeDtypeStruct((B, nt, E), F32)
    return pl.pallas_call(
        functools.partial(_route_kernel, T=T, cap_x=cap_x, cap_c=cap_c),
        out_shape=(jax.ShapeDtypeStruct((B, S, E), F32), plan, plan),
        grid=(B,),
        in_specs=[pl.BlockSpec((1, S, GATE_COLS), lambda b: (b, 0, 0))],
        out_specs=(pl.BlockSpec((1, S, E), lambda b: (b, 0, 0)),
                   pl.BlockSpec((1, nt, E), lambda b: (b, 0, 0)),
                   pl.BlockSpec((1, nt, E), lambda b: (b, 0, 0))),
        compiler_params=_params(1),
        name="route",
    )(aff)


def _slot_plan(start, cnt):
    start = start.astype(jnp.int32)
    cnt = cnt.astype(jnp.int32)
    first = (start // SLOT_ALIGN) * SLOT_ALIGN
    n_pass = jnp.max((start - first + cnt + SLOT_WIN - 1) // SLOT_WIN, axis=-1)
    return first, n_pass.astype(jnp.int32)


def _window_rel(slot, first, k, cap_total):
    nominal = first + k * SLOT_WIN
    begin = jnp.minimum(nominal, float(cap_total - SLOT_WIN))
    rel = slot - nominal
    return jnp.where((rel >= 0) & (rel < SLOT_WIN), rel + (nominal - begin), 255.0)


def _window_begin(first_s, k, cap_total):
    return pl.multiple_of(jnp.minimum(first_s + k * SLOT_WIN, cap_total - SLOT_WIN), SLOT_ALIGN)


TILES_PER_STEP = 2


def _dispatch_kernel(first_s, npass_s, slot_ref, first_ref, h_ref, o_ref, *, n_tiles):
    b, step = pl.program_id(0), pl.program_id(1)
    n_e = o_ref.shape[1]
    cap_total = o_ref.shape[2]

    @pl.when(step == 0)
    def _():
        o_ref[...] = jnp.zeros_like(o_ref)

    sub = lax.broadcasted_iota(jnp.int32, (SLOT_WIN, TILE), 0).astype(F32)
    eye = (lax.broadcasted_iota(jnp.int32, (n_e, N_EXPERTS), 0)
           == lax.broadcasted_iota(jnp.int32, (n_e, N_EXPERTS), 1)).astype(BF16)

    def one_tile(i, t):
        rows_in = slice(i * TILE, (i + 1) * TILE)
        slot = slot_ref[0, rows_in, :]
        first = first_ref[0, i]

        def one_pass(k, carry):
            rel = _window_rel(slot, first, k.astype(F32), cap_total).astype(BF16)
            rel_t = _dot_nt(eye, rel)
            onehot = jnp.concatenate(
                [(jnp.broadcast_to(rel_t[e:e + 1, :], (SLOT_WIN, TILE)) == sub).astype(BF16)
                 for e in range(n_e)], axis=0)
            rows = _dot(onehot, h_ref[0, rows_in, :])
            for e in range(n_e):
                begin = _window_begin(first_s[(b * n_tiles + t) * N_EXPERTS + e], k, cap_total)
                o_ref[0, e, pl.ds(begin, SLOT_WIN), :] += (
                    rows[e * SLOT_WIN:(e + 1) * SLOT_WIN].astype(BF16))
            return carry

        lax.fori_loop(0, npass_s[b * n_tiles + t], one_pass, 0)

    for i in range(TILES_PER_STEP):
        t = step * TILES_PER_STEP + i
        pl.when(t < n_tiles)(functools.partial(one_tile, i, t))


def _dispatch(hx, slot, first, n_pass, cap_total):
    B, S, W = hx.shape
    nt = S // TILE
    rows = TILES_PER_STEP * TILE
    grid_spec = pltpu.PrefetchScalarGridSpec(
        num_scalar_prefetch=2,
        grid=(B, pl.cdiv(nt, TILES_PER_STEP)),
        in_specs=[pl.BlockSpec((1, rows, N_EXPERTS), lambda b, t, *_: (b, t, 0)),
                  pl.BlockSpec((1, TILES_PER_STEP, 1, N_EXPERTS), lambda b, t, *_: (b, t, 0, 0)),
                  pl.BlockSpec((1, rows, W), lambda b, t, *_: (b, t, 0))],
        out_specs=pl.BlockSpec((1, N_EXPERTS, cap_total, W), lambda b, t, *_: (b, 0, 0, 0)),
    )
    return pl.pallas_call(
        functools.partial(_dispatch_kernel, n_tiles=nt),
        out_shape=jax.ShapeDtypeStruct((B, N_EXPERTS, cap_total, W), BF16),
        grid_spec=grid_spec,
        compiler_params=_params(2),
        name="dispatch",
    )(first.reshape(-1), n_pass.reshape(-1), slot,
      first.astype(F32).reshape(B, nt, 1, N_EXPERTS), hx)


def _ffn_kernel(x_ref, wg_ref, wu_ref, wd_ref, *rest):
    if len(rest) == 1:
        o_ref, = rest
    else:
        o_ref = rest[3]
        for src, dst in zip(rest[:3], rest[4:]):
            dst[0] = src[0, 0].astype(BF16)
    D = wg_ref.shape[1]
    ff = wg_ref.shape[2]
    x = x_ref[0, 0, :, :D]
    pieces = x_ref[0, 0, :, D:].astype(F32)
    lane = lax.broadcasted_iota(jnp.int32, pieces.shape, 1)
    mine = ((lane & (N_EXPERTS - 1)) == pl.program_id(0)) & (lane < 3 * N_EXPERTS)
    gate = jnp.sum(jnp.where(mine, pieces, 0.0), axis=-1, keepdims=True)
    chunk = 512
    acc = jnp.zeros((x.shape[0], D), F32)
    for j in range(ff // chunk):
        cols = slice(j * chunk, (j + 1) * chunk)
        g = _dot(x, wg_ref[0, :, cols])
        u = _dot(x, wu_ref[0, :, cols])
        hid = (g * jax.nn.sigmoid(g) * u).astype(BF16)
        acc = acc + _dot(hid, wd_ref[0, cols, :])
    o_ref[0, 0] = (acc * gate).astype(BF16)


def _ffn(xin, wg, wu, wd, next_f32=None, next_layer=None):
    B, E, S, W = xin.shape
    D, FF = wg.shape[1], wg.shape[2]
    in_specs = [pl.BlockSpec((1, 1, S, W), lambda e, b: (b, e, 0, 0)),
                pl.BlockSpec((1, D, FF), lambda e, b: (e, 0, 0)),
                pl.BlockSpec((1, D, FF), lambda e, b: (e, 0, 0)),
                pl.BlockSpec((1, FF, D), lambda e, b: (e, 0, 0))]
    out_shape = [jax.ShapeDtypeStruct((B, E, S, D), BF16)]
    out_specs = [pl.BlockSpec((1, 1, S, D), lambda e, b: (b, e, 0, 0))]
    args = [xin, wg, wu, wd]
    if next_f32 is not None:
        for w in next_f32:
            assert w.shape[2] % (B * SLOT_ALIGN) == 0
            rows, cols = w.shape[2] // B, w.shape[3]
            in_specs.append(pl.BlockSpec((1, 1, rows, cols), lambda e, b: (next_layer, e, b, 0)))
            out_shape.append(jax.ShapeDtypeStruct(w.shape[1:], BF16))
            out_specs.append(pl.BlockSpec((1, rows, cols), lambda e, b: (e, b, 0)))
            args.append(w)
    out = pl.pallas_call(
        _ffn_kernel,
        out_shape=out_shape,
        grid=(E, B),
        in_specs=in_specs,
        out_specs=out_specs,
        compiler_params=_params(2),
        name="expert_ffn",
    )(*args)
    return out[0] if next_f32 is None else out


def _combine_kernel(first_s, npass_s, slot_ref, first_ref, expand_ref, y_ref, x_ref, g2_ref,
                    *rest, n_tiles, out_tiles):
    o_ref = rest[-1]
    b, step = pl.program_id(0), pl.program_id(1)
    cap_total = y_ref.shape[2]
    lane_row = (lax.broadcasted_iota(jnp.int32, (TILE, N_EXPERTS * SLOT_WIN), 1)
                & (SLOT_WIN - 1)).astype(F32)

    def one_tile(i, t):
        rows_out = slice(i * TILE, (i + 1) * TILE)
        o_ref[0, rows_out, :] = x_ref[0, rows_out, :]
        slot = slot_ref[0, rows_out, :]
        first = first_ref[0, i]
        g2 = jnp.where(t == n_tiles - 1, g2_ref[0, 1], g2_ref[0, 0])

        def one_pass(k, carry):
            rel = _window_rel(slot, first, k.astype(F32), cap_total).astype(BF16)
            onehot = (_dot(rel, expand_ref[...]) == lane_row).astype(BF16)
            rows = jnp.concatenate(
                [y_ref[0, e, pl.ds(_window_begin(first_s[(b * n_tiles + t) * N_EXPERTS + e], k,
                                                 cap_total), SLOT_WIN), :]
                 for e in range(N_EXPERTS)], axis=0)
            o_ref[0, rows_out, :] += g2 * _dot(onehot, rows)
            return carry

        lax.fori_loop(0, npass_s[b * n_tiles + t], one_pass, 0)
        if len(rest) == 2:
            x = o_ref[0, rows_out, :]
            ms = jnp.mean(x * x, axis=-1, keepdims=True)
            o_ref[0, rows_out, :] = x * lax.rsqrt(ms + RMS_EPS) * rest[0][...]

    for i in range(TILES_PER_STEP):
        t = step * TILES_PER_STEP + i
        pl.when(t < out_tiles)(functools.partial(one_tile, i, t))


def _combine(y, slot, first, n_pass, xa, gate2, final_g=None):
    B, S, D = xa.shape
    nt = S // TILE
    E, cap_total = y.shape[1], y.shape[2]
    rows = TILES_PER_STEP * TILE
    expand = np.repeat(np.eye(E, dtype=np.float32), SLOT_WIN, axis=1)
    in_specs = [pl.BlockSpec((1, rows, E), lambda b, t, *_: (b, t, 0)),
                pl.BlockSpec((1, TILES_PER_STEP, 1, E), lambda b, t, *_: (b, t, 0, 0)),
                pl.BlockSpec((E, E * SLOT_WIN), lambda b, t, *_: (0, 0)),
                pl.BlockSpec((1, E, cap_total, D), lambda b, t, *_: (b, 0, 0, 0)),
                pl.BlockSpec((1, rows, D), lambda b, t, *_: (b, t, 0)),
                pl.BlockSpec((1, 2, 1, D), lambda b, t, *_: (b, 0, 0, 0))]
    args = [slot, first.astype(F32).reshape(B, nt, 1, E), jnp.asarray(expand, BF16), y, xa, gate2]
    out_tiles = nt
    if final_g is not None:
        in_specs.append(pl.BlockSpec((1, D), lambda b, t, *_: (0, 0)))
        args.append(final_g)
        out_tiles = nt - 1
    grid_spec = pltpu.PrefetchScalarGridSpec(
        num_scalar_prefetch=2,
        grid=(B, pl.cdiv(out_tiles, TILES_PER_STEP)),
        in_specs=in_specs,
        out_specs=pl.BlockSpec((1, rows, D), lambda b, t, *_: (b, t, 0)),
    )
    return pl.pallas_call(
        functools.partial(_combine_kernel, n_tiles=nt, out_tiles=out_tiles),
        out_shape=jax.ShapeDtypeStruct((B, out_tiles * TILE, D), F32),
        grid_spec=grid_spec,
        compiler_params=_params(2),
        name="combine",
    )(first.reshape(-1), n_pass.reshape(-1), *args)


def _scale_q(w, nq):
    return jnp.concatenate([w[:, :nq] * (HEAD_DIM ** -0.5 * LOG2E), w[:, nq:]], axis=1)


def _win_weights(w_in):
    D = w_in.shape[0]
    nq = 16 * HEAD_DIM
    nk = A_KV_HEADS * HEAD_DIM
    w = _scale_q(w_in, nq)
    dup = lambda m: jnp.concatenate([m.reshape(D, A_KV_HEADS, 1, HEAD_DIM)] * 2,
                                    axis=2).reshape(D, 2 * nk)
    return jnp.concatenate([w[:, :nq], dup(w[:, nq:nq + nk]), dup(w[:, nq + nk:])],
                           axis=1).astype(BF16)


def _router_weights(rw):
    D, E = rw.shape
    return jnp.concatenate([rw, rw, rw, jnp.zeros((D, GATE_COLS - 3 * E), rw.dtype)], axis=1)


def kernel(x, c, ctx, c_ctx, ada_w, ada_b, norm1_g, norm2_g, final_g, win_w_in, win_w_out,
           win_sink, diff_w_in, diff_w_out, diff_lambda, diff_subln_g, na_w_in, na_w_out,
           na_rpb, router_w, w_gate, w_up, w_down):
    B, T, D = x.shape
    L = ctx.shape[1]
    S = T + L
    depth = ada_w.shape[0]
    assert L == TILE and T % ROW_TILE == 0 and (T // GRID_W) >= NA_WIN_ROWS
    cs = jnp.concatenate([c, c_ctx[None, :]], axis=0)
    rope = _rope_tables(T, L)
    experts = [w[0].astype(BF16) for w in (w_gate, w_up, w_down)]
    xa = None
    for i in range(depth):
        kind = i % N_MIXERS
        slot = i // N_MIXERS
        mod = _ada(cs, ada_w[i], ada_b[i][None, :])
        mod = jnp.stack([mod[:B], jnp.broadcast_to(mod[B:], (B, 6 * D))], axis=1)
        mod = mod.reshape(B, 2, 1, 6, D)
        sh1, sc1, g1, sh2, sc2, g2 = [mod[:, :, :, k, :] for k in range(6)]
        if kind == 0:
            w_in = _win_weights(win_w_in[slot])
            n_rope = (16 + 2 * A_KV_HEADS) * HEAD_DIM
            w_out = win_w_out[slot]
        elif kind == 1:
            w_in = _scale_q(diff_w_in[slot], 16 * HEAD_DIM).astype(BF16)
            n_rope = 32 * HEAD_DIM
            w_out = diff_w_out[slot]
        else:
            w_in = _scale_q(na_w_in[slot], 16 * HEAD_DIM).astype(BF16)
            n_rope = 0
            w_out = na_w_out[slot]
        if i == 0:
            qkv, xa = _norm_proj((x, ctx), norm1_g[i][None, :], sh1, sc1, w_in, rope, n_rope)
        else:
            qkv = _norm_proj((xa,), norm1_g[i][None, :], sh1, sc1, w_in, rope, n_rope)
        if kind == 0:
            y = _attn_win(qkv, win_sink[slot], T)
        elif kind == 1:
            lambda_init = 0.8 - 0.6 * math.exp(-0.3 * i)
            lp = diff_lambda[slot]
            lam = (jnp.exp(jnp.sum(lp[0] * lp[1])) - jnp.exp(jnp.sum(lp[2] * lp[3]))
                   + lambda_init).reshape(1)
            y = _attn_diff(qkv, lam, diff_subln_g[slot][None, :], T, 1.0 - lambda_init)
        else:
            y = _attn_na(qkv, _na_bias_tables(na_rpb[slot], T), _na_bias_bound(na_rpb[slot]), T)
        xa, hx, aff = _out_router(y, w_out.astype(BF16), xa, g1, norm2_g[i][None, :],
                                  sh2, sc2, _router_weights(router_w[i]))
        tok_slot, start, cnt = _route(aff, T)
        first, n_pass = _slot_plan(start, cnt)
        cap_total = CAPACITY_FACTOR * S // N_EXPERTS
        xin = _dispatch(hx, tok_slot, first, n_pass, cap_total)
        last = i == depth - 1
        if last:
            ye = _ffn(xin, *experts)
        else:
            ye, *experts = _ffn(xin, *experts, (w_gate, w_up, w_down), i + 1)
        xa = _combine(ye, tok_slot, first, n_pass, xa, g2, final_g[None, :] if last else None)
    return xa
```

```python
import functools
import math

import numpy as np
import jax
import jax.numpy as jnp
from jax import lax
from jax.experimental import pallas as pl
from jax.experimental.pallas import tpu as pltpu

HEAD_DIM = 64
LANES = 128
GRID_W = 64
NA_ROWS = 8
NA_COLS = 16
A_WINDOW = 128
A_KV_HEADS = 4
N_MIXERS = 3
ROPE_BASE = 10000.0
ROPE_AXIS_DIM = HEAD_DIM // 2
N_EXPERTS = 16
CAPACITY_FACTOR = 2
RMS_EPS = 1e-6
NEG_INF = -1e30
LOG2E = math.log2(math.e)
TILE = 256
ROW_TILE = 2 * TILE
NA_TILE_ROWS = TILE // GRID_W
NA_WIN_ROWS = NA_TILE_ROWS + NA_ROWS
VMEM_LIMIT = 56 * 1024 * 1024
SLOT_WIN = 64
SLOT_ALIGN = 16
GATE_COLS = LANES

BF16 = jnp.bfloat16
F32 = jnp.float32


def _params(n_grid):
    return pltpu.CompilerParams(
        dimension_semantics=("arbitrary",) * n_grid, vmem_limit_bytes=VMEM_LIMIT)


def _split_bf16(a):
    hi = a.astype(BF16)
    lo = (a - hi.astype(F32)).astype(BF16)
    return hi, lo


def _dot(a, b):
    return jnp.dot(a, b, preferred_element_type=F32)


def _dot_nt(a, b):
    return lax.dot_general(a, b, (((1,), (1,)), ((), ())), preferred_element_type=F32)


def _dot_tn(a, b):
    return lax.dot_general(a, b, (((0,), (0,)), ((), ())), preferred_element_type=F32)


def _ada_kernel(c_ref, w_ref, b_ref, o_ref):
    c = c_ref[...]
    a = c * jax.nn.sigmoid(c)
    a_hi, a_lo = _split_bf16(a)
    w_hi, w_lo = _split_bf16(w_ref[...])
    o_ref[...] = _dot(a_hi, w_hi) + _dot(a_lo, w_hi) + _dot(a_hi, w_lo) + b_ref[...]


def _ada(cs, w, b):
    R, D = cs.shape
    N = w.shape[1]
    tn = 1024
    return pl.pallas_call(
        _ada_kernel,
        out_shape=jax.ShapeDtypeStruct((R, N), F32),
        grid=(N // tn,),
        in_specs=[pl.BlockSpec((R, D), lambda j: (0, 0)),
                  pl.BlockSpec((D, tn), lambda j: (0, j)),
                  pl.BlockSpec((1, tn), lambda j: (0, j))],
        out_specs=pl.BlockSpec((R, tn), lambda j: (0, j)),
        compiler_params=_params(1),
        name="ada",
    )(cs, w, b)


def _rms_mod(x, g, shift, scale):
    ms = jnp.mean(x * x, axis=-1, keepdims=True)
    y = x * lax.rsqrt(ms + RMS_EPS) * g
    return y * (1.0 + scale) + shift


def _norm_proj_kernel(*refs, n_rope, joins_streams):
    if joins_streams:
        x_ref, c_ref, g_ref, sh_ref, sc_ref, w_ref, cos_ref, sa_ref, sb_ref, o_ref, xa_ref = refs
        is_ctx = pl.program_id(1) == pl.num_programs(1) - 1
        ctx_rows = jnp.concatenate([c_ref[0]] * (ROW_TILE // TILE), axis=0)
        x = jnp.where(is_ctx, ctx_rows, x_ref[0])
        xa_ref[0] = x
    else:
        x_ref, g_ref, sh_ref, sc_ref, w_ref, cos_ref, sa_ref, sb_ref, o_ref = refs
        x = x_ref[0]
    h = _rms_mod(x, g_ref[...], sh_ref[0, 0], sc_ref[0, 0]).astype(BF16)
    n_cols = w_ref.shape[1]
    chunk = 512
    for j in range(n_cols // chunk):
        acc = _dot(h, w_ref[:, j * chunk:(j + 1) * chunk])
        for t in range(chunk // LANES):
            col = j * chunk + t * LANES
            a = acc[:, t * LANES:(t + 1) * LANES]
            if col < n_rope:
                a = (a * cos_ref[...]
                     + pltpu.roll(a, LANES - 16, 1) * sa_ref[...]
                     + pltpu.roll(a, 16, 1) * sb_ref[...])
            o_ref[0, :, col:col + LANES] = a.astype(BF16)


def _norm_proj(streams, g, shift, scale, w, rope, n_rope):
    joins = len(streams) == 2
    B, _, D = streams[0].shape
    S = sum(a.shape[1] for a in streams)
    N = w.shape[1]
    nt = pl.cdiv(S, ROW_TILE)
    mod_spec = pl.BlockSpec((1, 1, 1, D), lambda b, t: (b, t // (nt - 1), 0, 0))
    rope_spec = pl.BlockSpec((ROW_TILE, LANES), lambda b, t: (t, 0))
    tile_spec = pl.BlockSpec((1, ROW_TILE, D), lambda b, t: (b, t, 0))
    if joins:
        stream_specs = [pl.BlockSpec((1, ROW_TILE, D), lambda b, t: (b, jnp.minimum(t, nt - 2), 0)),
                        pl.BlockSpec((1, TILE, D), lambda b, t: (b, 0, 0))]
    else:
        stream_specs = [tile_spec]
    qkv_shape = jax.ShapeDtypeStruct((B, S, N), BF16)
    qkv_spec = pl.BlockSpec((1, ROW_TILE, N), lambda b, t: (b, t, 0))
    return pl.pallas_call(
        functools.partial(_norm_proj_kernel, n_rope=n_rope, joins_streams=joins),
        out_shape=(qkv_shape, jax.ShapeDtypeStruct((B, S, D), F32)) if joins else qkv_shape,
        grid=(B, nt),
        in_specs=stream_specs + [pl.BlockSpec((1, D), lambda b, t: (0, 0)),
                                 mod_spec, mod_spec,
                                 pl.BlockSpec((D, N), lambda b, t: (0, 0)),
                                 rope_spec, rope_spec, rope_spec],
        out_specs=(qkv_spec, tile_spec) if joins else qkv_spec,
        compiler_params=_params(2),
        name="norm_proj",
    )(*streams, g, shift, scale, w, *rope)


def _rope_tables(T, L):
    t = np.arange(T)
    pos = np.stack([t // GRID_W, t % GRID_W], axis=0).astype(np.float32)
    inv = (1.0 / (ROPE_BASE ** (np.arange(0, ROPE_AXIS_DIM, 2, dtype=np.float32)
                                / ROPE_AXIS_DIM))).astype(np.float32)
    d = np.arange(LANES) % HEAD_DIM
    axis = d // ROPE_AXIS_DIM
    half = (d % ROPE_AXIS_DIM) // (ROPE_AXIS_DIM // 2)
    freq = d % (ROPE_AXIS_DIM // 2)
    ang = jnp.asarray(pos[axis].T) * jnp.asarray(inv[freq])[None, :]
    cos = jnp.cos(ang)
    sin = jnp.sin(ang)
    first = jnp.asarray(half == 0)[None, :]
    sa = jnp.where(first, -sin, 0.0)
    sb = jnp.where(first, 0.0, sin)
    pad = lambda a, v: jnp.concatenate([a, jnp.full((L, LANES), v, F32)], axis=0)
    return pad(cos, 1.0), pad(sa, 0.0), pad(sb, 0.0)


def _half_masks(q):
    lane = lax.broadcasted_iota(jnp.int32, q.shape, 1)
    zero = jnp.zeros_like(q)
    return jnp.where(lane < HEAD_DIM, q, zero), jnp.where(lane >= HEAD_DIM, q, zero)


def _merge_halves(o_first, o_second):
    lane = lax.broadcasted_iota(jnp.int32, o_first.shape, 1)
    return jnp.where(lane < HEAD_DIM, o_first, o_second)


SHIFT_SLACK = 1.02
L_MIN = 2.0 ** -64


def _head_bounds_row(qs):
    qf = qs.astype(F32)
    n2 = jnp.sum(qf * qf, axis=-1, keepdims=True)
    return jnp.concatenate(
        [jnp.broadcast_to(jnp.max(n2[r:r + TILE], axis=0, keepdims=True), (1, TILE))
         for r in range(0, qs.shape[0], TILE)], axis=1)


def _store_key_bounds(kmax_ref, j, k):
    kf = k.astype(F32)
    same_head = ((lax.broadcasted_iota(jnp.int32, (LANES, LANES), 0) < HEAD_DIM)
                 == (lax.broadcasted_iota(jnp.int32, (LANES, LANES), 1) < HEAD_DIM)).astype(BF16)
    n2 = _dot((kf * kf).astype(BF16), same_head)
    kmax_ref[j] = jnp.broadcast_to(jnp.max(n2, axis=0, keepdims=True), kmax_ref.shape[1:])


def _redo_if_underflow(attend, *args):
    den_min = attend(*args, exact=False)

    @pl.when(den_min[0, 0] < L_MIN)
    def _():
        attend(*args, exact=True)


WIN_KV_PER_STEP = 4


def _win_kernel(sink_ref, q_ref, k_ref, v_ref, o_ref, kmax_ref, *, T, win):
    hb = pl.program_id(1)
    t = pl.program_id(2)
    n_x = T // TILE
    is_x = t < n_x
    heads = [slice(j * LANES, (j + 1) * LANES) for j in range(WIN_KV_PER_STEP)]

    @pl.when(t == 0)
    def _():
        for j, cols in enumerate(heads):
            _store_key_bounds(kmax_ref, j, k_ref[0, :, cols])

    start = pl.multiple_of(jnp.clip(t * TILE - A_WINDOW, 0, T - win), LANES)
    kpos = start + lax.broadcasted_iota(jnp.int32, (win, TILE), 0)
    qpos = t * TILE + lax.broadcasted_iota(jnp.int32, (win, TILE), 1)
    band = (jnp.abs(qpos - kpos) <= A_WINDOW) & is_x
    band = jnp.concatenate([band] * 4, axis=1)

    def attend(exact):
        queries, scores = [], []
        for j, cols in enumerate(heads):
            q = q_ref[0, :, 2 * j * LANES:2 * (j + 1) * LANES]
            qa, qb = _half_masks(q[:, :LANES])
            qc, qd = _half_masks(q[:, LANES:])
            qs = jnp.concatenate([qa, qb, qc, qd], axis=0)
            queries.append(qs)
            scores.append((_dot_nt(k_ref[0, T:, cols], qs),
                           _dot_nt(k_ref[0, pl.ds(start, win), cols], qs)))
        den_min = None
        for j, cols in enumerate(heads):
            s_c, s_w = scores[j]
            head0 = (hb * WIN_KV_PER_STEP + j) * 4
            s_w = jnp.where(band, s_w, NEG_INF)
            sink = jnp.concatenate(
                [jnp.full((1, TILE), sink_ref[head0 + g] * LOG2E, F32) for g in range(4)], axis=1)
            if exact:
                m = jnp.maximum(jnp.max(s_c, axis=0, keepdims=True),
                                jnp.max(s_w, axis=0, keepdims=True))
            else:
                m = jnp.sqrt(_head_bounds_row(queries[j]) * kmax_ref[j, :1, :1]) * SHIFT_SLACK
            m = jnp.maximum(m, sink)
            ec = jnp.exp2(s_c - m)
            ew = jnp.exp2(s_w - m)
            den = (jnp.sum(ec, axis=0, keepdims=True) + jnp.sum(ew, axis=0, keepdims=True)
                   + jnp.exp2(sink - m))
            o = ((_dot_tn(v_ref[0, T:, cols], ec.astype(BF16))
                  + _dot_tn(v_ref[0, pl.ds(start, win), cols], ew.astype(BF16))) * (1.0 / den)).T
            first = 2 * j * LANES
            o_ref[0, :, first:first + LANES] = _merge_halves(
                o[:TILE], o[TILE:2 * TILE]).astype(BF16)
            o_ref[0, :, first + LANES:first + 2 * LANES] = _merge_halves(
                o[2 * TILE:3 * TILE], o[3 * TILE:]).astype(BF16)
            low = jnp.min(den, axis=1, keepdims=True)
            den_min = low if den_min is None else jnp.minimum(den_min, low)
        return den_min

    _redo_if_underflow(attend)


def _attn_win(qkv, sink, T):
    B, S, _ = qkv.shape
    nt = S // TILE
    nq = 16 * HEAD_DIM
    win = TILE + 2 * A_WINDOW
    hb = A_KV_HEADS // WIN_KV_PER_STEP
    wq = WIN_KV_PER_STEP * 2 * LANES
    wk = WIN_KV_PER_STEP * LANES
    kb = nq // wk
    return pl.pallas_call(
        functools.partial(_win_kernel, T=T, win=win),
        out_shape=jax.ShapeDtypeStruct((B, S, nq), BF16),
        grid=(B, hb, nt),
        in_specs=[pl.BlockSpec(memory_space=pltpu.SMEM),
                  pl.BlockSpec((1, TILE, wq), lambda b, h, t: (b, t, h)),
                  pl.BlockSpec((1, S, wk), lambda b, h, t: (b, 0, kb + h)),
                  pl.BlockSpec((1, S, wk), lambda b, h, t: (b, 0, kb + hb + h))],
        out_specs=pl.BlockSpec((1, TILE, wq), lambda b, h, t: (b, t, h)),
        scratch_shapes=[pltpu.VMEM((WIN_KV_PER_STEP, 8, LANES), F32)],
        compiler_params=_params(3),
        name="attn_win",
    )(sink, qkv, qkv, qkv)


DIFF_HEADS_PER_STEP = 2


def _diff_kernel(lam_ref, q_ref, k_ref, v_ref, g_ref, o_ref, kmax_ref, *, T, out_scale):
    t = pl.program_id(2)
    n_x = T // TILE
    lam = lam_ref[0]
    heads = [slice(j * LANES, (j + 1) * LANES) for j in range(DIFF_HEADS_PER_STEP)]

    @pl.when(t == 0)
    def _():
        for j, cols in enumerate(heads):
            _store_key_bounds(kmax_ref, j, k_ref[0, :, cols])

    def attend(keys, exact):
        queries, scores = [], []
        for cols in heads:
            qa, qb = _half_masks(q_ref[0, :, cols])
            qs = jnp.concatenate([qa, qb], axis=0)
            queries.append(qs)
            scores.append(_dot_nt(qs, k_ref[0, keys, cols]))
        l_min = None
        for j, cols in enumerate(heads):
            s = scores[j]
            if exact:
                m = jnp.max(s, axis=-1, keepdims=True)
            else:
                qf = queries[j].astype(F32)
                row = lax.broadcasted_iota(jnp.int32, (2 * TILE, 1), 0)
                kmax = jnp.where(row < TILE, kmax_ref[j, :1, :1],
                                 kmax_ref[j, :1, HEAD_DIM:HEAD_DIM + 1])
                m = jnp.sqrt(jnp.sum(qf * qf, axis=-1, keepdims=True) * kmax) * SHIFT_SLACK
            e = jnp.exp2(s - m)
            l = jnp.sum(e, axis=-1, keepdims=True)
            a = e[:TILE] - e[TILE:] * (lam * l[:TILE] / l[TILE:])
            o = _dot(a.astype(BF16), v_ref[0, keys, cols]) / l[:TILE]
            ms = jnp.mean(o * o, axis=-1, keepdims=True)
            y = o * lax.rsqrt(ms + RMS_EPS) * g_ref[...] * out_scale
            o_ref[0, :, cols] = y.astype(BF16)
            low = jnp.min(l, axis=0, keepdims=True)
            l_min = low if l_min is None else jnp.minimum(l_min, low)
        return l_min

    @pl.when(t < n_x)
    def _():
        _redo_if_underflow(attend, slice(None))

    @pl.when(t >= n_x)
    def _():
        _redo_if_underflow(attend, slice(T, None))


def _attn_diff(qkv, lam, subln_g, T, out_scale):
    B, S, _ = qkv.shape
    nt = S // TILE
    H = 8
    hb = H // DIFF_HEADS_PER_STEP
    w = DIFF_HEADS_PER_STEP * LANES
    return pl.pallas_call(
        functools.partial(_diff_kernel, T=T, out_scale=out_scale),
        out_shape=jax.ShapeDtypeStruct((B, S, H * LANES), BF16),
        grid=(B, hb, nt),
        in_specs=[pl.BlockSpec(memory_space=pltpu.SMEM),
                  pl.BlockSpec((1, TILE, w), lambda b, h, t: (b, t, h)),
                  pl.BlockSpec((1, S, w), lambda b, h, t: (b, 0, hb + h)),
                  pl.BlockSpec((1, S, w), lambda b, h, t: (b, 0, 2 * hb + h)),
                  pl.BlockSpec((1, LANES), lambda b, h, t: (0, 0))],
        out_specs=pl.BlockSpec((1, TILE, w), lambda b, h, t: (b, t, h)),
        scratch_shapes=[pltpu.VMEM((DIFF_HEADS_PER_STEP, 8, LANES), F32)],
        compiler_params=_params(3),
        name="attn_diff",
    )(lam, qkv, qkv, qkv, subln_g)


NA_PAIRS_PER_STEP = 2


def _na_kernel(q_ref, k_ref, v_ref, bias_ref, bmax_ref, o_ref, kmax_ref, *, T):
    t = pl.program_id(2)
    n_x = T // TILE
    rows = T // GRID_W
    win = NA_WIN_ROWS * GRID_W
    pairs = [slice(j * LANES, (j + 1) * LANES) for j in range(NA_PAIRS_PER_STEP)]

    @pl.when(t == 0)
    def _():
        for j, cols in enumerate(pairs):
            _store_key_bounds(kmax_ref, j, k_ref[0, :, cols])

    cls = jnp.where(t >= n_x, 3, jnp.where(t == 0, 0, jnp.where(t == n_x - 1, 2, 1)))
    row0 = jnp.clip(t * NA_TILE_ROWS - NA_ROWS // 2, 0, rows - NA_WIN_ROWS)
    start = pl.multiple_of(row0 * GRID_W, GRID_W)

    def attend(exact):
        queries, scores = [], []
        for cols in pairs:
            qa, qb = _half_masks(q_ref[0, :, cols])
            qs = jnp.concatenate([qa, qb], axis=0)
            queries.append(qs)
            scores.append((_dot_nt(k_ref[0, T:, cols], qs),
                           _dot_nt(k_ref[0, pl.ds(start, win), cols], qs)))
        den_min = None
        for j, cols in enumerate(pairs):
            s_c, s_w = scores[j]
            s_w = s_w + bias_ref[cls, j]
            if exact:
                m = jnp.maximum(jnp.max(s_c, axis=0, keepdims=True),
                                jnp.max(s_w, axis=0, keepdims=True))
            else:
                lane = lax.broadcasted_iota(jnp.int32, (1, 2 * TILE), 1)
                kmax = jnp.where(lane < TILE, kmax_ref[j, :1, :1],
                                 kmax_ref[j, :1, HEAD_DIM:HEAD_DIM + 1])
                m = jnp.sqrt(_head_bounds_row(queries[j]) * kmax) * SHIFT_SLACK + bmax_ref[j]
            ec = jnp.exp2(s_c - m)
            ew = jnp.exp2(s_w - m)
            den = jnp.sum(ec, axis=0, keepdims=True) + jnp.sum(ew, axis=0, keepdims=True)
            o = ((_dot_tn(v_ref[0, T:, cols], ec.astype(BF16))
                  + _dot_tn(v_ref[0, pl.ds(start, win), cols], ew.astype(BF16))) * (1.0 / den)).T
            o_ref[0, :, cols] = _merge_halves(o[:TILE], o[TILE:]).astype(BF16)
            low = jnp.min(den, axis=1, keepdims=True)
            den_min = low if den_min is None else jnp.minimum(den_min, low)
        return den_min

    _redo_if_underflow(attend)


def _attn_na(qkv, bias, bias_max, T):
    B, S, _ = qkv.shape
    nt = S // TILE
    n = NA_PAIRS_PER_STEP
    hb = 8 // n
    w = n * LANES
    win = NA_WIN_ROWS * GRID_W
    return pl.pallas_call(
        functools.partial(_na_kernel, T=T),
        out_shape=jax.ShapeDtypeStruct((B, S, 8 * LANES), BF16),
        grid=(hb, B, nt),
        in_specs=[pl.BlockSpec((1, TILE, w), lambda h, b, t: (b, t, h)),
                  pl.BlockSpec((1, S, w), lambda h, b, t: (b, 0, hb + h)),
                  pl.BlockSpec((1, S, w), lambda h, b, t: (b, 0, 2 * hb + h)),
                  pl.BlockSpec((4, n, win, 2 * TILE), lambda h, b, t: (0, h, 0, 0)),
                  pl.BlockSpec((n, 1, 2 * TILE), lambda h, b, t: (h, 0, 0))],
        out_specs=pl.BlockSpec((1, TILE, w), lambda h, b, t: (b, t, h)),
        scratch_shapes=[pltpu.VMEM((NA_PAIRS_PER_STEP, 8, LANES), F32)],
        compiler_params=_params(3),
        name="attn_na",
    )(qkv, qkv, qkv, bias, bias_max)


def _na_bias_bound(rpb):
    top = jnp.maximum(jnp.max(rpb, axis=(1, 2)), 0.0) * LOG2E
    return jnp.repeat(top.reshape(-1, 2), TILE, axis=1)[:, None, :]


def _na_bias_tables(rpb, T):
    rows = T // GRID_W
    H = rpb.shape[0]
    win = NA_WIN_ROWS * GRID_W
    hi = lax.Precision.HIGHEST
    pick = lambda idx, n: jnp.asarray(idx[..., None] == np.arange(n), F32)
    c = np.arange(GRID_W)
    cs = np.clip(c - NA_COLS // 2, 0, GRID_W - NA_COLS)
    valid_c = (c[None, :] >= cs[:, None]) & (c[None, :] < cs[:, None] + NA_COLS)
    bidx_c = np.clip(c[None, :] - c[:, None] + NA_COLS - 1, 0, 2 * NA_COLS - 2)
    by_col = jnp.einsum('hrd,ckd->hrck', rpb, pick(bidx_c, 2 * NA_COLS - 1), precision=hi)
    i = np.arange(NA_TILE_ROWS)
    j = np.arange(NA_WIN_ROWS)
    tables = []
    for r0 in (0, NA_TILE_ROWS, rows - NA_TILE_ROWS):
        s = int(np.clip(r0 - NA_ROWS // 2, 0, rows - NA_WIN_ROWS))
        r = r0 + i
        rs = np.clip(r - NA_ROWS // 2, 0, rows - NA_ROWS)
        kr = s + j
        valid_r = (kr[None, :] >= rs[:, None]) & (kr[None, :] < rs[:, None] + NA_ROWS)
        bidx_r = np.clip(kr[None, :] - r[:, None] + NA_ROWS - 1, 0, 2 * NA_ROWS - 2)
        vals = jnp.einsum('hrck,ijr->hjkic', by_col, pick(bidx_r, 2 * NA_ROWS - 1), precision=hi)
        valid = valid_r.T[:, None, :, None] & valid_c.T[None, :, None, :]
        vals = jnp.where(jnp.asarray(valid)[None], vals, NEG_INF).reshape(H // 2, 2, win, TILE)
        tables.append(vals.transpose(0, 2, 1, 3).reshape(H // 2, win, 2 * TILE))
    tables.append(jnp.full_like(tables[0], NEG_INF))
    return jnp.stack(tables, axis=0).astype(F32) * LOG2E


def _out_router_kernel(y_ref, w_ref, x_ref, g1_ref, n2_ref, sh_ref, sc_ref, rw_ref,
                       xo_ref, h_ref, aff_ref):
    D = x_ref.shape[2]
    x = x_ref[0] + g1_ref[0, 0] * _dot(y_ref[0], w_ref[...])
    xo_ref[0] = x
    h = _rms_mod(x, n2_ref[...], sh_ref[0, 0], sc_ref[0, 0])
    h_hi, h_lo = _split_bf16(h)
    n = h.shape[0]
    r_hi_lo = jnp.concatenate(_split_bf16(rw_ref[...]), axis=1)
    prod = _dot(jnp.concatenate([h_hi, h_lo], axis=0), r_hi_lo)
    logits = prod[:n, :GATE_COLS] + prod[:n, GATE_COLS:] + prod[n:, :GATE_COLS]
    lane = lax.broadcasted_iota(jnp.int32, logits.shape, 1)
    first = lane < N_EXPERTS
    m = jnp.max(jnp.where(first, logits, -jnp.inf), axis=-1, keepdims=True)
    e = jnp.exp(logits - m)
    aff = e / jnp.sum(jnp.where(first, e, 0.0), axis=-1, keepdims=True)
    aff_ref[0] = aff
    hi = aff.astype(BF16)
    rem = aff - hi.astype(F32)
    mid = rem.astype(BF16)
    lo = (rem - mid.astype(F32)).astype(BF16)
    zero = jnp.zeros_like(hi)
    pieces = jnp.where(first, hi, jnp.where(lane < 2 * N_EXPERTS, mid,
                                            jnp.where(lane < 3 * N_EXPERTS, lo, zero)))
    h_ref[0, :, :D] = h_hi
    h_ref[0, :, D:] = pieces


def _out_router(y, w_out, xa, gate1, n2g, shift2, scale2, rw3):
    B, S, D = xa.shape
    nt = pl.cdiv(S, ROW_TILE)
    mod_spec = pl.BlockSpec((1, 1, 1, D), lambda b, t: (b, t // (nt - 1), 0, 0))
    tile_spec = pl.BlockSpec((1, ROW_TILE, D), lambda b, t: (b, t, 0))
    return pl.pallas_call(
        _out_router_kernel,
        out_shape=(jax.ShapeDtypeStruct((B, S, D), F32),
                   jax.ShapeDtypeStruct((B, S, D + GATE_COLS), BF16),
                   jax.ShapeDtypeStruct((B, S, GATE_COLS), F32)),
        grid=(B, nt),
        in_specs=[tile_spec,
                  pl.BlockSpec((D, D), lambda b, t: (0, 0)),
                  tile_spec, mod_spec,
                  pl.BlockSpec((1, D), lambda b, t: (0, 0)),
                  mod_spec, mod_spec,
                  pl.BlockSpec((D, GATE_COLS), lambda b, t: (0, 0))],
        out_specs=(tile_spec,
                   pl.BlockSpec((1, ROW_TILE, D + GATE_COLS), lambda b, t: (b, t, 0)),
                   pl.BlockSpec((1, ROW_TILE, GATE_COLS), lambda b, t: (b, t, 0))),
        compiler_params=_params(2),
        name="out_router",
    )(y, w_out, xa, gate1, n2g, shift2, scale2, rw3)


def _route_kernel(aff_ref, slot_ref, start_ref, cnt_ref, *, T, cap_x, cap_c):
    S = aff_ref.shape[1]
    aff = aff_ref[0]
    bits = lax.bitcast_convert_type(aff[:, :N_EXPERTS], jnp.int32)
    ri = lax.broadcasted_iota(jnp.int32, (TILE, TILE), 0)
    ci = lax.broadcasted_iota(jnp.int32, (TILE, TILE), 1)
    ltri = (ri > ci).astype(BF16)
    eye = (lax.broadcasted_iota(jnp.int32, (N_EXPERTS, N_EXPERTS), 0)
           == lax.broadcasted_iota(jnp.int32, (N_EXPERTS, N_EXPERTS), 1))

    def count(mask):
        return jnp.sum(mask.astype(F32), axis=0, keepdims=True)

    for lo_row, hi_row, cap, base in ((0, T, cap_x, 0), (T, S, cap_c, cap_x)):
        b = bits[lo_row:hi_row]
        dense = lax.bitcast_convert_type(jnp.concatenate(
            [aff[r:r + TILE].T[:N_EXPERTS] for r in range(lo_row, hi_row, TILE)], axis=1),
            jnp.int32)

        def enough(cand):
            return jnp.sum((dense >= cand).astype(F32), axis=1, keepdims=True) >= cap

        def step(i, thr):
            low = 29 - 2 * i
            c1, c2, c3 = [thr | lax.shift_left(jnp.int32(d), low) for d in (1, 2, 3)]
            return jnp.where(enough(c3), c3, jnp.where(enough(c2), c2,
                                                       jnp.where(enough(c1), c1, thr)))

        thr = lax.fori_loop(0, 15, step, jnp.zeros((N_EXPERTS, 1), jnp.int32))
        thr = jnp.where(enough(thr | 1), thr | 1, thr)
        thr = jnp.max(jnp.where(eye, jnp.broadcast_to(thr, eye.shape), 0), axis=0,
                      keepdims=True)
        need = cap - count(b > thr)
        seen_eq = jnp.zeros((1, N_EXPERTS), F32)
        seen = jnp.zeros((1, N_EXPERTS), F32)
        for j in range((hi_row - lo_row) // TILE):
            blk = b[j * TILE:(j + 1) * TILE]
            gt = blk > thr
            eq = blk == thr
            eq_rank = _dot(ltri, eq.astype(BF16)) + seen_eq
            sel = gt | (eq & (eq_rank < need))
            pos = _dot(ltri, sel.astype(BF16)) + seen + base
            t = lo_row // TILE + j
            slot_ref[0, t * TILE:(t + 1) * TILE, :] = jnp.where(sel, pos, -1.0)
            n_sel = count(sel)
            start_ref[0, t:t + 1, :] = seen + base
            cnt_ref[0, t:t + 1, :] = n_sel
            seen_eq = seen_eq + count(eq)
            seen = seen + n_sel


def _route(aff, T):
    B, S, _ = aff.shape
    E = N_EXPERTS
    nt = S // TILE
    cap_x = CAPACITY_FACTOR * T // N_EXPERTS
    cap_c = CAPACITY_FACTOR * (S - T) // N_EXPERTS
    plan = jax.ShapeDtypeStruct((B, nt, E), F32)
    return pl.pallas_call(
        functools.partial(_route_kernel, T=T, cap_x=cap_x, cap_c=cap_c),
        out_shape=(jax.ShapeDtypeStruct((B, S, E), F32), plan, plan),
        grid=(B,),
        in_specs=[pl.BlockSpec((1, S, GATE_COLS), lambda b: (b, 0, 0))],
        out_specs=(pl.BlockSpec((1, S, E), lambda b: (b, 0, 0)),
                   pl.BlockSpec((1, nt, E), lambda b: (b, 0, 0)),
                   pl.BlockSpec((1, nt, E), lambda b: (b, 0, 0))),
        compiler_params=_params(1),
        name="route",
    )(aff)


def _slot_plan(start, cnt):
    start = start.astype(jnp.int32)
    cnt = cnt.astype(jnp.int32)
    first = (start // SLOT_ALIGN) * SLOT_ALIGN
    n_pass = jnp.max((start - first + cnt + SLOT_WIN - 1) // SLOT_WIN, axis=-1)
    return first, n_pass.astype(jnp.int32)


def _window_rel(slot, first, k, cap_total):
    nominal = first + k * SLOT_WIN
    begin = jnp.minimum(nominal, float(cap_total - SLOT_WIN))
    rel = slot - nominal
    return jnp.where((rel >= 0) & (rel < SLOT_WIN), rel + (nominal - begin), 255.0)


def _window_begin(first_s, k, cap_total):
    return pl.multiple_of(jnp.minimum(first_s + k * SLOT_WIN, cap_total - SLOT_WIN), SLOT_ALIGN)


TILES_PER_STEP = 2


def _dispatch_kernel(first_s, npass_s, slot_ref, first_ref, h_ref, o_ref, *, n_tiles):
    b, step = pl.program_id(0), pl.program_id(1)
    n_e = o_ref.shape[1]
    cap_total = o_ref.shape[2]

    @pl.when(step == 0)
    def _():
        o_ref[...] = jnp.zeros_like(o_ref)

    sub = lax.broadcasted_iota(jnp.int32, (SLOT_WIN, TILE), 0).astype(F32)
    eye = (lax.broadcasted_iota(jnp.int32, (n_e, N_EXPERTS), 0)
           == lax.broadcasted_iota(jnp.int32, (n_e, N_EXPERTS), 1)).astype(BF16)

    def one_tile(i, t):
        rows_in = slice(i * TILE, (i + 1) * TILE)
        slot = slot_ref[0, rows_in, :]
        first = first_ref[0, i]

        def one_pass(k, carry):
            rel = _window_rel(slot, first, k.astype(F32), cap_total).astype(BF16)
            rel_t = _dot_nt(eye, rel)
            onehot = jnp.concatenate(
                [(jnp.broadcast_to(rel_t[e:e + 1, :], (SLOT_WIN, TILE)) == sub).astype(BF16)
                 for e in range(n_e)], axis=0)
            rows = _dot(onehot, h_ref[0, rows_in, :])
            for e in range(n_e):
                begin = _window_begin(first_s[(b * n_tiles + t) * N_EXPERTS + e], k, cap_total)
                o_ref[0, e, pl.ds(begin, SLOT_WIN), :] += (
                    rows[e * SLOT_WIN:(e + 1) * SLOT_WIN].astype(BF16))
            return carry

        lax.fori_loop(0, npass_s[b * n_tiles + t], one_pass, 0)

    for i in range(TILES_PER_STEP):
        t = step * TILES_PER_STEP + i
        pl.when(t < n_tiles)(functools.partial(one_tile, i, t))


def _dispatch(hx, slot, first, n_pass, cap_total):
    B, S, W = hx.shape
    nt = S // TILE
    rows = TILES_PER_STEP * TILE
    grid_spec = pltpu.PrefetchScalarGridSpec(
        num_scalar_prefetch=2,
        grid=(B, pl.cdiv(nt, TILES_PER_STEP)),
        in_specs=[pl.BlockSpec((1, rows, N_EXPERTS), lambda b, t, *_: (b, t, 0)),
                  pl.BlockSpec((1, TILES_PER_STEP, 1, N_EXPERTS), lambda b, t, *_: (b, t, 0, 0)),
                  pl.BlockSpec((1, rows, W), lambda b, t, *_: (b, t, 0))],
        out_specs=pl.BlockSpec((1, N_EXPERTS, cap_total, W), lambda b, t, *_: (b, 0, 0, 0)),
    )
    return pl.pallas_call(
        functools.partial(_dispatch_kernel, n_tiles=nt),
        out_shape=jax.ShapeDtypeStruct((B, N_EXPERTS, cap_total, W), BF16),
        grid_spec=grid_spec,
        compiler_params=_params(2),
        name="dispatch",
    )(first.reshape(-1), n_pass.reshape(-1), slot,
      first.astype(F32).reshape(B, nt, 1, N_EXPERTS), hx)


def _ffn_kernel(x_ref, wg_ref, wu_ref, wd_ref, *rest):
    if len(rest) == 1:
        o_ref, = rest
    else:
        o_ref = rest[3]
        for src, dst in zip(rest[:3], rest[4:]):
            dst[0] = src[0, 0].astype(BF16)
    D = wg_ref.shape[1]
    ff = wg_ref.shape[2]
    x = x_ref[0, 0, :, :D]
    pieces = x_ref[0, 0, :, D:].astype(F32)
    lane = lax.broadcasted_iota(jnp.int32, pieces.shape, 1)
    mine = ((lane & (N_EXPERTS - 1)) == pl.program_id(0)) & (lane < 3 * N_EXPERTS)
    gate = jnp.sum(jnp.where(mine, pieces, 0.0), axis=-1, keepdims=True)
    chunk = 512
    acc = jnp.zeros((x.shape[0], D), F32)
    for j in range(ff // chunk):
        cols = slice(j * chunk, (j + 1) * chunk)
        g = _dot(x, wg_ref[0, :, cols])
        u = _dot(x, wu_ref[0, :, cols])
        hid = (g * jax.nn.sigmoid(g) * u).astype(BF16)
        acc = acc + _dot(hid, wd_ref[0, cols, :])
    o_ref[0, 0] = (acc * gate).astype(BF16)


def _ffn(xin, wg, wu, wd, next_f32=None, next_layer=None):
    B, E, S, W = xin.shape
    D, FF = wg.shape[1], wg.shape[2]
    in_specs = [pl.BlockSpec((1, 1, S, W), lambda e, b: (b, e, 0, 0)),
                pl.BlockSpec((1, D, FF), lambda e, b: (e, 0, 0)),
                pl.BlockSpec((1, D, FF), lambda e, b: (e, 0, 0)),
                pl.BlockSpec((1, FF, D), lambda e, b: (e, 0, 0))]
    out_shape = [jax.ShapeDtypeStruct((B, E, S, D), BF16)]
    out_specs = [pl.BlockSpec((1, 1, S, D), lambda e, b: (b, e, 0, 0))]
    args = [xin, wg, wu, wd]
    if next_f32 is not None:
        for w in next_f32:
            assert w.shape[2] % (B * SLOT_ALIGN) == 0
            rows, cols = w.shape[2] // B, w.shape[3]
            in_specs.append(pl.BlockSpec((1, 1, rows, cols), lambda e, b: (next_layer, e, b, 0)))
            out_shape.append(jax.ShapeDtypeStruct(w.shape[1:], BF16))
            out_specs.append(pl.BlockSpec((1, rows, cols), lambda e, b: (e, b, 0)))
            args.append(w)
    out = pl.pallas_call(
        _ffn_kernel,
        out_shape=out_shape,
        grid=(E, B),
        in_specs=in_specs,
        out_specs=out_specs,
        compiler_params=_params(2),
        name="expert_ffn",
    )(*args)
    return out[0] if next_f32 is None else out


def _combine_kernel(first_s, npass_s, slot_ref, first_ref, expand_ref, y_ref, x_ref, g2_ref,
                    *rest, n_tiles, out_tiles):
    o_ref = rest[-1]
    b, step = pl.program_id(0), pl.program_id(1)
    cap_total = y_ref.shape[2]
    lane_row = (lax.broadcasted_iota(jnp.int32, (TILE, N_EXPERTS * SLOT_WIN), 1)
                & (SLOT_WIN - 1)).astype(F32)

    def one_tile(i, t):
        rows_out = slice(i * TILE, (i + 1) * TILE)
        o_ref[0, rows_out, :] = x_ref[0, rows_out, :]
        slot = slot_ref[0, rows_out, :]
        first = first_ref[0, i]
        g2 = jnp.where(t == n_tiles - 1, g2_ref[0, 1], g2_ref[0, 0])

        def one_pass(k, carry):
            rel = _window_rel(slot, first, k.astype(F32), cap_total).astype(BF16)
            onehot = (_dot(rel, expand_ref[...]) == lane_row).astype(BF16)
            rows = jnp.concatenate(
                [y_ref[0, e, pl.ds(_window_begin(first_s[(b * n_tiles + t) * N_EXPERTS + e], k,
                                                 cap_total), SLOT_WIN), :]
                 for e in range(N_EXPERTS)], axis=0)
            o_ref[0, rows_out, :] += g2 * _dot(onehot, rows)
            return carry

        lax.fori_loop(0, npass_s[b * n_tiles + t], one_pass, 0)
        if len(rest) == 2:
            x = o_ref[0, rows_out, :]
            ms = jnp.mean(x * x, axis=-1, keepdims=True)
            o_ref[0, rows_out, :] = x * lax.rsqrt(ms + RMS_EPS) * rest[0][...]

    for i in range(TILES_PER_STEP):
        t = step * TILES_PER_STEP + i
        pl.when(t < out_tiles)(functools.partial(one_tile, i, t))


def _combine(y, slot, first, n_pass, xa, gate2, final_g=None):
    B, S, D = xa.shape
    nt = S // TILE
    E, cap_total = y.shape[1], y.shape[2]
    rows = TILES_PER_STEP * TILE
    expand = np.repeat(np.eye(E, dtype=np.float32), SLOT_WIN, axis=1)
    in_specs = [pl.BlockSpec((1, rows, E), lambda b, t, *_: (b, t, 0)),
                pl.BlockSpec((1, TILES_PER_STEP, 1, E), lambda b, t, *_: (b, t, 0, 0)),
                pl.BlockSpec((E, E * SLOT_WIN), lambda b, t, *_: (0, 0)),
                pl.BlockSpec((1, E, cap_total, D), lambda b, t, *_: (b, 0, 0, 0)),
                pl.BlockSpec((1, rows, D), lambda b, t, *_: (b, t, 0)),
                pl.BlockSpec((1, 2, 1, D), lambda b, t, *_: (b, 0, 0, 0))]
    args = [slot, first.astype(F32).reshape(B, nt, 1, E), jnp.asarray(expand, BF16), y, xa, gate2]
    out_tiles = nt
    if final_g is not None:
        in_specs.append(pl.BlockSpec((1, D), lambda b, t, *_: (0, 0)))
        args.append(final_g)
        out_tiles = nt - 1
    grid_spec = pltpu.PrefetchScalarGridSpec(
        num_scalar_prefetch=2,
        grid=(B, pl.cdiv(out_tiles, TILES_PER_STEP)),
        in_specs=in_specs,
        out_specs=pl.BlockSpec((1, rows, D), lambda b, t, *_: (b, t, 0)),
    )
    return pl.pallas_call(
        functools.partial(_combine_kernel, n_tiles=nt, out_tiles=out_tiles),
        out_shape=jax.ShapeDtypeStruct((B, out_tiles * TILE, D), F32),
        grid_spec=grid_spec,
        compiler_params=_params(2),
        name="combine",
    )(first.reshape(-1), n_pass.reshape(-1), *args)


def _scale_q(w, nq):
    return jnp.concatenate([w[:, :nq] * (HEAD_DIM ** -0.5 * LOG2E), w[:, nq:]], axis=1)


def _win_weights(w_in):
    D = w_in.shape[0]
    nq = 16 * HEAD_DIM
    nk = A_KV_HEADS * HEAD_DIM
    w = _scale_q(w_in, nq)
    dup = lambda m: jnp.concatenate([m.reshape(D, A_KV_HEADS, 1, HEAD_DIM)] * 2,
                                    axis=2).reshape(D, 2 * nk)
    return jnp.concatenate([w[:, :nq], dup(w[:, nq:nq + nk]), dup(w[:, nq + nk:])],
                           axis=1).astype(BF16)


def _router_weights(rw):
    D, E = rw.shape
    return jnp.concatenate([rw, rw, rw, jnp.zeros((D, GATE_COLS - 3 * E), rw.dtype)], axis=1)


def kernel(x, c, ctx, c_ctx, ada_w, ada_b, norm1_g, norm2_g, final_g, win_w_in, win_w_out,
           win_sink, diff_w_in, diff_w_out, diff_lambda, diff_subln_g, na_w_in, na_w_out,
           na_rpb, router_w, w_gate, w_up, w_down):
    B, T, D = x.shape
    L = ctx.shape[1]
    S = T + L
    depth = ada_w.shape[0]
    assert L == TILE and T % ROW_TILE == 0 and (T // GRID_W) >= NA_WIN_ROWS
    cs = jnp.concatenate([c, c_ctx[None, :]], axis=0)
    rope = _rope_tables(T, L)
    experts = [w[0].astype(BF16) for w in (w_gate, w_up, w_down)]
    xa = None
    for i in range(depth):
        kind = i % N_MIXERS
        slot = i // N_MIXERS
        mod = _ada(cs, ada_w[i], ada_b[i][None, :])
        mod = jnp.stack([mod[:B], jnp.broadcast_to(mod[B:], (B, 6 * D))], axis=1)
        mod = mod.reshape(B, 2, 1, 6, D)
        sh1, sc1, g1, sh2, sc2, g2 = [mod[:, :, :, k, :] for k in range(6)]
        if kind == 0:
            w_in = _win_weights(win_w_in[slot])
            n_rope = (16 + 2 * A_KV_HEADS) * HEAD_DIM
            w_out = win_w_out[slot]
        elif kind == 1:
            w_in = _scale_q(diff_w_in[slot], 16 * HEAD_DIM).astype(BF16)
            n_rope = 32 * HEAD_DIM
            w_out = diff_w_out[slot]
        else:
            w_in = _scale_q(na_w_in[slot], 16 * HEAD_DIM).astype(BF16)
            n_rope = 0
            w_out = na_w_out[slot]
        if i == 0:
            qkv, xa = _norm_proj((x, ctx), norm1_g[i][None, :], sh1, sc1, w_in, rope, n_rope)
        else:
            qkv = _norm_proj((xa,), norm1_g[i][None, :], sh1, sc1, w_in, rope, n_rope)
        if kind == 0:
            y = _attn_win(qkv, win_sink[slot], T)
        elif kind == 1:
            lambda_init = 0.8 - 0.6 * math.exp(-0.3 * i)
            lp = diff_lambda[slot]
            lam = (jnp.exp(jnp.sum(lp[0] * lp[1])) - jnp.exp(jnp.sum(lp[2] * lp[3]))
                   + lambda_init).reshape(1)
            y = _attn_diff(qkv, lam, diff_subln_g[slot][None, :], T, 1.0 - lambda_init)
        else:
            y = _attn_na(qkv, _na_bias_tables(na_rpb[slot], T), _na_bias_bound(na_rpb[slot]), T)
        xa, hx, aff = _out_router(y, w_out.astype(BF16), xa, g1, norm2_g[i][None, :],
                                  sh2, sc2, _router_weights(router_w[i]))
        tok_slot, start, cnt = _route(aff, T)
        first, n_pass = _slot_plan(start, cnt)
        cap_total = CAPACITY_FACTOR * S // N_EXPERTS
        xin = _dispatch(hx, tok_slot, first, n_pass, cap_total)
        last = i == depth - 1
        if last:
            ye = _ffn(xin, *experts)
        else:
            ye, *experts = _ffn(xin, *experts, (w_gate, w_up, w_down), i + 1)
        xa = _combine(ye, tok_slot, first, n_pass, xa, g2, final_g[None, :] if last else None)
    return xa
```
